```python
import math
import jax, jax.numpy as jnp
from jax import lax
import numpy as np

D_MODEL = 2048
BATCH = 8
SEQ = 8192
DEPTH = 4

N_MIXERS = 3
N_LAYERS_A = (DEPTH + 2) // 3
N_LAYERS_B = (DEPTH + 1) // 3
N_LAYERS_C = DEPTH // 3
RMS_EPS = 1e-6

ATTN_HEAD_DIM = 64
ATTN_HEADS = D_MODEL // ATTN_HEAD_DIM
ATTN_KV_HEADS = ATTN_HEADS // 8
ATTN_GROUP = ATTN_HEADS // ATTN_KV_HEADS
WINDOW = 128
ATTN_BLOCK = 128
N_HALO = WINDOW // ATTN_BLOCK
ATTN_KEYS = (2 * N_HALO + 1) * ATTN_BLOCK

SSM_GROUP_CH = 16
SSM_GROUPS = D_MODEL // SSM_GROUP_CH
SSM_STATE = 64
SSM_DIRS = 2
DT_MIN = 1e-3
DT_MAX = 1e-1

MLA_HEADS = 16
MLA_Q_LORA = 512
MLA_KV_LORA = 512
MLA_NOPE = 128
MLA_ROPE = 64
MLA_V = 128
MLA_BLOCK = 128
ROPE_THETA = 10000.0

D_FF = 5632
CONV_WIDTH = 3

kernel_name = "hybrid_swa_s5_mla_convffn_encoder"


def _rmsnorm(x, g):
    xf = x.astype(jnp.float32)
    y = xf * lax.rsqrt(jnp.mean(xf * xf, axis=-1, keepdims=True) + RMS_EPS)
    return (y * g.astype(jnp.float32)).astype(x.dtype)


def _alibi_slopes(n_heads):
    return (2.0 ** (-8.0 * np.arange(1, n_heads + 1) / n_heads)).astype(np.float32)


def _windowed_gqa_alibi_sink(h, w_qkv, w_o, sink):
    B, S, _ = h.shape
    nb = S // ATTN_BLOCK
    H, KV, G, dh = ATTN_HEADS, ATTN_KV_HEADS, ATTN_GROUP, ATTN_HEAD_DIM
    qkv = h @ w_qkv
    q, k, v = jnp.split(qkv, [H * dh, (H + KV) * dh], axis=-1)
    q = q.reshape(B, nb, ATTN_BLOCK, KV, G, dh)
    pad = N_HALO * ATTN_BLOCK
    padw = ((0, 0), (pad, pad), (0, 0), (0, 0))
    kp = jnp.pad(k.reshape(B, S, KV, dh), padw).reshape(B, nb + 2 * N_HALO, ATTN_BLOCK, KV, dh)
    vp = jnp.pad(v.reshape(B, S, KV, dh), padw).reshape(B, nb + 2 * N_HALO, ATTN_BLOCK, KV, dh)
    kb = jnp.concatenate([kp[:, o:o + nb] for o in range(2 * N_HALO + 1)], axis=2)
    vb = jnp.concatenate([vp[:, o:o + nb] for o in range(2 * N_HALO + 1)], axis=2)
    s = jnp.einsum('bnqkgd,bnskd->bnkgqs', q, kb,
                   preferred_element_type=jnp.float32) * (dh ** -0.5)
    rel = jnp.arange(ATTN_KEYS)[None, :] - pad - jnp.arange(ATTN_BLOCK)[:, None]
    key_idx = jnp.arange(nb)[:, None] * ATTN_BLOCK - pad + jnp.arange(ATTN_KEYS)[None, :]
    valid = (jnp.abs(rel) <= WINDOW)[None] & ((key_idx >= 0) & (key_idx < S))[:, None, :]
    slopes = jnp.asarray(_alibi_slopes(H)).reshape(KV, G)
    bias = -slopes[:, :, None, None] * jnp.abs(rel).astype(jnp.float32)
    s = jnp.where(valid[None, :, None, None], s + bias, -jnp.inf)
    sink_l = sink.astype(jnp.float32).reshape(KV, G)[:, :, None, None]
    m = jnp.maximum(s.max(axis=-1, keepdims=True), sink_l)
    p = jnp.exp(s - m)
    p = p / (p.sum(axis=-1, keepdims=True) + jnp.exp(sink_l - m))
    o = jnp.einsum('bnkgqs,bnskd->bnqkgd', p.astype(vb.dtype), vb)
    return o.reshape(B, S, H * dh) @ w_o


def _complex_recurrence_combine(e1, e2):
    a1r, a1i, b1r, b1i = e1
    a2r, a2i, b2r, b2i = e2
    return (a2r * a1r - a2i * a1i,
            a2r * a1i + a2i * a1r,
            a2r * b1r - a2i * b1i + b2r,
            a2r * b1i + a2i * b1r + b2i)


def _s5_direction(ug, a_re, a_im, log_step, b_re, b_im, c_re, c_im, reverse):
    S = ug.shape[1]
    a_re = a_re.astype(jnp.float32)
    a_im = a_im.astype(jnp.float32)
    step = jnp.exp(log_step.astype(jnp.float32))[:, None]
    mag = jnp.exp(step * a_re)
    lb_re = mag * jnp.cos(step * a_im)
    lb_im = mag * jnp.sin(step * a_im)
    n_re, n_im = lb_re - 1.0, lb_im
    den = a_re * a_re + a_im * a_im
    coef_re = (n_re * a_re + n_im * a_im) / den
    coef_im = (n_im * a_re - n_re * a_im) / den
    b_re = b_re.astype(jnp.float32)
    b_im = b_im.astype(jnp.float32)
    bb_re = coef_re[..., None] * b_re - coef_im[..., None] * b_im
    bb_im = coef_re[..., None] * b_im + coef_im[..., None] * b_re
    bu_re = jnp.einsum('bsgc,gnc->bsgn', ug, bb_re)
    bu_im = jnp.einsum('bsgc,gnc->bsgn', ug, bb_im)
    lam_re = jnp.broadcast_to(lb_re, (1, S) + lb_re.shape)
    lam_im = jnp.broadcast_to(lb_im, (1, S) + lb_im.shape)
    _, _, x_re, x_im = lax.associative_scan(
        _complex_recurrence_combine, (lam_re, lam_im, bu_re, bu_im), reverse=reverse, axis=1)
    return (jnp.einsum('bsgn,gcn->bsgc', x_re, c_re.astype(jnp.float32))
            - jnp.einsum('bsgn,gcn->bsgc', x_im, c_im.astype(jnp.float32)))


def _s5_bidirectional_glu(h, a_re, a_im, log_step, b_re, b_im, c_re, c_im, d_skip, w_glu, b_glu):
    B, S, D = h.shape
    u = h.astype(jnp.float32)
    ug = u.reshape(B, S, SSM_GROUPS, SSM_GROUP_CH)
    y = d_skip.astype(jnp.float32) * u
    for dirn in range(SSM_DIRS):
        y = y + _s5_direction(ug, a_re[dirn], a_im[dirn], log_step[dirn], b_re[dirn], b_im[dirn],
                              c_re[dirn], c_im[dirn], reverse=(dirn == 1)).reshape(B, S, D)
    z = jax.nn.gelu(y)
    out = z * jax.nn.sigmoid(z @ w_glu.astype(jnp.float32) + b_glu.astype(jnp.float32))
    return out.astype(h.dtype)


def _rope(x, cos, sin):
    half = x.shape[-1] // 2
    x1, x2 = x[..., :half], x[..., half:]
    return jnp.concatenate([x1 * cos - x2 * sin, x2 * cos + x1 * sin], axis=-1).astype(x.dtype)


def _mla(h, w_dqkv, q_norm, kv_norm, w_uq, w_ukv, w_o):
    B, S, _ = h.shape
    H = MLA_HEADS
    nb = S // MLA_BLOCK
    d = h @ w_dqkv
    c_q, c_kv, k_rope = jnp.split(d, [MLA_Q_LORA, MLA_Q_LORA + MLA_KV_LORA], axis=-1)
    c_q = _rmsnorm(c_q, q_norm)
    c_kv = _rmsnorm(c_kv, kv_norm)
    q = (c_q @ w_uq).reshape(B, S, H, MLA_NOPE + MLA_ROPE)
    q_nope, q_rope = jnp.split(q, [MLA_NOPE], axis=-1)
    kv = (c_kv @ w_ukv).reshape(B, S, H, MLA_NOPE + MLA_V)
    k_nope, v = jnp.split(kv, [MLA_NOPE], axis=-1)
    half = MLA_ROPE // 2
    pos = jnp.arange(S, dtype=jnp.float32)
    inv = ROPE_THETA ** (-jnp.arange(half, dtype=jnp.float32) / half)
    ang = pos[:, None] * inv[None, :]
    cos, sin = jnp.cos(ang), jnp.sin(ang)
    q_rope = _rope(q_rope, cos[:, None, :], sin[:, None, :])
    k_rope = _rope(k_rope, cos, sin)
    scale = (MLA_NOPE + MLA_ROPE) ** -0.5

    def q_block(args):
        qn, qr = args
        s = (jnp.einsum('bqhd,bshd->bhqs', qn, k_nope, preferred_element_type=jnp.float32)
             + jnp.einsum('bqhd,bsd->bhqs', qr, k_rope, preferred_element_type=jnp.float32)) * scale
        p = jax.nn.softmax(s, axis=-1)
        return jnp.einsum('bhqs,bshd->bqhd', p.astype(v.dtype), v)

    qn_b = q_nope.reshape(B, nb, MLA_BLOCK, H, MLA_NOPE).swapaxes(0, 1)
    qr_b = q_rope.reshape(B, nb, MLA_BLOCK, H, MLA_ROPE).swapaxes(0, 1)
    o = lax.map(q_block, (qn_b, qr_b))
    o = o.swapaxes(0, 1).reshape(B, S, H * MLA_V)
    return o @ w_o


def _conv_ffn(h, w_up, conv_w, conv_b, w_down):
    S = h.shape[1]
    u = h @ w_up
    r = CONV_WIDTH // 2
    up = jnp.pad(u, ((0, 0), (r, r), (0, 0)))
    c = conv_b + sum(conv_w[t] * up[:, t:t + S] for t in range(CONV_WIDTH))
    gate, val = jnp.split(c, 2, axis=-1)
    return (jax.nn.silu(gate) * val) @ w_down


def _fwd_setup_inputs(seed: int = 0) -> dict:
    key = jax.random.key(seed)
    ks = iter(jax.random.split(key, 32))
    f32 = jnp.float32

    def nrm(shape, scale):
        return jax.random.normal(next(ks), shape, f32) * scale

    D = D_MODEL
    attn_qkv_w = (ATTN_HEADS + 2 * ATTN_KV_HEADS) * ATTN_HEAD_DIM
    ssm_shape = (N_LAYERS_B, SSM_DIRS, SSM_GROUPS, SSM_STATE)
    n_idx = jnp.arange(SSM_STATE, dtype=f32)
    return {
        "x": nrm((BATCH, SEQ, D), 1.0),
        "mix_norm": 1.0 + nrm((DEPTH, D), 0.02),
        "ffn_norm": 1.0 + nrm((DEPTH, D), 0.02),
        "final_norm": 1.0 + nrm((D,), 0.02),
        "attn_w_qkv": nrm((N_LAYERS_A, D, attn_qkv_w), D ** -0.5),
        "attn_w_o": nrm((N_LAYERS_A, ATTN_HEADS * ATTN_HEAD_DIM, D), (ATTN_HEADS * ATTN_HEAD_DIM) ** -0.5),
        "attn_sink": nrm((N_LAYERS_A, ATTN_HEADS), 0.5),
        "ssm_a_re": -0.5 + nrm(ssm_shape, 0.01),
        "ssm_a_im": jnp.pi * n_idx + nrm(ssm_shape, 0.01),
        "ssm_log_step": jax.random.uniform(next(ks), (N_LAYERS_B, SSM_DIRS, SSM_GROUPS), f32,
                                           math.log(DT_MIN), math.log(DT_MAX)),
        "ssm_b_re": nrm(ssm_shape + (SSM_GROUP_CH,), (2 * SSM_GROUP_CH) ** -0.5),
        "ssm_b_im": nrm(ssm_shape + (SSM_GROUP_CH,), (2 * SSM_GROUP_CH) ** -0.5),
        "ssm_c_re": nrm((N_LAYERS_B, SSM_DIRS, SSM_GROUPS, SSM_GROUP_CH, SSM_STATE), SSM_STATE ** -0.5),
        "ssm_c_im": nrm((N_LAYERS_B, SSM_DIRS, SSM_GROUPS, SSM_GROUP_CH, SSM_STATE), SSM_STATE ** -0.5),
        "ssm_d": nrm((N_LAYERS_B, D), 1.0),
        "ssm_w_glu": nrm((N_LAYERS_B, D, D), D ** -0.5),
        "ssm_b_glu": nrm((N_LAYERS_B, D), 0.01),
        "mla_w_dqkv": nrm((N_LAYERS_C, D, MLA_Q_LORA + MLA_KV_LORA + MLA_ROPE), D ** -0.5),
        "mla_q_norm": 1.0 + nrm((N_LAYERS_C, MLA_Q_LORA), 0.02),
        "mla_kv_norm": 1.0 + nrm((N_LAYERS_C, MLA_KV_LORA), 0.02),
        "mla_w_uq": nrm((N_LAYERS_C, MLA_Q_LORA, MLA_HEADS * (MLA_NOPE + MLA_ROPE)), MLA_Q_LORA ** -0.5),
        "mla_w_ukv": nrm((N_LAYERS_C, MLA_KV_LORA, MLA_HEADS * (MLA_NOPE + MLA_V)), MLA_KV_LORA ** -0.5),
        "mla_w_o": nrm((N_LAYERS_C, MLA_HEADS * MLA_V, D), (MLA_HEADS * MLA_V) ** -0.5),
        "ffn_w_up": nrm((DEPTH, D, 2 * D_FF), D ** -0.5),
        "ffn_conv_w": nrm((DEPTH, CONV_WIDTH, 2 * D_FF), CONV_WIDTH ** -0.5),
        "ffn_conv_b": nrm((DEPTH, 2 * D_FF), 0.01),
        "ffn_w_down": nrm((DEPTH, D_FF, D), D_FF ** -0.5),
    }


def _fwd_reference(x, mix_norm, ffn_norm, final_norm, attn_w_qkv, attn_w_o, attn_sink,
              ssm_a_re, ssm_a_im, ssm_log_step, ssm_b_re, ssm_b_im, ssm_c_re, ssm_c_im,
              ssm_d, ssm_w_glu, ssm_b_glu, mla_w_dqkv, mla_q_norm, mla_kv_norm, mla_w_uq,
              mla_w_ukv, mla_w_o, ffn_w_up, ffn_conv_w, ffn_conv_b, ffn_w_down):
    for i in range(DEPTH):
        kind = i % N_MIXERS
        j = i // N_MIXERS
        h = _rmsnorm(x, mix_norm[i])
        if kind == 0:
            h = _windowed_gqa_alibi_sink(h, attn_w_qkv[j], attn_w_o[j], attn_sink[j])
        elif kind == 1:
            h = _s5_bidirectional_glu(h, ssm_a_re[j], ssm_a_im[j], ssm_log_step[j], ssm_b_re[j],
                                      ssm_b_im[j], ssm_c_re[j], ssm_c_im[j], ssm_d[j],
                                      ssm_w_glu[j], ssm_b_glu[j])
        else:
            h = _mla(h, mla_w_dqkv[j], mla_q_norm[j], mla_kv_norm[j], mla_w_uq[j],
                     mla_w_ukv[j], mla_w_o[j])
        x = x + h
        x = x + _conv_ffn(_rmsnorm(x, ffn_norm[i]), ffn_w_up[i], ffn_conv_w[i],
                          ffn_conv_b[i], ffn_w_down[i])
    return _rmsnorm(x, final_norm)


import jax as _jax
import jax.numpy as _jnp

TWIN_FORMAT = 'train_step'
FWD_PARAMS = ['x', 'mix_norm', 'ffn_norm', 'final_norm', 'attn_w_qkv', 'attn_w_o', 'attn_sink', 'ssm_a_re', 'ssm_a_im', 'ssm_log_step', 'ssm_b_re', 'ssm_b_im', 'ssm_c_re', 'ssm_c_im', 'ssm_d', 'ssm_w_glu', 'ssm_b_glu', 'mla_w_dqkv', 'mla_q_norm', 'mla_kv_norm', 'mla_w_uq', 'mla_w_ukv', 'mla_w_o', 'ffn_w_up', 'ffn_conv_w', 'ffn_conv_b', 'ffn_w_down']
TWIN_WEIGHTS = ['mix_norm', 'ffn_norm', 'final_norm', 'attn_w_qkv', 'attn_w_o', 'attn_sink', 'ssm_a_re', 'ssm_a_im', 'ssm_log_step', 'ssm_b_re', 'ssm_b_im', 'ssm_c_re', 'ssm_c_im', 'ssm_d', 'ssm_w_glu', 'ssm_b_glu', 'mla_w_dqkv', 'mla_q_norm', 'mla_kv_norm', 'mla_w_uq', 'mla_w_ukv', 'mla_w_o', 'ffn_w_up', 'ffn_conv_w', 'ffn_conv_b', 'ffn_w_down']
TWIN_DIFF_INPUT = 'x'
TWIN_INPUTS = ['x', 'mix_norm', 'ffn_norm', 'final_norm', 'attn_w_qkv', 'attn_w_o', 'attn_sink', 'ssm_a_re', 'ssm_a_im', 'ssm_log_step', 'ssm_b_re', 'ssm_b_im', 'ssm_c_re', 'ssm_c_im', 'ssm_d', 'ssm_w_glu', 'ssm_b_glu', 'mla_w_dqkv', 'mla_q_norm', 'mla_kv_norm', 'mla_w_uq', 'mla_w_ukv', 'mla_w_o', 'ffn_w_up', 'ffn_conv_w', 'ffn_conv_b', 'ffn_w_down', 'loss_target', 'm_mix_norm', 'm_ffn_norm', 'm_final_norm', 'm_attn_w_qkv', 'm_attn_w_o', 'm_attn_sink', 'm_ssm_a_re', 'm_ssm_a_im', 'm_ssm_log_step', 'm_ssm_b_re', 'm_ssm_b_im', 'm_ssm_c_re', 'm_ssm_c_im', 'm_ssm_d', 'm_ssm_w_glu', 'm_ssm_b_glu', 'm_mla_w_dqkv', 'm_mla_q_norm', 'm_mla_kv_norm', 'm_mla_w_uq', 'm_mla_w_ukv', 'm_mla_w_o', 'm_ffn_w_up', 'm_ffn_conv_w', 'm_ffn_conv_b', 'm_ffn_w_down', 'v_mix_norm', 'v_ffn_norm', 'v_final_norm', 'v_attn_w_qkv', 'v_attn_w_o', 'v_attn_sink', 'v_ssm_a_re', 'v_ssm_a_im', 'v_ssm_log_step', 'v_ssm_b_re', 'v_ssm_b_im', 'v_ssm_c_re', 'v_ssm_c_im', 'v_ssm_d', 'v_ssm_w_glu', 'v_ssm_b_glu', 'v_mla_w_dqkv', 'v_mla_q_norm', 'v_mla_kv_norm', 'v_mla_w_uq', 'v_mla_w_ukv', 'v_mla_w_o', 'v_ffn_w_up', 'v_ffn_conv_w', 'v_ffn_conv_b', 'v_ffn_w_down']
TWIN_OUTPUTS = ['loss', 'grad_x', 'grad_mix_norm', 'grad_ffn_norm', 'grad_final_norm', 'grad_attn_w_qkv', 'grad_attn_w_o', 'grad_attn_sink', 'grad_ssm_a_re', 'grad_ssm_a_im', 'grad_ssm_log_step', 'grad_ssm_b_re', 'grad_ssm_b_im', 'grad_ssm_c_re', 'grad_ssm_c_im', 'grad_ssm_d', 'grad_ssm_w_glu', 'grad_ssm_b_glu', 'grad_mla_w_dqkv', 'grad_mla_q_norm', 'grad_mla_kv_norm', 'grad_mla_w_uq', 'grad_mla_w_ukv', 'grad_mla_w_o', 'grad_ffn_w_up', 'grad_ffn_conv_w', 'grad_ffn_conv_b', 'grad_ffn_w_down', 'delta_mix_norm', 'delta_ffn_norm', 'delta_final_norm', 'delta_attn_w_qkv', 'delta_attn_w_o', 'delta_attn_sink', 'delta_ssm_a_re', 'delta_ssm_a_im', 'delta_ssm_log_step', 'delta_ssm_b_re', 'delta_ssm_b_im', 'delta_ssm_c_re', 'delta_ssm_c_im', 'delta_ssm_d', 'delta_ssm_w_glu', 'delta_ssm_b_glu', 'delta_mla_w_dqkv', 'delta_mla_q_norm', 'delta_mla_kv_norm', 'delta_mla_w_uq', 'delta_mla_w_ukv', 'delta_mla_w_o', 'delta_ffn_w_up', 'delta_ffn_conv_w', 'delta_ffn_conv_b', 'delta_ffn_w_down', 'new_m_mix_norm', 'new_m_ffn_norm', 'new_m_final_norm', 'new_m_attn_w_qkv', 'new_m_attn_w_o', 'new_m_attn_sink', 'new_m_ssm_a_re', 'new_m_ssm_a_im', 'new_m_ssm_log_step', 'new_m_ssm_b_re', 'new_m_ssm_b_im', 'new_m_ssm_c_re', 'new_m_ssm_c_im', 'new_m_ssm_d', 'new_m_ssm_w_glu', 'new_m_ssm_b_glu', 'new_m_mla_w_dqkv', 'new_m_mla_q_norm', 'new_m_mla_kv_norm', 'new_m_mla_w_uq', 'new_m_mla_w_ukv', 'new_m_mla_w_o', 'new_m_ffn_w_up', 'new_m_ffn_conv_w', 'new_m_ffn_conv_b', 'new_m_ffn_w_down', 'new_v_mix_norm', 'new_v_ffn_norm', 'new_v_final_norm', 'new_v_attn_w_qkv', 'new_v_attn_w_o', 'new_v_attn_sink', 'new_v_ssm_a_re', 'new_v_ssm_a_im', 'new_v_ssm_log_step', 'new_v_ssm_b_re', 'new_v_ssm_b_im', 'new_v_ssm_c_re', 'new_v_ssm_c_im', 'new_v_ssm_d', 'new_v_ssm_w_glu', 'new_v_ssm_b_glu', 'new_v_mla_w_dqkv', 'new_v_mla_q_norm', 'new_v_mla_kv_norm', 'new_v_mla_w_uq', 'new_v_mla_w_ukv', 'new_v_mla_w_o', 'new_v_ffn_w_up', 'new_v_ffn_conv_w', 'new_v_ffn_conv_b', 'new_v_ffn_w_down']
TWIN_LEAF_KINDS = {'loss': 'loss', 'grad_x': 'grad_x', 'grad_mix_norm': 'grad_w', 'grad_ffn_norm': 'grad_w', 'grad_final_norm': 'grad_w', 'grad_attn_w_qkv': 'grad_w', 'grad_attn_w_o': 'grad_w', 'grad_attn_sink': 'grad_w', 'grad_ssm_a_re': 'grad_w', 'grad_ssm_a_im': 'grad_w', 'grad_ssm_log_step': 'grad_w', 'grad_ssm_b_re': 'grad_w', 'grad_ssm_b_im': 'grad_w', 'grad_ssm_c_re': 'grad_w', 'grad_ssm_c_im': 'grad_w', 'grad_ssm_d': 'grad_w', 'grad_ssm_w_glu': 'grad_w', 'grad_ssm_b_glu': 'grad_w', 'grad_mla_w_dqkv': 'grad_w', 'grad_mla_q_norm': 'grad_w', 'grad_mla_kv_norm': 'grad_w', 'grad_mla_w_uq': 'grad_w', 'grad_mla_w_ukv': 'grad_w', 'grad_mla_w_o': 'grad_w', 'grad_ffn_w_up': 'grad_w', 'grad_ffn_conv_w': 'grad_w', 'grad_ffn_conv_b': 'grad_w', 'grad_ffn_w_down': 'grad_w', 'delta_mix_norm': 'delta_w', 'delta_ffn_norm': 'delta_w', 'delta_final_norm': 'delta_w', 'delta_attn_w_qkv': 'delta_w', 'delta_attn_w_o': 'delta_w', 'delta_attn_sink': 'delta_w', 'delta_ssm_a_re': 'delta_w', 'delta_ssm_a_im': 'delta_w', 'delta_ssm_log_step': 'delta_w', 'delta_ssm_b_re': 'delta_w', 'delta_ssm_b_im': 'delta_w', 'delta_ssm_c_re': 'delta_w', 'delta_ssm_c_im': 'delta_w', 'delta_ssm_d': 'delta_w', 'delta_ssm_w_glu': 'delta_w', 'delta_ssm_b_glu': 'delta_w', 'delta_mla_w_dqkv': 'delta_w', 'delta_mla_q_norm': 'delta_w', 'delta_mla_kv_norm': 'delta_w', 'delta_mla_w_uq': 'delta_w', 'delta_mla_w_ukv': 'delta_w', 'delta_mla_w_o': 'delta_w', 'delta_ffn_w_up': 'delta_w', 'delta_ffn_conv_w': 'delta_w', 'delta_ffn_conv_b': 'delta_w', 'delta_ffn_w_down': 'delta_w', 'new_m_mix_norm': 'new_m', 'new_m_ffn_norm': 'new_m', 'new_m_final_norm': 'new_m', 'new_m_attn_w_qkv': 'new_m', 'new_m_attn_w_o': 'new_m', 'new_m_attn_sink': 'new_m', 'new_m_ssm_a_re': 'new_m', 'new_m_ssm_a_im': 'new_m', 'new_m_ssm_log_step': 'new_m', 'new_m_ssm_b_re': 'new_m', 'new_m_ssm_b_im': 'new_m', 'new_m_ssm_c_re': 'new_m', 'new_m_ssm_c_im': 'new_m', 'new_m_ssm_d': 'new_m', 'new_m_ssm_w_glu': 'new_m', 'new_m_ssm_b_glu': 'new_m', 'new_m_mla_w_dqkv': 'new_m', 'new_m_mla_q_norm': 'new_m', 'new_m_mla_kv_norm': 'new_m', 'new_m_mla_w_uq': 'new_m', 'new_m_mla_w_ukv': 'new_m', 'new_m_mla_w_o': 'new_m', 'new_m_ffn_w_up': 'new_m', 'new_m_ffn_conv_w': 'new_m', 'new_m_ffn_conv_b': 'new_m', 'new_m_ffn_w_down': 'new_m', 'new_v_mix_norm': 'new_v', 'new_v_ffn_norm': 'new_v', 'new_v_final_norm': 'new_v', 'new_v_attn_w_qkv': 'new_v', 'new_v_attn_w_o': 'new_v', 'new_v_attn_sink': 'new_v', 'new_v_ssm_a_re': 'new_v', 'new_v_ssm_a_im': 'new_v', 'new_v_ssm_log_step': 'new_v', 'new_v_ssm_b_re': 'new_v', 'new_v_ssm_b_im': 'new_v', 'new_v_ssm_c_re': 'new_v', 'new_v_ssm_c_im': 'new_v', 'new_v_ssm_d': 'new_v', 'new_v_ssm_w_glu': 'new_v', 'new_v_ssm_b_glu': 'new_v', 'new_v_mla_w_dqkv': 'new_v', 'new_v_mla_q_norm': 'new_v', 'new_v_mla_kv_norm': 'new_v', 'new_v_mla_w_uq': 'new_v', 'new_v_mla_w_ukv': 'new_v', 'new_v_mla_w_o': 'new_v', 'new_v_ffn_w_up': 'new_v', 'new_v_ffn_conv_w': 'new_v', 'new_v_ffn_conv_b': 'new_v', 'new_v_ffn_w_down': 'new_v'}


def _forward(args):
    return _fwd_reference(*[args[k] for k in FWD_PARAMS])


def _output_shape():
    def fwd():
        inp = _fwd_setup_inputs(0)
        return _fwd_reference(*[inp[k] for k in FWD_PARAMS])
    out = _jax.eval_shape(fwd)
    return out.shape, out.dtype

N_MICROBATCH = 1
ADAM_LR = 0.001
ADAM_B1 = 0.9
ADAM_B2 = 0.999
ADAM_EPS = 1e-08
ADAM_WD = 0.01
ADAM_STEP = 10
PER_EXAMPLE_BATCH_AXIS = {'x': 0, 'loss_target': 0}
SHARED_INPUTS = []
_WEIGHT_DTYPES = {'mix_norm': _jnp.float32, 'ffn_norm': _jnp.float32, 'final_norm': _jnp.float32, 'attn_w_qkv': _jnp.float32, 'attn_w_o': _jnp.float32, 'attn_sink': _jnp.float32, 'ssm_a_re': _jnp.float32, 'ssm_a_im': _jnp.float32, 'ssm_log_step': _jnp.float32, 'ssm_b_re': _jnp.float32, 'ssm_b_im': _jnp.float32, 'ssm_c_re': _jnp.float32, 'ssm_c_im': _jnp.float32, 'ssm_d': _jnp.float32, 'ssm_w_glu': _jnp.float32, 'ssm_b_glu': _jnp.float32, 'mla_w_dqkv': _jnp.float32, 'mla_q_norm': _jnp.float32, 'mla_kv_norm': _jnp.float32, 'mla_w_uq': _jnp.float32, 'mla_w_ukv': _jnp.float32, 'mla_w_o': _jnp.float32, 'ffn_w_up': _jnp.float32, 'ffn_conv_w': _jnp.float32, 'ffn_conv_b': _jnp.float32, 'ffn_w_down': _jnp.float32}
MOMENT_SCALE = {'mix_norm': 6.963724e-02, 'ffn_norm': 9.271798e-02, 'final_norm': 3.284498e+01, 'attn_w_qkv': 4.839698e-02, 'attn_w_o': 3.648580e-02, 'attn_sink': 4.688791e-02, 'ssm_a_re': 3.760297e-03, 'ssm_a_im': 3.703196e-03, 'ssm_log_step': 3.526045e+00, 'ssm_b_re': 2.319837e-03, 'ssm_b_im': 2.267142e-03, 'ssm_c_re': 3.242739e-03, 'ssm_c_im': 3.209745e-03, 'ssm_d': 1.054782e-01, 'ssm_w_glu': 1.937526e-02, 'ssm_b_glu': 5.216448e-02, 'mla_w_dqkv': 3.311503e-02, 'mla_q_norm': 2.090473e-02, 'mla_kv_norm': 4.317147e-02, 'mla_w_uq': 8.990759e-03, 'mla_w_ukv': 1.491995e-02, 'mla_w_o': 1.944056e-02, 'ffn_w_up': 3.892329e-02, 'ffn_conv_w': 3.864175e-02, 'ffn_conv_b': 3.891794e-02, 'ffn_w_down': 6.358574e-02}


def _to_microbatches(a, axis):
    t = _jnp.moveaxis(a, axis, 0)
    t = t.reshape((N_MICROBATCH, t.shape[0] // N_MICROBATCH) + t.shape[1:])
    return _jnp.moveaxis(t, 1, axis + 1)


def setup_inputs(seed: int = 0) -> dict:
    inp = _fwd_setup_inputs(seed)
    key = _jax.random.fold_in(_jax.random.key(seed), 7919)
    shape, _ = _output_shape()
    out = dict(inp)
    out["loss_target"] = _jax.random.normal(_jax.random.fold_in(key, 0), shape, _jnp.float32)
    for i, name in enumerate(TWIN_WEIGHTS):
        w = inp[name].astype(_jnp.float32)
        if MOMENT_SCALE is None:
            s = _jnp.sqrt(_jnp.mean(_jnp.square(w)) + 1e-30)
        else:
            s = MOMENT_SCALE[name]
        km, kv = _jax.random.split(_jax.random.fold_in(key, i + 1))
        out[name] = w
        out["m_" + name] = s * _jax.random.normal(km, w.shape, _jnp.float32)
        out["v_" + name] = (s * s) * _jax.random.uniform(kv, w.shape, _jnp.float32, 0.5, 1.5)
    if N_MICROBATCH > 1:
        for name, axis in PER_EXAMPLE_BATCH_AXIS.items():
            out[name] = _to_microbatches(out[name], axis)
    return {'x': out['x'], 'mix_norm': out['mix_norm'], 'ffn_norm': out['ffn_norm'], 'final_norm': out['final_norm'], 'attn_w_qkv': out['attn_w_qkv'], 'attn_w_o': out['attn_w_o'], 'attn_sink': out['attn_sink'], 'ssm_a_re': out['ssm_a_re'], 'ssm_a_im': out['ssm_a_im'], 'ssm_log_step': out['ssm_log_step'], 'ssm_b_re': out['ssm_b_re'], 'ssm_b_im': out['ssm_b_im'], 'ssm_c_re': out['ssm_c_re'], 'ssm_c_im': out['ssm_c_im'], 'ssm_d': out['ssm_d'], 'ssm_w_glu': out['ssm_w_glu'], 'ssm_b_glu': out['ssm_b_glu'], 'mla_w_dqkv': out['mla_w_dqkv'], 'mla_q_norm': out['mla_q_norm'], 'mla_kv_norm': out['mla_kv_norm'], 'mla_w_uq': out['mla_w_uq'], 'mla_w_ukv': out['mla_w_ukv'], 'mla_w_o': out['mla_w_o'], 'ffn_w_up': out['ffn_w_up'], 'ffn_conv_w': out['ffn_conv_w'], 'ffn_conv_b': out['ffn_conv_b'], 'ffn_w_down': out['ffn_w_down'], 'loss_target': out['loss_target'], 'm_mix_norm': out['m_mix_norm'], 'm_ffn_norm': out['m_ffn_norm'], 'm_final_norm': out['m_final_norm'], 'm_attn_w_qkv': out['m_attn_w_qkv'], 'm_attn_w_o': out['m_attn_w_o'], 'm_attn_sink': out['m_attn_sink'], 'm_ssm_a_re': out['m_ssm_a_re'], 'm_ssm_a_im': out['m_ssm_a_im'], 'm_ssm_log_step': out['m_ssm_log_step'], 'm_ssm_b_re': out['m_ssm_b_re'], 'm_ssm_b_im': out['m_ssm_b_im'], 'm_ssm_c_re': out['m_ssm_c_re'], 'm_ssm_c_im': out['m_ssm_c_im'], 'm_ssm_d': out['m_ssm_d'], 'm_ssm_w_glu': out['m_ssm_w_glu'], 'm_ssm_b_glu': out['m_ssm_b_glu'], 'm_mla_w_dqkv': out['m_mla_w_dqkv'], 'm_mla_q_norm': out['m_mla_q_norm'], 'm_mla_kv_norm': out['m_mla_kv_norm'], 'm_mla_w_uq': out['m_mla_w_uq'], 'm_mla_w_ukv': out['m_mla_w_ukv'], 'm_mla_w_o': out['m_mla_w_o'], 'm_ffn_w_up': out['m_ffn_w_up'], 'm_ffn_conv_w': out['m_ffn_conv_w'], 'm_ffn_conv_b': out['m_ffn_conv_b'], 'm_ffn_w_down': out['m_ffn_w_down'], 'v_mix_norm': out['v_mix_norm'], 'v_ffn_norm': out['v_ffn_norm'], 'v_final_norm': out['v_final_norm'], 'v_attn_w_qkv': out['v_attn_w_qkv'], 'v_attn_w_o': out['v_attn_w_o'], 'v_attn_sink': out['v_attn_sink'], 'v_ssm_a_re': out['v_ssm_a_re'], 'v_ssm_a_im': out['v_ssm_a_im'], 'v_ssm_log_step': out['v_ssm_log_step'], 'v_ssm_b_re': out['v_ssm_b_re'], 'v_ssm_b_im': out['v_ssm_b_im'], 'v_ssm_c_re': out['v_ssm_c_re'], 'v_ssm_c_im': out['v_ssm_c_im'], 'v_ssm_d': out['v_ssm_d'], 'v_ssm_w_glu': out['v_ssm_w_glu'], 'v_ssm_b_glu': out['v_ssm_b_glu'], 'v_mla_w_dqkv': out['v_mla_w_dqkv'], 'v_mla_q_norm': out['v_mla_q_norm'], 'v_mla_kv_norm': out['v_mla_kv_norm'], 'v_mla_w_uq': out['v_mla_w_uq'], 'v_mla_w_ukv': out['v_mla_w_ukv'], 'v_mla_w_o': out['v_mla_w_o'], 'v_ffn_w_up': out['v_ffn_w_up'], 'v_ffn_conv_w': out['v_ffn_conv_w'], 'v_ffn_conv_b': out['v_ffn_conv_b'], 'v_ffn_w_down': out['v_ffn_w_down']}


def _loss(weights, diff, rest, loss_target):
    with _jax.named_scope("forward"):
        args = {**rest, TWIN_DIFF_INPUT: diff, **{k: w.astype(_WEIGHT_DTYPES[k]) for k, w in weights.items()}}
        y = _forward(args)
    with _jax.named_scope("loss_head"):
        err = _jnp.square(y.astype(_jnp.float32) - loss_target)
        return 0.5 * _jnp.sum(_jnp.mean(err, axis=-1)) if err.ndim else 0.5 * err


def _adamw(w, g, m, v):
    m = ADAM_B1 * m + (1.0 - ADAM_B1) * g
    v = ADAM_B2 * v + (1.0 - ADAM_B2) * _jnp.square(g)
    m_hat = m / (1.0 - ADAM_B1 ** ADAM_STEP)
    v_hat = v / (1.0 - ADAM_B2 ** ADAM_STEP)
    delta = -ADAM_LR * (m_hat / (_jnp.sqrt(v_hat) + ADAM_EPS) + ADAM_WD * w)
    return delta, m, v


def reference(x, mix_norm, ffn_norm, final_norm, attn_w_qkv, attn_w_o, attn_sink, ssm_a_re, ssm_a_im, ssm_log_step, ssm_b_re, ssm_b_im, ssm_c_re, ssm_c_im, ssm_d, ssm_w_glu, ssm_b_glu, mla_w_dqkv, mla_q_norm, mla_kv_norm, mla_w_uq, mla_w_ukv, mla_w_o, ffn_w_up, ffn_conv_w, ffn_conv_b, ffn_w_down, loss_target, m_mix_norm, m_ffn_norm, m_final_norm, m_attn_w_qkv, m_attn_w_o, m_attn_sink, m_ssm_a_re, m_ssm_a_im, m_ssm_log_step, m_ssm_b_re, m_ssm_b_im, m_ssm_c_re, m_ssm_c_im, m_ssm_d, m_ssm_w_glu, m_ssm_b_glu, m_mla_w_dqkv, m_mla_q_norm, m_mla_kv_norm, m_mla_w_uq, m_mla_w_ukv, m_mla_w_o, m_ffn_w_up, m_ffn_conv_w, m_ffn_conv_b, m_ffn_w_down, v_mix_norm, v_ffn_norm, v_final_norm, v_attn_w_qkv, v_attn_w_o, v_attn_sink, v_ssm_a_re, v_ssm_a_im, v_ssm_log_step, v_ssm_b_re, v_ssm_b_im, v_ssm_c_re, v_ssm_c_im, v_ssm_d, v_ssm_w_glu, v_ssm_b_glu, v_mla_w_dqkv, v_mla_q_norm, v_mla_kv_norm, v_mla_w_uq, v_mla_w_ukv, v_mla_w_o, v_ffn_w_up, v_ffn_conv_w, v_ffn_conv_b, v_ffn_w_down):
    given = dict(x=x, mix_norm=mix_norm, ffn_norm=ffn_norm, final_norm=final_norm, attn_w_qkv=attn_w_qkv, attn_w_o=attn_w_o, attn_sink=attn_sink, ssm_a_re=ssm_a_re, ssm_a_im=ssm_a_im, ssm_log_step=ssm_log_step, ssm_b_re=ssm_b_re, ssm_b_im=ssm_b_im, ssm_c_re=ssm_c_re, ssm_c_im=ssm_c_im, ssm_d=ssm_d, ssm_w_glu=ssm_w_glu, ssm_b_glu=ssm_b_glu, mla_w_dqkv=mla_w_dqkv, mla_q_norm=mla_q_norm, mla_kv_norm=mla_kv_norm, mla_w_uq=mla_w_uq, mla_w_ukv=mla_w_ukv, mla_w_o=mla_w_o, ffn_w_up=ffn_w_up, ffn_conv_w=ffn_conv_w, ffn_conv_b=ffn_conv_b, ffn_w_down=ffn_w_down, loss_target=loss_target, m_mix_norm=m_mix_norm, m_ffn_norm=m_ffn_norm, m_final_norm=m_final_norm, m_attn_w_qkv=m_attn_w_qkv, m_attn_w_o=m_attn_w_o, m_attn_sink=m_attn_sink, m_ssm_a_re=m_ssm_a_re, m_ssm_a_im=m_ssm_a_im, m_ssm_log_step=m_ssm_log_step, m_ssm_b_re=m_ssm_b_re, m_ssm_b_im=m_ssm_b_im, m_ssm_c_re=m_ssm_c_re, m_ssm_c_im=m_ssm_c_im, m_ssm_d=m_ssm_d, m_ssm_w_glu=m_ssm_w_glu, m_ssm_b_glu=m_ssm_b_glu, m_mla_w_dqkv=m_mla_w_dqkv, m_mla_q_norm=m_mla_q_norm, m_mla_kv_norm=m_mla_kv_norm, m_mla_w_uq=m_mla_w_uq, m_mla_w_ukv=m_mla_w_ukv, m_mla_w_o=m_mla_w_o, m_ffn_w_up=m_ffn_w_up, m_ffn_conv_w=m_ffn_conv_w, m_ffn_conv_b=m_ffn_conv_b, m_ffn_w_down=m_ffn_w_down, v_mix_norm=v_mix_norm, v_ffn_norm=v_ffn_norm, v_final_norm=v_final_norm, v_attn_w_qkv=v_attn_w_qkv, v_attn_w_o=v_attn_w_o, v_attn_sink=v_attn_sink, v_ssm_a_re=v_ssm_a_re, v_ssm_a_im=v_ssm_a_im, v_ssm_log_step=v_ssm_log_step, v_ssm_b_re=v_ssm_b_re, v_ssm_b_im=v_ssm_b_im, v_ssm_c_re=v_ssm_c_re, v_ssm_c_im=v_ssm_c_im, v_ssm_d=v_ssm_d, v_ssm_w_glu=v_ssm_w_glu, v_ssm_b_glu=v_ssm_b_glu, v_mla_w_dqkv=v_mla_w_dqkv, v_mla_q_norm=v_mla_q_norm, v_mla_kv_norm=v_mla_kv_norm, v_mla_w_uq=v_mla_w_uq, v_mla_w_ukv=v_mla_w_ukv, v_mla_w_o=v_mla_w_o, v_ffn_w_up=v_ffn_w_up, v_ffn_conv_w=v_ffn_conv_w, v_ffn_conv_b=v_ffn_conv_b, v_ffn_w_down=v_ffn_w_down)
    weights = {n: given[n] for n in TWIN_WEIGHTS}
    shared = {n: given[n] for n in SHARED_INPUTS}
    per_example = {n: given[n] for n in ['x']}
    grad_fn = _jax.value_and_grad(_loss, argnums=(0, 1))

    def one_microbatch(ex, loss_target):
        ex = dict(ex)
        diff = ex.pop(TWIN_DIFF_INPUT)
        return grad_fn(weights, diff, {**shared, **ex}, loss_target)

    if N_MICROBATCH == 1:
        loss, (grad_w, grad_x) = one_microbatch(per_example, given["loss_target"])
    else:
        def body(carry, xs):
            loss_sum, grad_sum = carry
            l_k, (gw_k, gx_k) = one_microbatch(xs[0], xs[1])
            with _jax.named_scope("update"):
                return (loss_sum + l_k, _jax.tree.map(_jnp.add, grad_sum, gw_k)), gx_k

        init = (_jnp.zeros((), _jnp.float32), _jax.tree.map(_jnp.zeros_like, weights))
        (loss, grad_w), grad_x = _jax.lax.scan(body, init, (per_example, given["loss_target"]))
    with _jax.named_scope("update"):
        delta_w, new_m, new_v = {}, {}, {}
        for n in TWIN_WEIGHTS:
            delta_w[n], new_m[n], new_v[n] = _adamw(weights[n], grad_w[n], given["m_" + n], given["v_" + n])
    return (loss, grad_x, *[grad_w[n] for n in TWIN_WEIGHTS], *[delta_w[n] for n in TWIN_WEIGHTS],
            *[new_m[n] for n in TWIN_WEIGHTS], *[new_v[n] for n in TWIN_WEIGHTS])
```

```python
import functools
import math

import numpy as np
import jax
import jax.numpy as jnp
from jax import lax
from jax.experimental import pallas as pl
from jax.experimental.pallas import tpu as pltpu

F32 = jnp.float32
BF16 = jnp.bfloat16

RMS_EPS = 1e-6
ATTN_HEAD_DIM = 64
ATTN_GROUP = 8
ATTN_BLOCK = 128
SSM_GROUP_CH = 16
SSM_STATE = 64
MLA_NOPE = 128
MLA_ROPE = 64
MLA_V = 128
ROPE_THETA = 10000.0
ADAM_LR = 0.001
ADAM_B1 = 0.9
ADAM_B2 = 0.999
ADAM_EPS = 1e-08
ADAM_WD = 0.01
ADAM_STEP = 10

N_DEV = 8
LANES = 128
SUBLANES = 8
VMEM_LIMIT_BYTES = 50 * 2 ** 20
MASK_VALUE = -1e30
SEGS = SUBLANES


def _params(*sem):
    return pltpu.CompilerParams(dimension_semantics=sem, vmem_limit_bytes=VMEM_LIMIT_BYTES)


def _divisor(n, cap, align):
    best = None
    d = align
    while d <= min(n, cap):
        if n % d == 0:
            best = d
        d += align
    return best if best is not None else n


def mm_nn(a, w, out_dtype, name, tm=1024, tn_cap=1408, parts=1):
    M, K = a.shape
    nb, K2, n = w.shape
    assert K == K2
    tm = _divisor(M, tm, 16)
    tn = n if nb > 1 else _divisor(n // parts, tn_cap, LANES)
    jn = n // tn
    J = nb * jn
    assert J % parts == 0
    jp = J // parts

    def body(a_ref, w_ref, o_ref):
        o_ref[...] = jnp.dot(a_ref[...].astype(BF16), w_ref[...], preferred_element_type=F32).astype(o_ref.dtype)

    if parts == 1:
        out_shape = jax.ShapeDtypeStruct((M, nb * n), out_dtype)
        out_spec = pl.BlockSpec((tm, tn), lambda i, j: (i, j))
    else:
        out_shape = jax.ShapeDtypeStruct((parts, M, nb * n // parts), out_dtype)
        out_spec = pl.BlockSpec((None, tm, tn), lambda i, j: (j // jp, i, j % jp))
    return pl.pallas_call(
        body, name=name, out_shape=out_shape, grid=(M // tm, J),
        in_specs=[pl.BlockSpec((tm, K), lambda i, j: (i, 0)),
                  pl.BlockSpec((None, K, tn), lambda i, j: (j // jn, 0, j % jn))],
        out_specs=out_spec, compiler_params=_params("parallel", "arbitrary"),
    )(a, w)


def mm_nt(a, w, out_dtype, name, tm=1024, tk=1024, tc_cap=2816):
    nb, K, n = w.shape
    split = a.ndim == 3
    M = a.shape[-2]
    tm = _divisor(M, tm, 16)
    tk = _divisor(K, tk, LANES)
    P = a.shape[0] if split else 1
    tc = n if nb > 1 else _divisor(n // P, tc_cap, LANES)
    jn = n // tc
    J = nb * jn
    if split:
        P = a.shape[0]
        assert J % P == 0 and a.shape[2] * P == nb * n
        jp = J // P
        a_spec = pl.BlockSpec((None, tm, tc), lambda i, k, j: (j // jp, i, j % jp))
    else:
        assert a.shape[1] == nb * n
        a_spec = pl.BlockSpec((tm, tc), lambda i, k, j: (i, j))

    def body(a_ref, w_ref, o_ref, acc_ref):
        j = pl.program_id(2)
        part = lax.dot_general(a_ref[...].astype(BF16), w_ref[...], (((1,), (1,)), ((), ())),
                               preferred_element_type=F32)

        @pl.when(j == 0)
        def _():
            acc_ref[...] = part

        @pl.when(j > 0)
        def _():
            acc_ref[...] += part

        @pl.when(j == J - 1)
        def _():
            o_ref[...] = acc_ref[...].astype(o_ref.dtype)

    return pl.pallas_call(
        body, name=name, out_shape=jax.ShapeDtypeStruct((M, K), out_dtype), grid=(M // tm, K // tk, J),
        in_specs=[a_spec, pl.BlockSpec((None, tk, tc), lambda i, k, j: (j // jn, k, j % jn))],
        out_specs=pl.BlockSpec((tm, tk), lambda i, k, j: (i, k)),
        scratch_shapes=[pltpu.VMEM((tm, tk), F32)],
        compiler_params=_params("parallel", "parallel", "arbitrary"),
    )(a, w)


def mm_tn(a, b, nb, out_dtype, name, tm=512, tka=2048, tn_cap=1408):
    M, Ka = a.shape
    split = b.ndim == 3
    N = b.shape[-1] * (b.shape[0] if split else 1)
    n = N // nb
    tm = _divisor(M, tm, 16)
    tka = _divisor(Ka, tka, LANES)
    tn = n if nb > 1 else _divisor(n // (b.shape[0] if split else 1), tn_cap, LANES)
    jn = n // tn
    J = nb * jn
    steps = M // tm
    if split:
        P = b.shape[0]
        assert J % P == 0
        jp = J // P
        b_spec = pl.BlockSpec((None, tm, tn), lambda k, j, i: (j // jp, i, j % jp))
    else:
        b_spec = pl.BlockSpec((tm, tn), lambda k, j, i: (i, j))

    def body(a_ref, b_ref, o_ref, acc_ref):
        i = pl.program_id(2)
        part = lax.dot_general(a_ref[...].astype(BF16), b_ref[...].astype(BF16), (((0,), (0,)), ((), ())),
                               preferred_element_type=F32)

        @pl.when(i == 0)
        def _():
            acc_ref[...] = part

        @pl.when(i > 0)
        def _():
            acc_ref[...] += part

        @pl.when(i == steps - 1)
        def _():
            o_ref[...] = acc_ref[...].astype(o_ref.dtype)

    return pl.pallas_call(
        body, name=name, out_shape=jax.ShapeDtypeStruct((nb, Ka, n), out_dtype), grid=(Ka // tka, J, steps),
        in_specs=[pl.BlockSpec((tm, tka), lambda k, j, i: (i, k)), b_spec],
        out_specs=pl.BlockSpec((None, tka, tn), lambda k, j, i: (j // jn, k, j % jn)),
        scratch_shapes=[pltpu.VMEM((tka, tn), F32)],
        compiler_params=_params("parallel", "parallel", "arbitrary"),
    )(a, b)


def rms_fwd(x, g, out_dtype, name, ts=512):
    S, D = x.shape
    ts = _divisor(S, ts, 16)

    def body(x_ref, g_ref, h_ref, r_ref):
        x = x_ref[...]
        r = lax.rsqrt(jnp.mean(x * x, axis=-1, keepdims=True) + RMS_EPS)
        h_ref[...] = ((x * r) * g_ref[...]).astype(h_ref.dtype)
        r_ref[...] = r

    return pl.pallas_call(
        body, name=name,
        out_shape=(jax.ShapeDtypeStruct((S, D), out_dtype), jax.ShapeDtypeStruct((S, 1), F32)),
        grid=(S // ts,),
        in_specs=[pl.BlockSpec((ts, D), lambda i: (i, 0)), pl.BlockSpec((1, D), lambda i: (0, 0))],
        out_specs=(pl.BlockSpec((ts, D), lambda i: (i, 0)), pl.BlockSpec((ts, 1), lambda i: (i, 0))),
        compiler_params=_params("parallel"),
    )(x, g)


def rms_bwd(dh, x, rstd, g, name, dres=None, ts=512):
    S, D = x.shape
    ts = _divisor(S, ts, 16)
    has_res = dres is not None

    def body(*refs):
        if has_res:
            dh_ref, x_ref, r_ref, g_ref, res_ref, dx_ref, dg_ref = refs
        else:
            dh_ref, x_ref, r_ref, g_ref, dx_ref, dg_ref = refs
        i = pl.program_id(0)
        dh = dh_ref[...].astype(F32)
        r = r_ref[...]
        xhat = x_ref[...] * r
        dhg = dh * g_ref[...]
        c = jnp.mean(dhg * xhat, axis=-1, keepdims=True)
        dx = r * (dhg - xhat * c)
        if has_res:
            dx = dx + res_ref[...]
        dx_ref[...] = dx
        part = jnp.sum(dh * xhat, axis=0, keepdims=True)

        @pl.when(i == 0)
        def _():
            dg_ref[...] = part

        @pl.when(i > 0)
        def _():
            dg_ref[...] += part

    row = pl.BlockSpec((ts, D), lambda i: (i, 0))
    args = [dh, x, rstd, g] + ([dres] if has_res else [])
    specs = [row, row, pl.BlockSpec((ts, 1), lambda i: (i, 0)), pl.BlockSpec((1, D), lambda i: (0, 0))]
    specs += [row] if has_res else []
    return pl.pallas_call(
        body, name=name,
        out_shape=(jax.ShapeDtypeStruct((S, D), F32), jax.ShapeDtypeStruct((1, D), F32)),
        grid=(S // ts,), in_specs=specs,
        out_specs=(row, pl.BlockSpec((1, D), lambda i: (0, 0))),
        compiler_params=_params("arbitrary"),
    )(*args)


def ew(fn, name, mats, rows, out_dtypes, n_sums=0, ts=512, tc=1024):
    S, C = mats[0].shape
    ts = _divisor(S, ts, 16)
    tc = _divisor(C, tc, LANES)
    n_in = len(mats) + len(rows)
    n_out = len(out_dtypes)

    def body(*refs):
        i = pl.program_id(1)
        res = fn(*[r[...] for r in refs[:n_in]])
        if not isinstance(res, (tuple, list)):
            res = (res,)
        for o_ref, val in zip(refs[n_in:n_in + n_out], res[:n_out]):
            o_ref[...] = val.astype(o_ref.dtype)
        for s_ref, val in zip(refs[n_in + n_out:], res[n_out:]):
            part = jnp.sum(val, axis=0, keepdims=True)

            @pl.when(i == 0)
            def _():
                s_ref[...] = part

            @pl.when(i > 0)
            def _():
                s_ref[...] += part

    mat = pl.BlockSpec((ts, tc), lambda j, i: (i, j))
    row = pl.BlockSpec((1, tc), lambda j, i: (0, j))
    out_shape = tuple(jax.ShapeDtypeStruct((S, C), d) for d in out_dtypes)
    out_shape += tuple(jax.ShapeDtypeStruct((1, C), F32) for _ in range(n_sums))
    return pl.pallas_call(
        body, name=name, out_shape=out_shape, grid=(C // tc, S // ts),
        in_specs=[mat] * len(mats) + [row] * len(rows),
        out_specs=tuple([mat] * n_out + [row] * n_sums),
        compiler_params=_params("parallel", "arbitrary"),
    )(*mats, *rows)


def _halo_specs(ts, tc, S, lead):
    nblk = S // SUBLANES
    per = ts // SUBLANES
    pre = (None,) * 0
    if lead:
        main = pl.BlockSpec((lead, ts, tc), lambda j, i: (0, i, j))
        prev = pl.BlockSpec((lead, SUBLANES, tc), lambda j, i: (0, jnp.maximum(i * per - 1, 0), j))
        nxt = pl.BlockSpec((lead, SUBLANES, tc), lambda j, i: (0, jnp.minimum((i + 1) * per, nblk - 1), j))
    else:
        main = pl.BlockSpec((ts, tc), lambda j, i: (i, j))
        prev = pl.BlockSpec((SUBLANES, tc), lambda j, i: (jnp.maximum(i * per - 1, 0), j))
        nxt = pl.BlockSpec((SUBLANES, tc), lambda j, i: (jnp.minimum((i + 1) * per, nblk - 1), j))
    return main, prev, nxt


def _extended(prev, main, nxt, i, ts, S):
    ext = jnp.concatenate([prev, main, nxt], axis=0).astype(F32)
    rows = lax.broadcasted_iota(jnp.int32, (ts + 2 * SUBLANES, 1), 0) + (i * ts - SUBLANES)
    return jnp.where((rows >= 0) & (rows < S), ext, 0.0)


def _conv3(ue, w, b):
    n = ue.shape[0]
    return b + (w[0:1] * pltpu.roll(ue, 1, axis=0) + w[1:2] * ue + w[2:3] * pltpu.roll(ue, n - 1, axis=0))


def convgate_fwd(u3, conv_w, conv_b, name, ts=512, tc=512):
    _, S, F = u3.shape
    ts = _divisor(S, ts, 16)
    tc = _divisor(F, tc, LANES)
    main, prev, nxt = _halo_specs(ts, tc, S, 2)

    def body(m_ref, p_ref, n_ref, w_ref, b_ref, o_ref):
        i = pl.program_id(1)
        c = []
        for h in range(2):
            ue = _extended(p_ref[h], m_ref[h], n_ref[h], i, ts, S)
            c.append(_conv3(ue, w_ref[h], b_ref[h])[SUBLANES:SUBLANES + ts])
        o_ref[...] = (jax.nn.silu(c[0]) * c[1]).astype(o_ref.dtype)

    return pl.pallas_call(
        body, name=name, out_shape=jax.ShapeDtypeStruct((S, F), BF16), grid=(F // tc, S // ts),
        in_specs=[main, prev, nxt, pl.BlockSpec((2, 3, tc), lambda j, i: (0, 0, j)),
                  pl.BlockSpec((2, 1, tc), lambda j, i: (0, 0, j))],
        out_specs=pl.BlockSpec((ts, tc), lambda j, i: (i, j)),
        compiler_params=_params("parallel", "arbitrary"),
    )(u3, u3, u3, conv_w, conv_b)


def convgate_bwd(u3, da, conv_w, conv_b, name, ts=512, tc=512):
    _, S, F = u3.shape
    ts = _divisor(S, ts, 16)
    tc = _divisor(F, tc, LANES)
    main, prev, nxt = _halo_specs(ts, tc, S, 2)
    amain, aprev, anxt = _halo_specs(ts, tc, S, 0)
    n = ts + 2 * SUBLANES
    mid = slice(SUBLANES, SUBLANES + ts)

    def body(m_ref, p_ref, n_ref, am_ref, ap_ref, an_ref, w_ref, b_ref, du_ref, dw_ref, db_ref):
        i = pl.program_id(1)
        ue = [_extended(p_ref[h], m_ref[h], n_ref[h], i, ts, S) for h in range(2)]
        g = _conv3(ue[0], w_ref[0], b_ref[0])
        v = _conv3(ue[1], w_ref[1], b_ref[1])
        dae = _extended(ap_ref[...], am_ref[...], an_ref[...], i, ts, S)
        sg = jax.nn.sigmoid(g)
        dc = [dae * v * (sg * (1.0 + g * (1.0 - sg))), dae * (g * sg)]
        for h in range(2):
            w = w_ref[h]
            du = w[0:1] * pltpu.roll(dc[h], n - 1, axis=0) + w[1:2] * dc[h] + w[2:3] * pltpu.roll(dc[h], 1, axis=0)
            du_ref[h] = du[mid].astype(du_ref.dtype)
            dcm = dc[h][mid]
            taps = [pltpu.roll(ue[h], 1, axis=0)[mid], ue[h][mid], pltpu.roll(ue[h], n - 1, axis=0)[mid]]
            sums = [jnp.sum(dcm * t, axis=0, keepdims=True) for t in taps]
            db = jnp.sum(dcm, axis=0, keepdims=True)

            @pl.when(i == 0)
            def _():
                for t in range(3):
                    dw_ref[h, t:t + 1, :] = sums[t]
                db_ref[h] = db

            @pl.when(i > 0)
            def _():
                for t in range(3):
                    dw_ref[h, t:t + 1, :] += sums[t]
                db_ref[h] += db

    return pl.pallas_call(
        body, name=name,
        out_shape=(jax.ShapeDtypeStruct((2, S, F), BF16), jax.ShapeDtypeStruct((2, 3, F), F32),
                   jax.ShapeDtypeStruct((2, 1, F), F32)),
        grid=(F // tc, S // ts),
        in_specs=[main, prev, nxt, amain, aprev, anxt, pl.BlockSpec((2, 3, tc), lambda j, i: (0, 0, j)),
                  pl.BlockSpec((2, 1, tc), lambda j, i: (0, 0, j))],
        out_specs=(pl.BlockSpec((2, ts, tc), lambda j, i: (0, i, j)),
                   pl.BlockSpec((2, 3, tc), lambda j, i: (0, 0, j)),
                   pl.BlockSpec((2, 1, tc), lambda j, i: (0, 0, j))),
        compiler_params=_params("parallel", "arbitrary"),
    )(u3, u3, u3, da, da, da, conv_w, conv_b)


def ffn_fwd(x, norm_g, w_up, conv_w, conv_b, w_down, tag):
    h, rstd = rms_fwd(x, norm_g, BF16, f"ffn_norm_{tag}")
    u3 = mm_nn(h, w_up, F32, f"ffn_up_{tag}", parts=2)
    a = convgate_fwd(u3, conv_w, conv_b, f"ffn_gate_{tag}")
    y = mm_nn_res(a, w_down, x, f"ffn_down_{tag}")
    return y, (x, h, rstd, u3, a)


def mm_nn_res(a, w, res, name, tm=1024, tn=512):
    M, K = a.shape
    _, _, N = w.shape
    tm = _divisor(M, tm, 16)
    tn = _divisor(N, tn, LANES)

    def body(a_ref, w_ref, r_ref, o_ref):
        o_ref[...] = r_ref[...] + jnp.dot(a_ref[...].astype(BF16), w_ref[...], preferred_element_type=F32)

    return pl.pallas_call(
        body, name=name, out_shape=jax.ShapeDtypeStruct((M, N), F32), grid=(M // tm, N // tn),
        in_specs=[pl.BlockSpec((tm, K), lambda i, j: (i, 0)), pl.BlockSpec((None, K, tn), lambda i, j: (0, 0, j)),
                  pl.BlockSpec((tm, tn), lambda i, j: (i, j))],
        out_specs=pl.BlockSpec((tm, tn), lambda i, j: (i, j)),
        compiler_params=_params("parallel", "arbitrary"),
    )(a, w, res)


def ffn_bwd(dy, saved, norm_g, w_up, conv_w, conv_b, w_down, tag):
    x, h, rstd, u3, a = saved
    nb = w_up.shape[0]
    dw_down = mm_tn(a, dy, 1, BF16, f"ffn_dwdown_{tag}")
    da = mm_nt(dy, w_down, F32, f"ffn_da_{tag}", tk=512)
    du3, dconv_w, dconv_b = convgate_bwd(u3, da, conv_w, conv_b, f"ffn_dgate_{tag}")
    dh = mm_nt(du3, w_up, F32, f"ffn_dh_{tag}")
    dw_up = mm_tn(h, du3, nb, BF16, f"ffn_dwup_{tag}")
    dx, dnorm = rms_bwd(dh, x, rstd, norm_g, f"ffn_dnorm_{tag}", dres=dy)
    return dx, (dnorm, dw_up, dconv_w, dconv_b, dw_down)


def _swa_bias(n_heads):
    kv = n_heads // ATTN_GROUP
    slopes = (2.0 ** (-8.0 * np.arange(1, n_heads + 1) / n_heads)).astype(np.float32).reshape(kv, ATTN_GROUP)
    rel = np.arange(3 * ATTN_BLOCK)[None, :] - ATTN_BLOCK - np.arange(ATTN_BLOCK)[:, None]
    dist = np.abs(rel).astype(np.float32)
    bias = -slopes[:, :, None, None] * dist
    bias = np.where((np.abs(rel) <= ATTN_BLOCK)[None, None], bias, np.float32(MASK_VALUE))
    return jnp.asarray(bias, F32)


def _swa_probs(q, kc, bias, sink, invalid):
    s = lax.dot_general(q, kc, (((1,), (1,)), ((), ())), preferred_element_type=F32) * (ATTN_HEAD_DIM ** -0.5)
    s = jnp.where(invalid, MASK_VALUE, s + bias)
    m = jnp.maximum(jnp.max(s, axis=-1, keepdims=True), sink)
    p = jnp.exp(s - m)
    es = jnp.exp(sink - m)
    den = jnp.sum(p, axis=-1, keepdims=True) + es
    return p / den, es / den


def _swa_specs(S, kv_axis, n_axis):
    nb = S // ATTN_BLOCK

    def at(off):
        def index(*ids):
            n = jnp.minimum(ids[n_axis], nb - 1)
            return (ids[kv_axis], jnp.clip(n + off, 0, nb - 1), 0)
        return pl.BlockSpec((None, ATTN_BLOCK, ATTN_HEAD_DIM), index)

    qspec = pl.BlockSpec((None, ATTN_GROUP, ATTN_BLOCK, ATTN_HEAD_DIM),
                         lambda *ids: (ids[kv_axis], 0, jnp.minimum(ids[n_axis], nb - 1), 0))
    bias = pl.BlockSpec((None, ATTN_GROUP, ATTN_BLOCK, 3 * ATTN_BLOCK), lambda *ids: (ids[kv_axis], 0, 0, 0))
    sink = pl.BlockSpec((None, ATTN_GROUP, 1, LANES), lambda *ids: (ids[kv_axis], 0, 0, 0))
    return qspec, [at(-1), at(0), at(1)], bias, sink


def _swa_invalid(n, nb):
    col = lax.broadcasted_iota(jnp.int32, (1, 3 * ATTN_BLOCK), 1)
    return ((n == 0) & (col < ATTN_BLOCK)) | ((n == nb - 1) & (col >= 2 * ATTN_BLOCK))


def swa_fwd(q, k, v, bias, sink, name):
    KV, G, S, dh = q.shape
    nb = S // ATTN_BLOCK
    qspec, kspecs, bspec, sspec = _swa_specs(S, 0, 1)

    def body(q_ref, k0, k1, k2, v0, v1, v2, b_ref, s_ref, o_ref):
        n = pl.program_id(1)
        kc = jnp.concatenate([k0[...], k1[...], k2[...]], axis=0)
        vc = jnp.concatenate([v0[...], v1[...], v2[...]], axis=0)
        invalid = _swa_invalid(n, nb)
        for g in range(G):
            p, _ = _swa_probs(q_ref[g], kc, b_ref[g], s_ref[g][:, 0:1], invalid)
            o_ref[g] = jnp.dot(p.astype(BF16), vc, preferred_element_type=F32).astype(o_ref.dtype)

    return pl.pallas_call(
        body, name=name, out_shape=jax.ShapeDtypeStruct(q.shape, BF16), grid=(KV, nb),
        in_specs=[qspec] + kspecs + kspecs + [bspec, sspec], out_specs=qspec,
        compiler_params=_params("parallel", "arbitrary"),
    )(q, k, k, k, v, v, v, bias, sink)


def swa_bwd(q, k, v, bias, sink, do, name):
    KV, G, S, dh = q.shape
    nb = S // ATTN_BLOCK
    qspec, kspecs, bspec, sspec = _swa_specs(S, 0, 1)
    scale = ATTN_HEAD_DIM ** -0.5
    kv_out = pl.BlockSpec((None, ATTN_BLOCK, dh), lambda c, n: (c, jnp.maximum(n - 1, 0), 0))

    def body(q_ref, k0, k1, k2, v0, v1, v2, b_ref, s_ref, do_ref, dq_ref, dk_ref, dv_ref, ds_ref, dk_acc, dv_acc):
        n = pl.program_id(1)

        @pl.when(n == 0)
        def _():
            dk_acc[...] = jnp.zeros_like(dk_acc)
            dv_acc[...] = jnp.zeros_like(dv_acc)
            ds_ref[...] = jnp.zeros_like(ds_ref)

        @pl.when(n > 0)
        def _():
            dk_acc[(n + 1) % 3] = jnp.zeros((ATTN_BLOCK, dh), F32)
            dv_acc[(n + 1) % 3] = jnp.zeros((ATTN_BLOCK, dh), F32)

        @pl.when(n < nb)
        def _():
            kc = jnp.concatenate([k0[...], k1[...], k2[...]], axis=0)
            vc = jnp.concatenate([v0[...], v1[...], v2[...]], axis=0)
            invalid = _swa_invalid(n, nb)
            dkc = jnp.zeros((3 * ATTN_BLOCK, dh), F32)
            dvc = jnp.zeros((3 * ATTN_BLOCK, dh), F32)
            for g in range(G):
                qg = q_ref[g]
                dog = do_ref[g]
                p, psink = _swa_probs(qg, kc, b_ref[g], s_ref[g][:, 0:1], invalid)
                dp = lax.dot_general(dog, vc, (((1,), (1,)), ((), ())), preferred_element_type=F32)
                delta = jnp.sum(p * dp, axis=-1, keepdims=True)
                ds = ((p * (dp - delta)) * scale).astype(BF16)
                dq_ref[g] = jnp.dot(ds, kc, preferred_element_type=F32).astype(dq_ref.dtype)
                dkc = dkc + lax.dot_general(ds, qg, (((0,), (0,)), ((), ())), preferred_element_type=F32)
                dvc = dvc + lax.dot_general(p.astype(BF16), dog, (((0,), (0,)), ((), ())),
                                            preferred_element_type=F32)
                dsink = jnp.sum(-psink * delta, axis=0, keepdims=True)
                ds_ref[g:g + 1, :] += jnp.broadcast_to(dsink, (1, LANES))
            for o in range(3):
                slot = (n + 2 + o) % 3
                dk_acc[slot] += dkc[o * ATTN_BLOCK:(o + 1) * ATTN_BLOCK]
                dv_acc[slot] += dvc[o * ATTN_BLOCK:(o + 1) * ATTN_BLOCK]

        done = (n + 2) % 3
        dk_ref[...] = dk_acc[done].astype(dk_ref.dtype)
        dv_ref[...] = dv_acc[done].astype(dv_ref.dtype)

    return pl.pallas_call(
        body, name=name,
        out_shape=(jax.ShapeDtypeStruct(q.shape, BF16), jax.ShapeDtypeStruct(k.shape, BF16),
                   jax.ShapeDtypeStruct(v.shape, BF16), jax.ShapeDtypeStruct((KV, G, LANES), F32)),
        grid=(KV, nb + 1),
        in_specs=[qspec] + kspecs + kspecs + [bspec, sspec, qspec],
        out_specs=(qspec, kv_out, kv_out, pl.BlockSpec((None, G, LANES), lambda c, n: (c, 0, 0))),
        scratch_shapes=[pltpu.VMEM((3, ATTN_BLOCK, dh), F32), pltpu.VMEM((3, ATTN_BLOCK, dh), F32)],
        compiler_params=_params("parallel", "arbitrary"),
    )(q, k, k, k, v, v, v, bias, sink, do)


def _heads_major(x, n_heads):
    S = x.shape[0]
    return x.reshape(S, n_heads, -1).transpose(1, 0, 2)


def _heads_minor(x):
    H, S, dh = x.shape
    return x.transpose(1, 0, 2).reshape(S, H * dh)


def swa_layer_fwd(x, norm_g, w_qkv, w_o, sink, tag):
    S, D = x.shape
    H = w_o.shape[1] // ATTN_HEAD_DIM
    KV = H // ATTN_GROUP
    h, rstd = rms_fwd(x, norm_g, BF16, f"swa_norm_{tag}")
    qkv = mm_nn(h, w_qkv, BF16, f"swa_qkv_{tag}", tn_cap=1280)
    q = _heads_major(qkv[:, :H * ATTN_HEAD_DIM], H).reshape(KV, ATTN_GROUP, S, ATTN_HEAD_DIM)
    k = _heads_major(qkv[:, H * ATTN_HEAD_DIM:(H + KV) * ATTN_HEAD_DIM], KV)
    v = _heads_major(qkv[:, (H + KV) * ATTN_HEAD_DIM:], KV)
    bias = _swa_bias(H)
    sinkb = jnp.broadcast_to(sink.astype(F32).reshape(KV, ATTN_GROUP, 1, 1), (KV, ATTN_GROUP, 1, LANES))
    o = swa_fwd(q, k, v, bias, sinkb, f"swa_attn_{tag}")
    o2 = _heads_minor(o.reshape(H, S, ATTN_HEAD_DIM))
    y = mm_nn_res(o2, w_o, x, f"swa_out_{tag}")
    return y, (x, h, rstd, q, k, v, bias, sinkb, o2)


def swa_layer_bwd(dy, saved, norm_g, w_qkv, w_o, tag):
    x, h, rstd, q, k, v, bias, sinkb, o2 = saved
    KV, G, S, dh = q.shape
    H = KV * G
    dw_o = mm_tn(o2, dy, 1, BF16, f"swa_dwo_{tag}")
    do2 = mm_nt(dy, w_o, BF16, f"swa_do_{tag}")
    do = _heads_major(do2, H).reshape(KV, G, S, dh)
    dq, dk, dv, dsink = swa_bwd(q, k, v, bias, sinkb, do, f"swa_dattn_{tag}")
    dqkv = jnp.concatenate([_heads_minor(dq.reshape(H, S, dh)), _heads_minor(dk), _heads_minor(dv)], axis=1)
    dw_qkv = mm_tn(h, dqkv, 1, BF16, f"swa_dwqkv_{tag}", tn_cap=1280)
    dh_ = mm_nt(dqkv, w_qkv, F32, f"swa_dh_{tag}", tc_cap=2560)
    dx, dnorm = rms_bwd(dh_, x, rstd, norm_g, f"swa_dnorm_{tag}", dres=dy)
    return dx, (dnorm, dw_qkv, dw_o, dsink[:, :, 0].reshape(H))


def flash_fwd(q, k, v, scale, name, tq=512, tk=512):
    H, S, dk = q.shape
    dv = v.shape[-1]
    tq = _divisor(S, tq, 16)
    tk = _divisor(S, tk, LANES)
    nk = S // tk

    def body(q_ref, k_ref, v_ref, o_ref, lse_ref, m_sc, l_sc, acc_sc):
        ki = pl.program_id(2)

        @pl.when(ki == 0)
        def _():
            m_sc[...] = jnp.full(m_sc.shape, MASK_VALUE, F32)
            l_sc[...] = jnp.zeros(l_sc.shape, F32)
            acc_sc[...] = jnp.zeros(acc_sc.shape, F32)

        s = lax.dot_general(q_ref[...], k_ref[...], (((1,), (1,)), ((), ())), preferred_element_type=F32) * scale
        m_prev = m_sc[...]
        m_new = jnp.maximum(m_prev, jnp.max(s, axis=-1, keepdims=True))
        alpha = jnp.exp(m_prev - m_new)
        p = jnp.exp(s - m_new)
        l_sc[...] = alpha * l_sc[...] + jnp.sum(p, axis=-1, keepdims=True)
        acc_sc[...] = alpha * acc_sc[...] + jnp.dot(p.astype(BF16), v_ref[...], preferred_element_type=F32)
        m_sc[...] = m_new

        @pl.when(ki == nk - 1)
        def _():
            o_ref[...] = (acc_sc[...] / l_sc[...]).astype(o_ref.dtype)
            lse_ref[...] = m_sc[...] + jnp.log(l_sc[...])

    return pl.pallas_call(
        body, name=name,
        out_shape=(jax.ShapeDtypeStruct((H, S, dv), BF16), jax.ShapeDtypeStruct((H, S, 1), F32)),
        grid=(H, S // tq, nk),
        in_specs=[pl.BlockSpec((None, tq, dk), lambda h, i, j: (h, i, 0)),
                  pl.BlockSpec((None, tk, dk), lambda h, i, j: (h, j, 0)),
                  pl.BlockSpec((None, tk, dv), lambda h, i, j: (h, j, 0))],
        out_specs=(pl.BlockSpec((None, tq, dv), lambda h, i, j: (h, i, 0)),
                   pl.BlockSpec((None, tq, 1), lambda h, i, j: (h, i, 0))),
        scratch_shapes=[pltpu.VMEM((tq, 1), F32), pltpu.VMEM((tq, 1), F32), pltpu.VMEM((tq, dv), F32)],
        compiler_params=_params("parallel", "parallel", "arbitrary"),
    )(q, k, v)


def flash_delta(o, do, name, ts=1024):
    H, S, dv = o.shape
    ts = _divisor(S, ts, 16)

    def body(o_ref, do_ref, d_ref):
        d_ref[...] = jnp.sum(o_ref[...].astype(F32) * do_ref[...].astype(F32), axis=-1, keepdims=True)

    spec = pl.BlockSpec((None, ts, dv), lambda h, i: (h, i, 0))
    return pl.pallas_call(
        body, name=name, out_shape=jax.ShapeDtypeStruct((H, S, 1), F32), grid=(H, S // ts),
        in_specs=[spec, spec], out_specs=pl.BlockSpec((None, ts, 1), lambda h, i: (h, i, 0)),
        compiler_params=_params("parallel", "parallel"),
    )(o, do)


def flash_bwd_dq(q, k, v, do, lse, delta, scale, name, tq=512, tk=512):
    H, S, dk = q.shape
    dv = v.shape[-1]
    tq = _divisor(S, tq, 16)
    tk = _divisor(S, tk, LANES)
    nk = S // tk

    def body(q_ref, k_ref, v_ref, do_ref, lse_ref, dl_ref, dq_ref, acc_sc):
        ki = pl.program_id(2)
        kk = k_ref[...]
        s = lax.dot_general(q_ref[...], kk, (((1,), (1,)), ((), ())), preferred_element_type=F32) * scale
        p = jnp.exp(s - lse_ref[...])
        dp = lax.dot_general(do_ref[...], v_ref[...], (((1,), (1,)), ((), ())), preferred_element_type=F32)
        ds = ((p * (dp - dl_ref[...])) * scale).astype(BF16)
        part = jnp.dot(ds, kk, preferred_element_type=F32)

        @pl.when(ki == 0)
        def _():
            acc_sc[...] = part

        @pl.when(ki > 0)
        def _():
            acc_sc[...] += part

        @pl.when(ki == nk - 1)
        def _():
            dq_ref[...] = acc_sc[...].astype(dq_ref.dtype)

    qs = pl.BlockSpec((None, tq, dk), lambda h, i, j: (h, i, 0))
    col = pl.BlockSpec((None, tq, 1), lambda h, i, j: (h, i, 0))
    return pl.pallas_call(
        body, name=name, out_shape=jax.ShapeDtypeStruct((H, S, dk), F32), grid=(H, S // tq, nk),
        in_specs=[qs, pl.BlockSpec((None, tk, dk), lambda h, i, j: (h, j, 0)),
                  pl.BlockSpec((None, tk, dv), lambda h, i, j: (h, j, 0)),
                  pl.BlockSpec((None, tq, dv), lambda h, i, j: (h, i, 0)), col, col],
        out_specs=qs, scratch_shapes=[pltpu.VMEM((tq, dk), F32)],
        compiler_params=_params("parallel", "parallel", "arbitrary"),
    )(q, k, v, do, lse, delta)


def flash_bwd_dkv(q, k, v, do, lse_row, delta_row, scale, name, tq=512, tk=512):
    H, S, dk = q.shape
    dv = v.shape[-1]
    tq = _divisor(S, tq, LANES)
    tk = _divisor(S, tk, 16)
    nq = S // tq

    def body(q_ref, k_ref, v_ref, do_ref, lse_ref, dl_ref, dk_ref, dv_ref, dk_sc, dv_sc):
        qi = pl.program_id(2)
        qq = q_ref[...]
        dd = do_ref[...]
        st = lax.dot_general(k_ref[...], qq, (((1,), (1,)), ((), ())), preferred_element_type=F32) * scale
        pt = jnp.exp(st - lse_ref[...])
        dpt = lax.dot_general(v_ref[...], dd, (((1,), (1,)), ((), ())), preferred_element_type=F32)
        dst = ((pt * (dpt - dl_ref[...])) * scale).astype(BF16)
        dv_part = jnp.dot(pt.astype(BF16), dd, preferred_element_type=F32)
        dk_part = jnp.dot(dst, qq, preferred_element_type=F32)

        @pl.when(qi == 0)
        def _():
            dk_sc[...] = dk_part
            dv_sc[...] = dv_part

        @pl.when(qi > 0)
        def _():
            dk_sc[...] += dk_part
            dv_sc[...] += dv_part

        @pl.when(qi == nq - 1)
        def _():
            dk_ref[...] = dk_sc[...].astype(dk_ref.dtype)
            dv_ref[...] = dv_sc[...].astype(dv_ref.dtype)

    ks = pl.BlockSpec((None, tk, dk), lambda h, j, i: (h, j, 0))
    vs = pl.BlockSpec((None, tk, dv), lambda h, j, i: (h, j, 0))
    row = pl.BlockSpec((None, 1, tq), lambda h, j, i: (h, 0, i))
    return pl.pallas_call(
        body, name=name,
        out_shape=(jax.ShapeDtypeStruct((H, S, dk), F32), jax.ShapeDtypeStruct((H, S, dv), BF16)),
        grid=(H, S // tk, nq),
        in_specs=[pl.BlockSpec((None, tq, dk), lambda h, j, i: (h, i, 0)), ks, vs,
                  pl.BlockSpec((None, tq, dv), lambda h, j, i: (h, i, 0)), row, row],
        out_specs=(ks, vs), scratch_shapes=[pltpu.VMEM((tk, dk), F32), pltpu.VMEM((tk, dv), F32)],
        compiler_params=_params("parallel", "parallel", "arbitrary"),
    )(q, k, v, do, lse_row, delta_row)


def _rope_tables(S, reps):
    half = MLA_ROPE // 2
    pos = jnp.arange(S, dtype=F32)
    inv = ROPE_THETA ** (-jnp.arange(half, dtype=F32) / half)
    ang = pos[:, None] * inv[None, :]
    return jnp.tile(jnp.cos(ang), (1, reps)), jnp.tile(jnp.sin(ang), (1, reps))


def _rotate(x1, x2, cos, sin, out_dtype, name):
    return ew(lambda a, b, c, s: (a * c - b * s, b * c + a * s), name, [x1, x2, cos, sin], [], [out_dtype, out_dtype])


def mla_layer_fwd(x, norm_g, w_dqkv, q_norm, kv_norm, w_uq, w_ukv, w_o, tag):
    S, D = x.shape
    QL, KL = q_norm.shape[1], kv_norm.shape[1]
    H = w_o.shape[1] // MLA_V
    half = MLA_ROPE // 2
    h, rstd = rms_fwd(x, norm_g, BF16, f"mla_norm_{tag}")
    d = mm_nn(h, w_dqkv, F32, f"mla_down_{tag}")
    c_q, c_kv, k_rope = d[:, :QL], d[:, QL:QL + KL], d[:, QL + KL:]
    cq_n, rstd_q = rms_fwd(c_q, q_norm, BF16, f"mla_qnorm_{tag}")
    ckv_n, rstd_kv = rms_fwd(c_kv, kv_norm, BF16, f"mla_kvnorm_{tag}")
    q = mm_nn(cq_n, w_uq, F32, f"mla_uq_{tag}").reshape(S, H, MLA_NOPE + MLA_ROPE)
    kv = mm_nn(ckv_n, w_ukv, BF16, f"mla_ukv_{tag}").reshape(S, H, MLA_NOPE + MLA_V)
    cos, sin = _rope_tables(S, H + 1)
    x1 = jnp.concatenate([q[:, :, MLA_NOPE:MLA_NOPE + half].reshape(S, H * half), k_rope[:, :half]], axis=1)
    x2 = jnp.concatenate([q[:, :, MLA_NOPE + half:].reshape(S, H * half), k_rope[:, half:]], axis=1)
    r1, r2 = _rotate(x1, x2, cos, sin, BF16, f"mla_rope_{tag}")
    qr = jnp.concatenate([r1[:, :H * half].reshape(S, H, half), r2[:, :H * half].reshape(S, H, half)], axis=2)
    kr = jnp.concatenate([r1[:, H * half:], r2[:, H * half:]], axis=1)
    qh = jnp.concatenate([q[:, :, :MLA_NOPE].astype(BF16), qr], axis=2).transpose(1, 0, 2)
    kh = jnp.concatenate([kv[:, :, :MLA_NOPE], jnp.broadcast_to(kr[:, None, :], (S, H, MLA_ROPE))],
                         axis=2).transpose(1, 0, 2)
    vh = kv[:, :, MLA_NOPE:].transpose(1, 0, 2)
    scale = (MLA_NOPE + MLA_ROPE) ** -0.5
    oh, lse = flash_fwd(qh, kh, vh, scale, f"mla_attn_{tag}")
    o2 = _heads_minor(oh)
    y = mm_nn_res(o2, w_o, x, f"mla_out_{tag}")
    return y, (x, h, rstd, c_q, rstd_q, cq_n, c_kv, rstd_kv, ckv_n, qh, kh, vh, oh, lse, o2, cos, sin)


def mla_layer_bwd(dy, saved, norm_g, w_dqkv, q_norm, kv_norm, w_uq, w_ukv, w_o, tag):
    x, h, rstd, c_q, rstd_q, cq_n, c_kv, rstd_kv, ckv_n, qh, kh, vh, oh, lse, o2, cos, sin = saved
    H, S, _ = qh.shape
    half = MLA_ROPE // 2
    scale = (MLA_NOPE + MLA_ROPE) ** -0.5
    dw_o = mm_tn(o2, dy, 1, BF16, f"mla_dwo_{tag}")
    do2 = mm_nt(dy, w_o, BF16, f"mla_do_{tag}")
    doh = _heads_major(do2, H)
    delta = flash_delta(oh, doh, f"mla_delta_{tag}")
    dqh = flash_bwd_dq(qh, kh, vh, doh, lse, delta, scale, f"mla_dq_{tag}")
    dkh, dvh = flash_bwd_dkv(qh, kh, vh, doh, lse.reshape(H, 1, S), delta.reshape(H, 1, S), scale, f"mla_dkv_{tag}")
    dq = dqh.transpose(1, 0, 2)
    dk = dkh.transpose(1, 0, 2)
    dkr = jnp.sum(dk[:, :, MLA_NOPE:], axis=1)
    g1 = jnp.concatenate([dq[:, :, MLA_NOPE:MLA_NOPE + half].reshape(S, H * half), dkr[:, :half]], axis=1)
    g2 = jnp.concatenate([dq[:, :, MLA_NOPE + half:].reshape(S, H * half), dkr[:, half:]], axis=1)
    b1, b2 = _rotate(g1, g2, cos, -sin, F32, f"mla_drope_{tag}")
    dq_rope = jnp.concatenate([b1[:, :H * half].reshape(S, H, half), b2[:, :H * half].reshape(S, H, half)], axis=2)
    dk_rope = jnp.concatenate([b1[:, H * half:], b2[:, H * half:]], axis=1)
    dq_full = jnp.concatenate([dq[:, :, :MLA_NOPE], dq_rope], axis=2).reshape(S, -1).astype(BF16)
    dkv = jnp.concatenate([dk[:, :, :MLA_NOPE].astype(BF16), dvh.transpose(1, 0, 2)], axis=2).reshape(S, -1)
    dw_uq = mm_tn(cq_n, dq_full, w_uq.shape[0], BF16, f"mla_dwuq_{tag}")
    dw_ukv = mm_tn(ckv_n, dkv, w_ukv.shape[0], BF16, f"mla_dwukv_{tag}")
    dcq_n = mm_nt(dq_full, w_uq, F32, f"mla_dcq_{tag}")
    dckv_n = mm_nt(dkv, w_ukv, F32, f"mla_dckv_{tag}")
    dc_q, dq_norm = rms_bwd(dcq_n, c_q, rstd_q, q_norm, f"mla_dqnorm_{tag}")
    dc_kv, dkv_norm = rms_bwd(dckv_n, c_kv, rstd_kv, kv_norm, f"mla_dkvnorm_{tag}")
    dd = jnp.concatenate([dc_q, dc_kv, dk_rope], axis=1).astype(BF16)
    dw_dqkv = mm_tn(h, dd, 1, BF16, f"mla_dwdown_{tag}")
    dh_ = mm_nt(dd, w_dqkv, F32, f"mla_dh_{tag}")
    dx, dnorm = rms_bwd(dh_, x, rstd, norm_g, f"mla_dnorm_{tag}", dres=dy)
    return dx, (dnorm, dw_dqkv, dq_norm, dkv_norm, dw_uq, dw_ukv, dw_o)


SLAB = LANES
SLAB_GROUPS = SLAB // SSM_GROUP_CH
SLAB_HALF = SLAB_GROUPS * SSM_STATE
SLAB_W = 2 * SLAB_HALF


def _scan_rows(st_ref, carry_ref, lam_ref, nt, rev):
    h = SLAB_HALF
    lr = lam_ref[:, :h]
    li = lam_ref[:, h:]

    def step(i, c):
        xr, xi = c
        ii = (nt - 1 - i) if rev else i
        row = pl.multiple_of(ii * SEGS, SEGS)
        nr = lr * xr - li * xi + st_ref[pl.ds(row, SEGS), :h]
        ni = lr * xi + li * xr + st_ref[pl.ds(row, SEGS), h:]
        st_ref[pl.ds(row, SEGS), :h] = nr
        st_ref[pl.ds(row, SEGS), h:] = ni
        return nr, ni

    xr, xi = lax.fori_loop(0, nt, step, (carry_ref[:, :h], carry_ref[:, h:]), unroll=4)
    carry_ref[:, :h] = xr
    carry_ref[:, h:] = xi


def s5_scan(mode, inp, win, lam, rev, name, init=None, wout=None, xs=None, xinit=None, u=None, rows=512):
    T, C = inp.shape
    K = win.shape[0]
    W = SLAB_W
    Tc = _divisor(T, rows, 16)
    nt = Tc // SEGS
    nT = T // Tc
    tiles = T // SEGS

    def chunk(jj):
        return (nT - 1 - jj) if rev else jj

    slab_in = pl.BlockSpec((Tc, SLAB), lambda k, jj: (chunk(jj), k))
    wspec = pl.BlockSpec((None, SLAB, W), lambda k, jj: (k, 0, 0))
    vspec = pl.BlockSpec((None, SEGS, W), lambda k, jj: (k, 0, 0))
    wospec = pl.BlockSpec((None, W, SLAB), lambda k, jj: (k, 0, 0))
    xspec = pl.BlockSpec((Tc, W), lambda k, jj: (chunk(jj), k))
    scratch = [pltpu.VMEM((Tc, W), F32), pltpu.VMEM((SEGS, W), F32)]
    sem = _params("parallel", "arbitrary")

    def project_in(in_ref, w_ref, st_ref):
        st_ref[...] = jnp.dot(in_ref[...].astype(BF16), w_ref[...], preferred_element_type=F32)

    if mode == "finals":
        def body(in_ref, w_ref, lam_ref, fin_ref, st_ref, carry_ref):
            jj = pl.program_id(1)

            @pl.when(jj == 0)
            def _():
                carry_ref[...] = jnp.zeros_like(carry_ref)

            project_in(in_ref, w_ref, st_ref)
            _scan_rows(st_ref, carry_ref, lam_ref, nt, rev)

            @pl.when(jj == nT - 1)
            def _():
                fin_ref[...] = carry_ref[...]

        return pl.pallas_call(
            body, name=name, out_shape=jax.ShapeDtypeStruct((K, SEGS, W), F32), grid=(K, nT),
            in_specs=[slab_in, wspec, vspec], out_specs=vspec, scratch_shapes=scratch, compiler_params=sem,
        )(inp, win, lam)

    if mode == "fwd":
        def body(in_ref, w_ref, lam_ref, init_ref, wo_ref, xs_ref, y_ref, st_ref, carry_ref):
            jj = pl.program_id(1)

            @pl.when(jj == 0)
            def _():
                carry_ref[...] = init_ref[...]

            project_in(in_ref, w_ref, st_ref)
            _scan_rows(st_ref, carry_ref, lam_ref, nt, rev)
            xs = st_ref[...]
            xs_ref[...] = xs
            y_ref[...] = jnp.dot(xs.astype(BF16), wo_ref[...], preferred_element_type=F32)

        return pl.pallas_call(
            body, name=name,
            out_shape=(jax.ShapeDtypeStruct((T, K * W), F32), jax.ShapeDtypeStruct((T, C), F32)), grid=(K, nT),
            in_specs=[slab_in, wspec, vspec, vspec, wospec], out_specs=(xspec, slab_in),
            scratch_shapes=scratch, compiler_params=sem,
        )(inp, win, lam, init, wout)

    assert mode == "bwd"
    x_rev = not rev

    def halo_index(k, jj):
        ch = chunk(jj)
        tile = jnp.minimum((ch + 1) * nt, tiles - 1) if x_rev else jnp.maximum(ch * nt - 1, 0)
        return (tile, k)

    halo = pl.BlockSpec((SEGS, W), halo_index)

    def body(in_ref, w_ref, lam_ref, init_ref, wo_ref, xs_ref, xh_ref, xi_ref, u_ref,
             du_ref, dwin_ref, dwout_ref, dlam_ref, st_ref, carry_ref):
        jj = pl.program_id(1)
        ch = chunk(jj)

        @pl.when(jj == 0)
        def _():
            carry_ref[...] = init_ref[...]

        g = in_ref[...].astype(BF16)
        st_ref[...] = jnp.dot(g, w_ref[...], preferred_element_type=F32)
        _scan_rows(st_ref, carry_ref, lam_ref, nt, rev)
        adj = st_ref[...]
        adj16 = adj.astype(BF16)
        du_ref[...] = jnp.dot(adj16, wo_ref[...], preferred_element_type=F32)
        xs = xs_ref[...]
        edge = (ch == nT - 1) if x_rev else (ch == 0)
        first = jnp.where(edge, xi_ref[...], xh_ref[...])
        if x_rev:
            xp = jnp.concatenate([xs[SEGS:], first], axis=0)
        else:
            xp = jnp.concatenate([first, xs[:Tc - SEGS]], axis=0)
        h = SLAB_HALF
        ar, ai, pr, pi = adj[:, :h], adj[:, h:], xp[:, :h], xp[:, h:]
        dlr = (ar * pr + ai * pi).reshape(nt, SEGS, h).sum(axis=0)
        dli = (ai * pr - ar * pi).reshape(nt, SEGS, h).sum(axis=0)
        dwin = lax.dot_general(u_ref[...].astype(BF16), adj16, (((0,), (0,)), ((), ())), preferred_element_type=F32)
        dwout = lax.dot_general(xs.astype(BF16), g, (((0,), (0,)), ((), ())), preferred_element_type=F32)

        @pl.when(jj == 0)
        def _():
            dwin_ref[...] = dwin
            dwout_ref[...] = dwout
            dlam_ref[:, :h] = dlr
            dlam_ref[:, h:] = dli

        @pl.when(jj > 0)
        def _():
            dwin_ref[...] += dwin
            dwout_ref[...] += dwout
            dlam_ref[:, :h] += dlr
            dlam_ref[:, h:] += dli

    return pl.pallas_call(
        body, name=name,
        out_shape=(jax.ShapeDtypeStruct((T, C), F32), jax.ShapeDtypeStruct((K, SLAB, W), F32),
                   jax.ShapeDtypeStruct((K, W, SLAB), F32), jax.ShapeDtypeStruct((K, SEGS, W), F32)),
        grid=(K, nT),
        in_specs=[slab_in, wspec, vspec, vspec, wospec, xspec, halo, vspec, slab_in],
        out_specs=(slab_in, wspec, wospec, vspec), scratch_shapes=scratch, compiler_params=sem,
    )(inp, win, lam, init, wout, xs, xs, xinit, u)


def _s5_discretize(a_re, a_im, log_step, b_re, b_im):
    step = jnp.exp(log_step)[:, None]
    mag = jnp.exp(step * a_re)
    lb_re = mag * jnp.cos(step * a_im)
    lb_im = mag * jnp.sin(step * a_im)
    n_re, n_im = lb_re - 1.0, lb_im
    den = a_re * a_re + a_im * a_im
    coef_re = (n_re * a_re + n_im * a_im) / den
    coef_im = (n_im * a_re - n_re * a_im) / den
    bb_re = coef_re[..., None] * b_re - coef_im[..., None] * b_im
    bb_im = coef_re[..., None] * b_im + coef_im[..., None] * b_re
    return lb_re, lb_im, bb_re, bb_im


def _slab_in_matrix(bb_re, bb_im):
    G, N, Cg = bb_re.shape
    K = G // SLAB_GROUPS
    eye = jnp.eye(SLAB_GROUPS, dtype=F32)
    parts = [jnp.einsum('kgnc,gh->kgchn', b.reshape(K, SLAB_GROUPS, N, Cg), eye).reshape(K, SLAB, SLAB_HALF)
             for b in (bb_re, bb_im)]
    return jnp.concatenate(parts, axis=2)


def _slab_in_unpack(m):
    K = m.shape[0]
    m6 = m.reshape(K, SLAB_GROUPS, SSM_GROUP_CH, 2, SLAB_GROUPS, SSM_STATE)
    d = jnp.einsum('kgcphn,gh->pkgnc', m6, jnp.eye(SLAB_GROUPS, dtype=F32))
    d = d.reshape(2, K * SLAB_GROUPS, SSM_STATE, SSM_GROUP_CH)
    return d[0], d[1]


def _slab_out_matrix(c_re, c_im):
    G, Cg, N = c_re.shape
    K = G // SLAB_GROUPS
    eye = jnp.eye(SLAB_GROUPS, dtype=F32)
    parts = [jnp.einsum('kgcn,gh->kgnhc', c.reshape(K, SLAB_GROUPS, Cg, N), eye).reshape(K, SLAB_HALF, SLAB)
             for c in (c_re, -c_im)]
    return jnp.concatenate(parts, axis=1)


def _slab_out_unpack(m):
    K = m.shape[0]
    m6 = m.reshape(K, 2, SLAB_GROUPS, SSM_STATE, SLAB_GROUPS, SSM_GROUP_CH)
    d = jnp.einsum('kpgnhc,gh->pkgcn', m6, jnp.eye(SLAB_GROUPS, dtype=F32))
    d = d.reshape(2, K * SLAB_GROUPS, SSM_GROUP_CH, SSM_STATE)
    return d[0], -d[1]


def _slab_vec(re, im):
    K = re.shape[0] // SLAB_GROUPS
    v = jnp.concatenate([re.reshape(K, SLAB_HALF), im.reshape(K, SLAB_HALF)], axis=1)
    return jnp.broadcast_to(v[:, None, :], (K, SEGS, SLAB_W))


def _segment_inits(fin, lam, seg_len, rev):
    h = SLAB_HALF
    pr, pi = lam[:, 0, :h], lam[:, 0, h:]
    steps = int(round(math.log2(seg_len)))
    assert 2 ** steps == seg_len
    for _ in range(steps):
        pr, pi = pr * pr - pi * pi, 2.0 * pr * pi
    cr = jnp.zeros_like(pr)
    ci = jnp.zeros_like(pi)
    inits = [None] * SEGS
    for s in (range(SEGS - 1, -1, -1) if rev else range(SEGS)):
        inits[s] = jnp.concatenate([cr, ci], axis=1)
        cr, ci = pr * cr - pi * ci + fin[:, s, :h], pr * ci + pi * cr + fin[:, s, h:]
    return jnp.stack(inits, axis=1)


def _time_permute(x):
    T, C = x.shape
    return x.reshape(SEGS, T // SEGS, C).transpose(1, 0, 2).reshape(T, C)


def _time_unpermute(x):
    T, C = x.shape
    return x.reshape(T // SEGS, SEGS, C).transpose(1, 0, 2).reshape(T, C)


_GELU_K = math.sqrt(2.0 / math.pi)
_GELU_A = 0.044715


def _gelu_grad(y):
    t = jnp.tanh(_GELU_K * (y + _GELU_A * y * y * y))
    return 0.5 * (1.0 + t) + 0.5 * y * (1.0 - t * t) * (_GELU_K * (1.0 + 3.0 * _GELU_A * y * y))


def _conj(lam):
    return jnp.concatenate([lam[:, :, :SLAB_HALF], -lam[:, :, SLAB_HALF:]], axis=2)


def s5_layer_fwd(x, norm_g, ssm, w_glu, tag):
    S, D = x.shape
    u_nat, rstd = rms_fwd(x, norm_g, F32, f"s5_norm_{tag}")
    u = _time_permute(u_nat)
    dirs = []
    ys = []
    for dr in range(2):
        rev = dr == 1
        lb_re, lb_im, bb_re, bb_im = _s5_discretize(ssm["a_re"][dr], ssm["a_im"][dr], ssm["log_step"][dr],
                                                    ssm["b_re"][dr], ssm["b_im"][dr])
        win = _slab_in_matrix(bb_re, bb_im).astype(BF16)
        wout = _slab_out_matrix(ssm["c_re"][dr], ssm["c_im"][dr]).astype(BF16)
        lam = _slab_vec(lb_re, lb_im)
        fin = s5_scan("finals", u, win, lam, rev, f"s5_fin{dr}_{tag}")
        init = _segment_inits(fin, lam, S // SEGS, rev)
        xs, y = s5_scan("fwd", u, win, lam, rev, f"s5_fwd{dr}_{tag}", init=init, wout=wout)
        dirs.append((win, wout, lam, init, xs))
        ys.append(y)
    yy, zb = ew(lambda uu, a, b, d: (d * uu + a + b, jax.nn.gelu(d * uu + a + b)), f"s5_y_{tag}",
                [u, ys[0], ys[1]], [ssm["d"]], [F32, BF16])
    lin = mm_nn(zb, w_glu, F32, f"s5_glu_{tag}", tn_cap=512)
    mix = ew(lambda y_, l_, b: jax.nn.gelu(y_) * jax.nn.sigmoid(l_ + b), f"s5_mix_{tag}",
             [yy, lin], [ssm["b_glu"]], [F32])[0]
    out = x + _time_unpermute(mix)
    return out, (x, rstd, u, dirs, yy, zb, lin)


def s5_layer_bwd(dy, saved, norm_g, ssm, w_glu, tag):
    x, rstd, u, dirs, yy, zb, lin = saved
    S, D = x.shape
    dmix = _time_permute(dy)

    def glu_back(dm, y_, l_, b):
        z = jax.nn.gelu(y_)
        sg = jax.nn.sigmoid(l_ + b)
        dlin = dm * z * (sg * (1.0 - sg))
        return dlin, dm * sg, dlin

    dlin, dz_direct, db_glu = ew(glu_back, f"s5_dmix_{tag}", [dmix, yy, lin], [ssm["b_glu"]], [BF16, F32], n_sums=1)
    dw_glu = mm_tn(zb, dlin, 1, BF16, f"s5_dwglu_{tag}", tn_cap=512)
    dz_mm = mm_nt(dlin, w_glu, F32, f"s5_dz_{tag}")

    def gelu_back(a, b, y_, uu):
        dyy = (a + b) * _gelu_grad(y_)
        return dyy, dyy * uu

    dyy, dd = ew(gelu_back, f"s5_dy_{tag}", [dz_direct, dz_mm, yy, u], [], [F32], n_sums=1)
    grads = {"d": dd, "b_glu": db_glu, "w_glu": dw_glu}
    dus = []
    per_dir = []
    for dr in range(2):
        rev = dr == 1
        win, wout, lam, xinit, xs = dirs[dr]
        lamc = _conj(lam)
        ein = wout.transpose(0, 2, 1)
        eout = win.transpose(0, 2, 1)
        fin = s5_scan("finals", dyy, ein, lamc, not rev, f"s5_bfin{dr}_{tag}")
        init = _segment_inits(fin, lamc, S // SEGS, not rev)
        du, dwin, dwout, dlam = s5_scan("bwd", dyy, ein, lamc, not rev, f"s5_bwd{dr}_{tag}", init=init, wout=eout,
                                        xs=xs, xinit=xinit, u=u)
        dus.append(du)
        dbb_re, dbb_im = _slab_in_unpack(dwin)
        dc_re, dc_im = _slab_out_unpack(dwout)
        dl = dlam.sum(axis=1)
        dlb_re = dl[:, :SLAB_HALF].reshape(-1, SSM_STATE)
        dlb_im = dl[:, SLAB_HALF:].reshape(-1, SSM_STATE)
        prm = (ssm["a_re"][dr], ssm["a_im"][dr], ssm["log_step"][dr], ssm["b_re"][dr], ssm["b_im"][dr])
        _, vjp = jax.vjp(_s5_discretize, *prm)
        per_dir.append(vjp((dlb_re, dlb_im, dbb_re, dbb_im)) + (dc_re, dc_im))
    for i, nm in enumerate(["a_re", "a_im", "log_step", "b_re", "b_im", "c_re", "c_im"]):
        grads[nm] = jnp.stack([per_dir[0][i], per_dir[1][i]], axis=0)
    du_p = ew(lambda g, a, b, d: d * g + a + b, f"s5_du_{tag}", [dyy, dus[0], dus[1]], [ssm["d"]], [F32])[0]
    dx, dnorm = rms_bwd(_time_unpermute(du_p), x, rstd, norm_g, f"s5_dnorm_{tag}", dres=dy)
    grads["norm"] = dnorm
    return dx, grads


def final_loss(x, g, target, name, ts=512):
    S, D = x.shape
    ts = _divisor(S, ts, 16)

    def body(x_ref, g_ref, t_ref, loss_ref, dx_ref, dg_ref):
        i = pl.program_id(0)
        x = x_ref[...]
        gg = g_ref[...]
        r = lax.rsqrt(jnp.mean(x * x, axis=-1, keepdims=True) + RMS_EPS)
        xhat = x * r
        err = xhat * gg - t_ref[...]
        row_loss = jnp.mean(err * err, axis=-1, keepdims=True)
        part = jnp.broadcast_to(0.5 * jnp.sum(row_loss, axis=0, keepdims=True), (1, LANES))
        dy = err * (1.0 / D)
        dhg = dy * gg
        c = jnp.mean(dhg * xhat, axis=-1, keepdims=True)
        dx_ref[...] = r * (dhg - xhat * c)
        dg = jnp.sum(dy * xhat, axis=0, keepdims=True)

        @pl.when(i == 0)
        def _():
            loss_ref[...] = part
            dg_ref[...] = dg

        @pl.when(i > 0)
        def _():
            loss_ref[...] += part
            dg_ref[...] += dg

    row = pl.BlockSpec((ts, D), lambda i: (i, 0))
    vec = pl.BlockSpec((1, D), lambda i: (0, 0))
    return pl.pallas_call(
        body, name=name,
        out_shape=(jax.ShapeDtypeStruct((1, LANES), F32), jax.ShapeDtypeStruct((S, D), F32),
                   jax.ShapeDtypeStruct((1, D), F32)),
        grid=(S // ts,), in_specs=[row, vec, row],
        out_specs=(pl.BlockSpec((1, LANES), lambda i: (0, 0)), row, vec),
        compiler_params=_params("arbitrary"),
    )(x, g, target)


FLAT_W = 8 * LANES


def _adamw_math(w, g, m, v):
    m = ADAM_B1 * m + (1.0 - ADAM_B1) * g
    v = ADAM_B2 * v + (1.0 - ADAM_B2) * (g * g)
    m_hat = m / (1.0 - ADAM_B1 ** ADAM_STEP)
    v_hat = v / (1.0 - ADAM_B2 ** ADAM_STEP)
    delta = -ADAM_LR * (m_hat / (jnp.sqrt(v_hat) + ADAM_EPS) + ADAM_WD * w)
    return delta, m, v


def _ordered_sum(parts_ref):
    total = parts_ref[0].astype(F32)
    for s in range(1, N_DEV):
        total = total + parts_ref[s].astype(F32)
    return total


def adamw_from_parts(parts, w, m, v, name, tr=512):
    R = w.shape[0]
    tr = _divisor(R, tr, 16)

    def body(p_ref, w_ref, m_ref, v_ref, g_ref, d_ref, nm_ref, nv_ref):
        g = _ordered_sum(p_ref)
        delta, nm, nv = _adamw_math(w_ref[...], g, m_ref[...], v_ref[...])
        g_ref[...] = g
        d_ref[...] = delta
        nm_ref[...] = nm
        nv_ref[...] = nv

    flat = pl.BlockSpec((tr, FLAT_W), lambda i: (i, 0))
    out = jax.ShapeDtypeStruct((R, FLAT_W), F32)
    return pl.pallas_call(
        body, name=name, out_shape=(out, out, out, out), grid=(R // tr,),
        in_specs=[pl.BlockSpec((N_DEV, tr, FLAT_W), lambda i: (0, i, 0)), flat, flat, flat],
        out_specs=(flat, flat, flat, flat), compiler_params=_params("parallel"),
    )(parts, w, m, v)


def sum_parts(parts, name, tr=512):
    R = parts.shape[1]
    tr = _divisor(R, tr, 16)

    def body(p_ref, o_ref):
        o_ref[...] = _ordered_sum(p_ref)

    return pl.pallas_call(
        body, name=name, out_shape=jax.ShapeDtypeStruct((R, FLAT_W), F32), grid=(R // tr,),
        in_specs=[pl.BlockSpec((N_DEV, tr, FLAT_W), lambda i: (0, i, 0))],
        out_specs=pl.BlockSpec((tr, FLAT_W), lambda i: (i, 0)), compiler_params=_params("parallel"),
    )(parts)


def adamw_flat(g, w, m, v, name, tr=512):
    R = w.shape[0]
    tr = _divisor(R, tr, 16)

    def body(g_ref, w_ref, m_ref, v_ref, d_ref, nm_ref, nv_ref):
        delta, nm, nv = _adamw_math(w_ref[...], g_ref[...], m_ref[...], v_ref[...])
        d_ref[...] = delta
        nm_ref[...] = nm
        nv_ref[...] = nv

    flat = pl.BlockSpec((tr, FLAT_W), lambda i: (i, 0))
    out = jax.ShapeDtypeStruct((R, FLAT_W), F32)
    return pl.pallas_call(
        body, name=name, out_shape=(out, out, out), grid=(R // tr,),
        in_specs=[flat, flat, flat, flat], out_specs=(flat, flat, flat), compiler_params=_params("parallel"),
    )(g, w, m, v)


MESH_ID = pl.DeviceIdType.MESH
HBM_SPEC = pl.BlockSpec(memory_space=pltpu.HBM)


def _position():
    x, y, c = lax.axis_index("x"), lax.axis_index("y"), lax.axis_index("c")
    return x, y, c


def _flat_index(px, py, pc):
    return 4 * px + 2 * py + pc


def all_gather(arrays, name):
    n = len(arrays)

    def body(*refs):
        ins, outs = refs[:n], refs[n:2 * n]
        send_sems, recv_sems, local_sems = refs[2 * n:]
        x, y, c = _position()
        me, sibling = (x, y, c), (x, y, 1 - c)
        chips = [(1 - x, y), (x, 1 - y), (1 - x, 1 - y)]

        def copy(a, k, block, to, src=None):
            rows = outs[a].at[_flat_index(*block)]
            return pltpu.make_async_remote_copy(
                src_ref=rows if src is None else src, dst_ref=rows,
                send_sem=send_sems.at[7 * a + k], recv_sem=recv_sems.at[7 * a + k],
                device_id=to, device_id_type=MESH_ID)

        mine, first, passed = [], [], []
        for a in range(n):
            cp = pltpu.make_async_copy(ins[a], outs[a].at[_flat_index(*me)], local_sems.at[a])
            cp.start()
            mine.append(cp)
            first.append(copy(a, 0, me, sibling, src=ins[a]))
            first += [copy(a, 1 + j, me, (*chip, c), src=ins[a]) for j, chip in enumerate(chips)]
        for cp in first:
            cp.start()
        for a in range(n):
            for j, chip in enumerate(chips):
                copy(a, 1 + j, (*chip, c), me).wait_recv()
                fwd = copy(a, 4 + j, (*chip, c), sibling)
                fwd.start()
                passed.append(fwd)
        for a in range(n):
            copy(a, 0, sibling, me).wait_recv()
            for j, chip in enumerate(chips):
                copy(a, 4 + j, (*chip, 1 - c), me).wait_recv()
        for cp in first + passed:
            cp.wait_send()
        for cp in mine:
            cp.wait()

    return pl.pallas_call(
        body, name=name,
        out_shape=tuple(jax.ShapeDtypeStruct((N_DEV,) + a.shape, a.dtype) for a in arrays),
        in_specs=[HBM_SPEC] * n, out_specs=tuple([HBM_SPEC] * n),
        scratch_shapes=[pltpu.SemaphoreType.DMA((7 * n,)), pltpu.SemaphoreType.DMA((7 * n,)),
                        pltpu.SemaphoreType.DMA((n,))],
    )(*arrays)


def exchange(slotted, whole, name):
    def body(sl_ref, wh_ref, rsl_ref, rwh_ref, send_sems, recv_sems, local_sems):
        x, y, c = _position()
        me = _flat_index(x, y, c)
        own = [pltpu.make_async_copy(sl_ref.at[me], rsl_ref.at[me], local_sems.at[0]),
               pltpu.make_async_copy(wh_ref, rwh_ref.at[me], local_sems.at[1])]
        for cp in own:
            cp.start()
        sends, recvs = [], []
        for r in range(1, N_DEV):
            peer = (1 - x if r & 4 else x, 1 - y if r & 2 else y, 1 - c if r & 1 else c)
            pidx = _flat_index(*peer)
            for a, (src, dst_mine, dst_theirs) in enumerate([(sl_ref.at[pidx], rsl_ref.at[me], rsl_ref.at[pidx]),
                                                             (wh_ref, rwh_ref.at[me], rwh_ref.at[pidx])]):
                k = 7 * a + r - 1
                sends.append(pltpu.make_async_remote_copy(
                    src_ref=src, dst_ref=dst_mine, send_sem=send_sems.at[k], recv_sem=recv_sems.at[k],
                    device_id=peer, device_id_type=MESH_ID))
                recvs.append(pltpu.make_async_remote_copy(
                    src_ref=src, dst_ref=dst_theirs, send_sem=send_sems.at[k], recv_sem=recv_sems.at[k],
                    device_id=peer, device_id_type=MESH_ID))
        for cp in sends:
            cp.start()
        for cp in recvs:
            cp.wait_recv()
        for cp in sends:
            cp.wait_send()
        for cp in own:
            cp.wait()

    return pl.pallas_call(
        body, name=name,
        out_shape=(jax.ShapeDtypeStruct(slotted.shape, slotted.dtype),
                   jax.ShapeDtypeStruct((N_DEV,) + whole.shape, whole.dtype)),
        in_specs=[HBM_SPEC, HBM_SPEC], out_specs=(HBM_SPEC, HBM_SPEC),
        scratch_shapes=[pltpu.SemaphoreType.DMA((14,)), pltpu.SemaphoreType.DMA((14,)),
                        pltpu.SemaphoreType.DMA((2,))],
    )(slotted, whole)


WEIGHTS = ['mix_norm', 'ffn_norm', 'final_norm', 'attn_w_qkv', 'attn_w_o', 'attn_sink', 'ssm_a_re', 'ssm_a_im',
           'ssm_log_step', 'ssm_b_re', 'ssm_b_im', 'ssm_c_re', 'ssm_c_im', 'ssm_d', 'ssm_w_glu', 'ssm_b_glu',
           'mla_w_dqkv', 'mla_q_norm', 'mla_kv_norm', 'mla_w_uq', 'mla_w_ukv', 'mla_w_o', 'ffn_w_up', 'ffn_conv_w',
           'ffn_conv_b', 'ffn_w_down']
BIG = [('attn_w_qkv', 'col'), ('attn_w_o', 'row'), ('ssm_w_glu', 'row'), ('mla_w_dqkv', 'row'), ('mla_w_uq', 'col'),
       ('mla_w_ukv', 'col'), ('mla_w_o', 'row'), ('ffn_w_up', 'col'), ('ffn_w_down', 'row')]
BIG_NAMES = [n for n, _ in BIG]
SMALL_SHARDED = ['mla_q_norm', 'mla_kv_norm', 'ffn_conv_w']
SMALL = [n for n in WEIGHTS if n not in BIG_NAMES]


def _pack(arrays, dtype):
    flat = jnp.concatenate([a.astype(dtype).reshape(-1) for a in arrays])
    pad = (-flat.shape[0]) % (16 * FLAT_W)
    if pad:
        flat = jnp.concatenate([flat, jnp.zeros((pad,), dtype)])
    return flat.reshape(-1, FLAT_W)


def _unpack(flat, shapes):
    flat = flat.reshape(-1)
    out, off = [], 0
    for s in shapes:
        n = int(np.prod(s))
        out.append(flat[off:off + n].reshape(s))
        off += n
    return out


def _own_slice(full, idx):
    n = full.shape[-1] // N_DEV
    return lax.dynamic_slice_in_dim(full, idx * n, n, axis=full.ndim - 1)


def kernel(x, mix_norm, ffn_norm, final_norm, attn_w_qkv, attn_w_o, attn_sink, ssm_a_re, ssm_a_im, ssm_log_step, ssm_b_re, ssm_b_im, ssm_c_re, ssm_c_im, ssm_d, ssm_w_glu, ssm_b_glu, mla_w_dqkv, mla_q_norm, mla_kv_norm, mla_w_uq, mla_w_ukv, mla_w_o, ffn_w_up, ffn_conv_w, ffn_conv_b, ffn_w_down, loss_target, m_mix_norm, m_ffn_norm, m_final_norm, m_attn_w_qkv, m_attn_w_o, m_attn_sink, m_ssm_a_re, m_ssm_a_im, m_ssm_log_step, m_ssm_b_re, m_ssm_b_im, m_ssm_c_re, m_ssm_c_im, m_ssm_d, m_ssm_w_glu, m_ssm_b_glu, m_mla_w_dqkv, m_mla_q_norm, m_mla_kv_norm, m_mla_w_uq, m_mla_w_ukv, m_mla_w_o, m_ffn_w_up, m_ffn_conv_w, m_ffn_conv_b, m_ffn_w_down, v_mix_norm, v_ffn_norm, v_final_norm, v_attn_w_qkv, v_attn_w_o, v_attn_sink, v_ssm_a_re, v_ssm_a_im, v_ssm_log_step, v_ssm_b_re, v_ssm_b_im, v_ssm_c_re, v_ssm_c_im, v_ssm_d, v_ssm_w_glu, v_ssm_b_glu, v_mla_w_dqkv, v_mla_q_norm, v_mla_kv_norm, v_mla_w_uq, v_mla_w_ukv, v_mla_w_o, v_ffn_w_up, v_ffn_conv_w, v_ffn_conv_b, v_ffn_w_down):
    given = dict(locals())
    idx = _flat_index(*_position())
    depth = mix_norm.shape[0]
    xs = x[0]
    S, D = xs.shape

    big_flat = _pack([given[n] for n in BIG_NAMES], BF16)
    small_flat = _pack([given[n] for n in SMALL_SHARDED], F32)
    gbig, gsmall = all_gather([big_flat, small_flat], "gather_weights")
    big_shapes = [given[n].shape for n in BIG_NAMES]
    per_dev = _unpack_rows(gbig, big_shapes)
    W = {}
    for (name, kind), wg in zip(BIG, per_dev):
        layers = []
        for j in range(wg.shape[1]):
            blk = wg[:, j]
            if kind == 'row':
                layers.append(blk.reshape(1, -1, blk.shape[2]))
            elif blk.shape[2] % LANES:
                layers.append(blk.transpose(1, 0, 2).reshape(1, blk.shape[1], -1))
            else:
                layers.append(blk)
        W[name] = layers
    sm = _unpack_rows(gsmall, [given[n].shape for n in SMALL_SHARDED])
    q_norm_full = sm[0][:, 0].reshape(1, -1)
    kv_norm_full = sm[1][:, 0].reshape(1, -1)
    conv_w_full = sm[2].transpose(1, 2, 0, 3).reshape(depth, 3, -1)
    F = conv_w_full.shape[2] // 2

    def conv_params(i):
        cw = conv_w_full[i].reshape(3, 2, F).transpose(1, 0, 2)
        cb = ffn_conv_b[i].reshape(2, 1, F)
        return cw, cb

    ssm = lambda j: {"a_re": ssm_a_re[j], "a_im": ssm_a_im[j], "log_step": ssm_log_step[j], "b_re": ssm_b_re[j],
                     "b_im": ssm_b_im[j], "c_re": ssm_c_re[j], "c_im": ssm_c_im[j], "d": ssm_d[j][None],
                     "b_glu": ssm_b_glu[j][None]}

    cur = xs
    saved = []
    for i in range(depth):
        kind, j, tag = i % 3, i // 3, f"l{i}"
        if kind == 0:
            cur, sv = swa_layer_fwd(cur, mix_norm[i][None], W['attn_w_qkv'][j], W['attn_w_o'][j], attn_sink[j], tag)
        elif kind == 1:
            cur, sv = s5_layer_fwd(cur, mix_norm[i][None], ssm(j), W['ssm_w_glu'][j], tag)
        else:
            cur, sv = mla_layer_fwd(cur, mix_norm[i][None], W['mla_w_dqkv'][j], q_norm_full, kv_norm_full,
                                    W['mla_w_uq'][j], W['mla_w_ukv'][j], W['mla_w_o'][j], tag)
        cw, cb = conv_params(i)
        cur, fsv = ffn_fwd(cur, ffn_norm[i][None], W['ffn_w_up'][i], cw, cb, W['ffn_w_down'][i], tag)
        saved.append((sv, fsv))
    loss_part, dcur, dfinal = final_loss(cur, final_norm[None], loss_target[0], "loss_head")

    gb = {n: [None] * given[n].shape[0] for n in BIG_NAMES}
    gs = {n: [None] * given[n].shape[0] for n in SMALL if given[n].ndim > 1}
    gs['final_norm'] = dfinal[0]

    def blocked(g, kind, name):
        a, b = given[name].shape[1:]
        if kind == 'row':
            return g.reshape(N_DEV, a, b)
        if b % LANES:
            return g.reshape(a, N_DEV, b).transpose(1, 0, 2)
        return g

    for i in reversed(range(depth)):
        kind, j, tag = i % 3, i // 3, f"l{i}"
        sv, fsv = saved[i]
        cw, cb = conv_params(i)
        dcur, (dn, dwup, dcw, dcb, dwdn) = ffn_bwd(dcur, fsv, ffn_norm[i][None], W['ffn_w_up'][i], cw, cb,
                                                   W['ffn_w_down'][i], tag)
        gs['ffn_norm'][i] = dn[0]
        gb['ffn_w_up'][i] = blocked(dwup, 'col', 'ffn_w_up')
        gb['ffn_w_down'][i] = blocked(dwdn, 'row', 'ffn_w_down')
        gs['ffn_conv_w'][i] = dcw.transpose(1, 0, 2).reshape(3, 2 * F)
        gs['ffn_conv_b'][i] = dcb.reshape(2 * F)
        if kind == 0:
            dcur, (dn, dwqkv, dwo, dsink) = swa_layer_bwd(dcur, sv, mix_norm[i][None], W['attn_w_qkv'][j],
                                                          W['attn_w_o'][j], tag)
            gb['attn_w_qkv'][j] = blocked(dwqkv, 'col', 'attn_w_qkv')
            gb['attn_w_o'][j] = blocked(dwo, 'row', 'attn_w_o')
            gs['attn_sink'][j] = dsink
        elif kind == 1:
            dcur, g5 = s5_layer_bwd(dcur, sv, mix_norm[i][None], ssm(j), W['ssm_w_glu'][j], tag)
            dn = g5["norm"]
            gb['ssm_w_glu'][j] = blocked(g5["w_glu"], 'row', 'ssm_w_glu')
            for nm in ("a_re", "a_im", "log_step", "b_re", "b_im", "c_re", "c_im"):
                gs['ssm_' + nm][j] = g5[nm]
            gs['ssm_d'][j] = g5["d"][0]
            gs['ssm_b_glu'][j] = g5["b_glu"][0]
        else:
            dcur, (dn, dwd, dqn, dkn, dwuq, dwukv, dwo) = mla_layer_bwd(
                dcur, sv, mix_norm[i][None], W['mla_w_dqkv'][j], q_norm_full, kv_norm_full, W['mla_w_uq'][j],
                W['mla_w_ukv'][j], W['mla_w_o'][j], tag)
            gb['mla_w_dqkv'][j] = blocked(dwd, 'row', 'mla_w_dqkv')
            gb['mla_w_uq'][j] = blocked(dwuq, 'col', 'mla_w_uq')
            gb['mla_w_ukv'][j] = blocked(dwukv, 'col', 'mla_w_ukv')
            gb['mla_w_o'][j] = blocked(dwo, 'row', 'mla_w_o')
            gs['mla_q_norm'][j] = dqn[0]
            gs['mla_kv_norm'][j] = dkn[0]
        gs['mix_norm'][i] = dn[0]
    grad_x = dcur[None]

    slotted = jnp.concatenate([jnp.stack(gb[n], axis=1).reshape(N_DEV, -1) for n in BIG_NAMES], axis=1)
    pad = (-slotted.shape[1]) % (16 * FLAT_W)
    if pad:
        slotted = jnp.concatenate([slotted, jnp.zeros((N_DEV, pad), BF16)], axis=1)
    slotted = slotted.reshape(N_DEV, -1, FLAT_W)
    small_full = [gs[n] if n == 'final_norm' else jnp.stack(gs[n], axis=0) for n in SMALL]
    whole = _pack([loss_part[0, :1]] + small_full, F32)
    parts_big, parts_small = exchange(slotted, whole, "exchange_grads")

    g_big, d_big, nm_big, nv_big = adamw_from_parts(
        parts_big, _pack([given[n] for n in BIG_NAMES], F32), _pack([given['m_' + n] for n in BIG_NAMES], F32),
        _pack([given['v_' + n] for n in BIG_NAMES], F32), "adamw_big")
    out = {}
    for key, flat in (("grad", g_big), ("delta", d_big), ("new_m", nm_big), ("new_v", nv_big)):
        for n, val in zip(BIG_NAMES, _unpack(flat, big_shapes)):
            out[key, n] = val
    summed = _unpack(sum_parts(parts_small, "sum_small"), [(1,)] + [g.shape for g in small_full])
    loss = summed[0][0]
    small_grads = []
    for n, g in zip(SMALL, summed[1:]):
        small_grads.append(_own_slice(g, idx) if n in SMALL_SHARDED else g)
    small_shapes = [given[n].shape for n in SMALL]
    d_s, nm_s, nv_s = adamw_flat(_pack(small_grads, F32), _pack([given[n] for n in SMALL], F32),
                                 _pack([given['m_' + n] for n in SMALL], F32),
                                 _pack([given['v_' + n] for n in SMALL], F32), "adamw_small")
    for n, g in zip(SMALL, small_grads):
        out["grad", n] = g
    for key, flat in (("delta", d_s), ("new_m", nm_s), ("new_v", nv_s)):
        for n, val in zip(SMALL, _unpack(flat, small_shapes)):
            out[key, n] = val
    return (loss, grad_x, *[out["grad", n] for n in WEIGHTS], *[out["delta", n] for n in WEIGHTS],
            *[out["new_m", n] for n in WEIGHTS], *[out["new_v", n] for n in WEIGHTS])


def _unpack_rows(gathered, shapes):
    flat = gathered.reshape(N_DEV, -1)
    out, off = [], 0
    for s in shapes:
        n = int(np.prod(s))
        out.append(flat[:, off:off + n].reshape((N_DEV,) + tuple(s)))
        off += n
    return out
```

```python
import functools
import math

import numpy as np
import jax
import jax.numpy as jnp
from jax import lax
from jax.experimental import pallas as pl
from jax.experimental.pallas import tpu as pltpu

F32 = jnp.float32
BF16 = jnp.bfloat16

RMS_EPS = 1e-6
ATTN_HEAD_DIM = 64
ATTN_GROUP = 8
ATTN_BLOCK = 128
SSM_GROUP_CH = 16
SSM_STATE = 64
MLA_NOPE = 128
MLA_ROPE = 64
MLA_V = 128
ROPE_THETA = 10000.0
ADAM_LR = 0.001
ADAM_B1 = 0.9
ADAM_B2 = 0.999
ADAM_EPS = 1e-08
ADAM_WD = 0.01
ADAM_STEP = 10

N_DEV = 8
LANES = 128
SUBLANES = 8
VMEM_LIMIT_BYTES = 50 * 2 ** 20
MASK_VALUE = -1e30
LOG2E = math.log2(math.e)
SEGS = SUBLANES


def _params(*sem):
    return pltpu.CompilerParams(dimension_semantics=sem, vmem_limit_bytes=VMEM_LIMIT_BYTES)


def _divisor(n, cap, align):
    best = None
    d = align
    while d <= min(n, cap):
        if n % d == 0:
            best = d
        d += align
    return best if best is not None else n


def _layered(w):
    return w if isinstance(w, tuple) else (w[None], 0)


def _nblocks(w):
    return _layered(w)[0].shape[1]


def mm_nn(a, w, out_dtype, name, tm=1024, tn_cap=1408, parts=1):
    M, K = a.shape
    w, layer = _layered(w)
    _, nb, K2, n = w.shape
    assert K == K2
    tm = _divisor(M, tm, 16)
    tn = n if nb > 1 else _divisor(n // parts, tn_cap, LANES)
    jn = n // tn
    J = nb * jn
    assert J % parts == 0
    jp = J // parts

    def body(a_ref, w_ref, o_ref):
        o_ref[...] = jnp.dot(a_ref[...].astype(BF16), w_ref[...], preferred_element_type=F32).astype(o_ref.dtype)

    if parts == 1:
        out_shape = jax.ShapeDtypeStruct((M, nb * n), out_dtype)
        out_spec = pl.BlockSpec((tm, tn), lambda i, j: (i, j))
    else:
        out_shape = jax.ShapeDtypeStruct((parts, M, nb * n // parts), out_dtype)
        out_spec = pl.BlockSpec((None, tm, tn), lambda i, j: (j // jp, i, j % jp))
    return pl.pallas_call(
        body, name=name, out_shape=out_shape, grid=(M // tm, J),
        in_specs=[pl.BlockSpec((tm, K), lambda i, j: (i, 0)),
                  pl.BlockSpec((None, None, K, tn), lambda i, j: (layer, j // jn, 0, j % jn))],
        out_specs=out_spec, compiler_params=_params("parallel", "arbitrary"),
    )(a, w)


def mm_nt(a, w, out_dtype, name, tm=1024, tk=1024, tc_cap=2816):
    w, layer = _layered(w)
    _, nb, K, n = w.shape
    split = a.ndim == 3
    M = a.shape[-2]
    tm = _divisor(M, tm, 16)
    tk = _divisor(K, tk, LANES)
    P = a.shape[0] if split else 1
    tc = n if nb > 1 else _divisor(n // P, tc_cap, LANES)
    jn = n // tc
    J = nb * jn
    if split:
        P = a.shape[0]
        assert J % P == 0 and a.shape[2] * P == nb * n
        jp = J // P
        a_spec = pl.BlockSpec((None, tm, tc), lambda i, k, j: (j // jp, i, j % jp))
    else:
        assert a.shape[1] == nb * n
        a_spec = pl.BlockSpec((tm, tc), lambda i, k, j: (i, j))

    def body(a_ref, w_ref, o_ref, acc_ref):
        j = pl.program_id(2)
        part = lax.dot_general(a_ref[...].astype(BF16), w_ref[...], (((1,), (1,)), ((), ())),
                               preferred_element_type=F32)

        @pl.when(j == 0)
        def _():
            acc_ref[...] = part

        @pl.when(j > 0)
        def _():
            acc_ref[...] += part

        @pl.when(j == J - 1)
        def _():
            o_ref[...] = acc_ref[...].astype(o_ref.dtype)

    return pl.pallas_call(
        body, name=name, out_shape=jax.ShapeDtypeStruct((M, K), out_dtype), grid=(M // tm, K // tk, J),
        in_specs=[a_spec, pl.BlockSpec((None, None, tk, tc), lambda i, k, j: (layer, j // jn, k, j % jn))],
        out_specs=pl.BlockSpec((tm, tk), lambda i, k, j: (i, k)),
        scratch_shapes=[pltpu.VMEM((tm, tk), F32)],
        compiler_params=_params("parallel", "parallel", "arbitrary"),
    )(a, w)


def mm_tn(a, b, nb, out_dtype, name, tm=512, tka=2048, tn_cap=1408):
    M, Ka = a.shape
    split = b.ndim == 3
    N = b.shape[-1] * (b.shape[0] if split else 1)
    n = N // nb
    tm = _divisor(M, tm, 16)
    tka = _divisor(Ka, tka, LANES)
    tn = n if nb > 1 else _divisor(n // (b.shape[0] if split else 1), tn_cap, LANES)
    jn = n // tn
    J = nb * jn
    steps = M // tm
    if split:
        P = b.shape[0]
        assert J % P == 0
        jp = J // P
        b_spec = pl.BlockSpec((None, tm, tn), lambda k, j, i: (j // jp, i, j % jp))
    else:
        b_spec = pl.BlockSpec((tm, tn), lambda k, j, i: (i, j))

    def body(a_ref, b_ref, o_ref, acc_ref):
        i = pl.program_id(2)
        part = lax.dot_general(a_ref[...].astype(BF16), b_ref[...].astype(BF16), (((0,), (0,)), ((), ())),
                               preferred_element_type=F32)

        @pl.when(i == 0)
        def _():
            acc_ref[...] = part

        @pl.when(i > 0)
        def _():
            acc_ref[...] += part

        @pl.when(i == steps - 1)
        def _():
            o_ref[...] = acc_ref[...].astype(o_ref.dtype)

    return pl.pallas_call(
        body, name=name, out_shape=jax.ShapeDtypeStruct((nb, Ka, n), out_dtype), grid=(Ka // tka, J, steps),
        in_specs=[pl.BlockSpec((tm, tka), lambda k, j, i: (i, k)), b_spec],
        out_specs=pl.BlockSpec((None, tka, tn), lambda k, j, i: (j // jn, k, j % jn)),
        scratch_shapes=[pltpu.VMEM((tka, tn), F32)],
        compiler_params=_params("parallel", "parallel", "arbitrary"),
    )(a, b)


def rms_fwd(x, g, out_dtype, name, ts=512):
    S, D = x.shape
    ts = _divisor(S, ts, 16)

    def body(x_ref, g_ref, h_ref, r_ref):
        x = x_ref[...]
        r = lax.rsqrt(jnp.mean(x * x, axis=-1, keepdims=True) + RMS_EPS)
        h_ref[...] = ((x * r) * g_ref[...]).astype(h_ref.dtype)
        r_ref[...] = r

    return pl.pallas_call(
        body, name=name,
        out_shape=(jax.ShapeDtypeStruct((S, D), out_dtype), jax.ShapeDtypeStruct((S, 1), F32)),
        grid=(S // ts,),
        in_specs=[pl.BlockSpec((ts, D), lambda i: (i, 0)), pl.BlockSpec((1, D), lambda i: (0, 0))],
        out_specs=(pl.BlockSpec((ts, D), lambda i: (i, 0)), pl.BlockSpec((ts, 1), lambda i: (i, 0))),
        compiler_params=_params("parallel"),
    )(x, g)


def rms_bwd(dh, x, rstd, g, name, dres=None, ts=512):
    S, D = x.shape
    ts = _divisor(S, ts, 16)
    has_res = dres is not None

    def body(*refs):
        if has_res:
            dh_ref, x_ref, r_ref, g_ref, res_ref, dx_ref, dg_ref = refs
        else:
            dh_ref, x_ref, r_ref, g_ref, dx_ref, dg_ref = refs
        i = pl.program_id(0)
        dh = dh_ref[...].astype(F32)
        r = r_ref[...]
        xhat = x_ref[...] * r
        dhg = dh * g_ref[...]
        c = jnp.mean(dhg * xhat, axis=-1, keepdims=True)
        dx = r * (dhg - xhat * c)
        if has_res:
            dx = dx + res_ref[...]
        dx_ref[...] = dx
        part = jnp.sum(dh * xhat, axis=0, keepdims=True)

        @pl.when(i == 0)
        def _():
            dg_ref[...] = part

        @pl.when(i > 0)
        def _():
            dg_ref[...] += part

    row = pl.BlockSpec((ts, D), lambda i: (i, 0))
    args = [dh, x, rstd, g] + ([dres] if has_res else [])
    specs = [row, row, pl.BlockSpec((ts, 1), lambda i: (i, 0)), pl.BlockSpec((1, D), lambda i: (0, 0))]
    specs += [row] if has_res else []
    return pl.pallas_call(
        body, name=name,
        out_shape=(jax.ShapeDtypeStruct((S, D), F32), jax.ShapeDtypeStruct((1, D), F32)),
        grid=(S // ts,), in_specs=specs,
        out_specs=(row, pl.BlockSpec((1, D), lambda i: (0, 0))),
        compiler_params=_params("arbitrary"),
    )(*args)


def ew(fn, name, mats, rows, out_dtypes, n_sums=0, ts=512, tc=1024):
    S, C = mats[0].shape
    ts = _divisor(S, ts, 16)
    tc = _divisor(C, tc, LANES)
    n_in = len(mats) + len(rows)
    n_out = len(out_dtypes)

    def body(*refs):
        i = pl.program_id(1)
        res = fn(*[r[...] for r in refs[:n_in]])
        if not isinstance(res, (tuple, list)):
            res = (res,)
        for o_ref, val in zip(refs[n_in:n_in + n_out], res[:n_out]):
            o_ref[...] = val.astype(o_ref.dtype)
        for s_ref, val in zip(refs[n_in + n_out:], res[n_out:]):
            part = jnp.sum(val, axis=0, keepdims=True)

            @pl.when(i == 0)
            def _():
                s_ref[...] = part

            @pl.when(i > 0)
            def _():
                s_ref[...] += part

    mat = pl.BlockSpec((ts, tc), lambda j, i: (i, j))
    row = pl.BlockSpec((1, tc), lambda j, i: (0, j))
    out_shape = tuple(jax.ShapeDtypeStruct((S, C), d) for d in out_dtypes)
    out_shape += tuple(jax.ShapeDtypeStruct((1, C), F32) for _ in range(n_sums))
    return pl.pallas_call(
        body, name=name, out_shape=out_shape, grid=(C // tc, S // ts),
        in_specs=[mat] * len(mats) + [row] * len(rows),
        out_specs=tuple([mat] * n_out + [row] * n_sums),
        compiler_params=_params("parallel", "arbitrary"),
    )(*mats, *rows)


def _halo_specs(ts, tc, S, lead):
    nblk = S // SUBLANES
    per = ts // SUBLANES
    pre = (None,) * 0
    if lead:
        main = pl.BlockSpec((lead, ts, tc), lambda j, i: (0, i, j))
        prev = pl.BlockSpec((lead, SUBLANES, tc), lambda j, i: (0, jnp.maximum(i * per - 1, 0), j))
        nxt = pl.BlockSpec((lead, SUBLANES, tc), lambda j, i: (0, jnp.minimum((i + 1) * per, nblk - 1), j))
    else:
        main = pl.BlockSpec((ts, tc), lambda j, i: (i, j))
        prev = pl.BlockSpec((SUBLANES, tc), lambda j, i: (jnp.maximum(i * per - 1, 0), j))
        nxt = pl.BlockSpec((SUBLANES, tc), lambda j, i: (jnp.minimum((i + 1) * per, nblk - 1), j))
    return main, prev, nxt


def _extended(prev, main, nxt, i, ts, S):
    ext = jnp.concatenate([prev, main, nxt], axis=0).astype(F32)
    rows = lax.broadcasted_iota(jnp.int32, (ts + 2 * SUBLANES, 1), 0) + (i * ts - SUBLANES)
    return jnp.where((rows >= 0) & (rows < S), ext, 0.0)


def _conv3(ue, w, b):
    n = ue.shape[0]
    return b + (w[0:1] * pltpu.roll(ue, 1, axis=0) + w[1:2] * ue + w[2:3] * pltpu.roll(ue, n - 1, axis=0))


def convgate_fwd(u3, conv_w, conv_b, name, ts=512, tc=512):
    _, S, F = u3.shape
    ts = _divisor(S, ts, 16)
    tc = _divisor(F, tc, LANES)
    main, prev, nxt = _halo_specs(ts, tc, S, 2)

    def body(m_ref, p_ref, n_ref, w_ref, b_ref, o_ref):
        i = pl.program_id(1)
        c = []
        for h in range(2):
            ue = _extended(p_ref[h], m_ref[h], n_ref[h], i, ts, S)
            c.append(_conv3(ue, w_ref[h], b_ref[h])[SUBLANES:SUBLANES + ts])
        o_ref[...] = (jax.nn.silu(c[0]) * c[1]).astype(o_ref.dtype)

    return pl.pallas_call(
        body, name=name, out_shape=jax.ShapeDtypeStruct((S, F), BF16), grid=(F // tc, S // ts),
        in_specs=[main, prev, nxt, pl.BlockSpec((2, 3, tc), lambda j, i: (0, 0, j)),
                  pl.BlockSpec((2, 1, tc), lambda j, i: (0, 0, j))],
        out_specs=pl.BlockSpec((ts, tc), lambda j, i: (i, j)),
        compiler_params=_params("parallel", "arbitrary"),
    )(u3, u3, u3, conv_w, conv_b)


def convgate_bwd(u3, da, conv_w, conv_b, name, ts=512, tc=512):
    _, S, F = u3.shape
    ts = _divisor(S, ts, 16)
    tc = _divisor(F, tc, LANES)
    main, prev, nxt = _halo_specs(ts, tc, S, 2)
    amain, aprev, anxt = _halo_specs(ts, tc, S, 0)
    n = ts + 2 * SUBLANES
    mid = slice(SUBLANES, SUBLANES + ts)

    def body(m_ref, p_ref, n_ref, am_ref, ap_ref, an_ref, w_ref, b_ref, du_ref, dw_ref, db_ref):
        i = pl.program_id(1)
        ue = [_extended(p_ref[h], m_ref[h], n_ref[h], i, ts, S) for h in range(2)]
        g = _conv3(ue[0], w_ref[0], b_ref[0])
        v = _conv3(ue[1], w_ref[1], b_ref[1])
        dae = _extended(ap_ref[...], am_ref[...], an_ref[...], i, ts, S)
        sg = jax.nn.sigmoid(g)
        dc = [dae * v * (sg * (1.0 + g * (1.0 - sg))), dae * (g * sg)]
        for h in range(2):
            w = w_ref[h]
            du = w[0:1] * pltpu.roll(dc[h], n - 1, axis=0) + w[1:2] * dc[h] + w[2:3] * pltpu.roll(dc[h], 1, axis=0)
            du_ref[h] = du[mid].astype(du_ref.dtype)
            dcm = dc[h][mid]
            taps = [pltpu.roll(ue[h], 1, axis=0)[mid], ue[h][mid], pltpu.roll(ue[h], n - 1, axis=0)[mid]]
            sums = [jnp.sum(dcm * t, axis=0, keepdims=True) for t in taps]
            db = jnp.sum(dcm, axis=0, keepdims=True)

            @pl.when(i == 0)
            def _():
                for t in range(3):
                    dw_ref[h, t:t + 1, :] = sums[t]
                db_ref[h] = db

            @pl.when(i > 0)
            def _():
                for t in range(3):
                    dw_ref[h, t:t + 1, :] += sums[t]
                db_ref[h] += db

    return pl.pallas_call(
        body, name=name,
        out_shape=(jax.ShapeDtypeStruct((2, S, F), BF16), jax.ShapeDtypeStruct((2, 3, F), F32),
                   jax.ShapeDtypeStruct((2, 1, F), F32)),
        grid=(F // tc, S // ts),
        in_specs=[main, prev, nxt, amain, aprev, anxt, pl.BlockSpec((2, 3, tc), lambda j, i: (0, 0, j)),
                  pl.BlockSpec((2, 1, tc), lambda j, i: (0, 0, j))],
        out_specs=(pl.BlockSpec((2, ts, tc), lambda j, i: (0, i, j)),
                   pl.BlockSpec((2, 3, tc), lambda j, i: (0, 0, j)),
                   pl.BlockSpec((2, 1, tc), lambda j, i: (0, 0, j))),
        compiler_params=_params("parallel", "arbitrary"),
    )(u3, u3, u3, da, da, da, conv_w, conv_b)


def ffn_fwd(x, norm_g, w_up, conv_w, conv_b, w_down, tag):
    h, rstd = rms_fwd(x, norm_g, BF16, f"ffn_norm_{tag}")
    u3 = mm_nn(h, w_up, F32, f"ffn_up_{tag}", parts=2)
    a = convgate_fwd(u3, conv_w, conv_b, f"ffn_gate_{tag}")
    y = mm_nn_res(a, w_down, x, f"ffn_down_{tag}")
    return y, (x, h, rstd, u3, a)


def mm_nn_res(a, w, res, name, tm=1024, tn=512):
    M, K = a.shape
    w, layer = _layered(w)
    N = w.shape[3]
    assert w.shape[1] == 1
    tm = _divisor(M, tm, 16)
    tn = _divisor(N, tn, LANES)

    def body(a_ref, w_ref, r_ref, o_ref):
        o_ref[...] = r_ref[...] + jnp.dot(a_ref[...].astype(BF16), w_ref[...], preferred_element_type=F32)

    return pl.pallas_call(
        body, name=name, out_shape=jax.ShapeDtypeStruct((M, N), F32), grid=(M // tm, N // tn),
        in_specs=[pl.BlockSpec((tm, K), lambda i, j: (i, 0)),
                  pl.BlockSpec((None, None, K, tn), lambda i, j: (layer, 0, 0, j)),
                  pl.BlockSpec((tm, tn), lambda i, j: (i, j))],
        out_specs=pl.BlockSpec((tm, tn), lambda i, j: (i, j)),
        compiler_params=_params("parallel", "arbitrary"),
    )(a, w, res)


def ffn_bwd(dy, saved, norm_g, w_up, conv_w, conv_b, w_down, tag):
    x, h, rstd, u3, a = saved
    nb = _nblocks(w_up)
    dw_down = mm_tn(a, dy, 1, BF16, f"ffn_dwdown_{tag}")
    da = mm_nt(dy, w_down, F32, f"ffn_da_{tag}", tk=512)
    du3, dconv_w, dconv_b = convgate_bwd(u3, da, conv_w, conv_b, f"ffn_dgate_{tag}")
    dh = mm_nt(du3, w_up, F32, f"ffn_dh_{tag}")
    dw_up = mm_tn(h, du3, nb, BF16, f"ffn_dwup_{tag}")
    dx, dnorm = rms_bwd(dh, x, rstd, norm_g, f"ffn_dnorm_{tag}", dres=dy)
    return dx, (dnorm, dw_up, dconv_w, dconv_b, dw_down)


def _swa_bias(n_heads):
    kv = n_heads // ATTN_GROUP
    slopes = jnp.asarray((2.0 ** (-8.0 * np.arange(1, n_heads + 1) / n_heads)).astype(np.float32))
    rel = jnp.arange(SWA_KEYS)[None, :] - ATTN_BLOCK - jnp.arange(ATTN_BLOCK)[:, None]
    dist = jnp.abs(rel).astype(F32)
    bias = (-slopes.reshape(kv, ATTN_GROUP, 1, 1) * dist) * LOG2E
    col = jnp.arange(SWA_KEYS)
    out = []
    for dead in (col < ATTN_BLOCK, col < 0, col >= 2 * ATTN_BLOCK):
        keep = (jnp.abs(rel) <= ATTN_BLOCK) & ~dead[None, :]
        out.append(jnp.where(keep[None, None], bias, MASK_VALUE).reshape(kv, SWA_ROWS, SWA_KEYS))
    return jnp.stack(out)


SWA_ROWS = ATTN_GROUP * ATTN_BLOCK
SWA_KEYS = 3 * ATTN_BLOCK
SWA_LOGIT_SCALE = ATTN_HEAD_DIM ** -0.5 * LOG2E


def _swa_probs(q, kc, bias, sink):
    reps = (1, SWA_KEYS // LANES)
    s = lax.dot_general(q, kc, (((1,), (1,)), ((), ())), preferred_element_type=F32) * SWA_LOGIT_SCALE + bias
    m = jnp.maximum(jnp.max(s, axis=1, keepdims=True), sink)
    p = jnp.exp2(s - jnp.tile(m, reps))
    es = jnp.exp2(sink - m)
    r = 1.0 / (jnp.sum(p, axis=1, keepdims=True) + es)
    return p * jnp.tile(r, reps), es * r


def _swa_specs(S):
    nb = S // ATTN_BLOCK

    def at(off):
        return pl.BlockSpec((None, ATTN_BLOCK, ATTN_HEAD_DIM),
                            lambda c, n: (c, jnp.clip(jnp.minimum(n, nb - 1) + off, 0, nb - 1), 0))

    qspec = pl.BlockSpec((None, ATTN_GROUP, ATTN_BLOCK, ATTN_HEAD_DIM), lambda c, n: (c, 0, jnp.minimum(n, nb - 1), 0))
    bias = pl.BlockSpec((None, None, SWA_ROWS, SWA_KEYS),
                        lambda c, n: (jnp.where(n == 0, 0, jnp.where(n >= nb - 1, 2, 1)), c, 0, 0))
    sink = pl.BlockSpec((None, SWA_ROWS, LANES), lambda c, n: (c, 0, 0))
    return qspec, [at(-1), at(0), at(1)], bias, sink


def swa_fwd(q, k, v, bias, sink, name):
    KV, G, S, dh = q.shape
    nb = S // ATTN_BLOCK
    qspec, kspecs, bspec, sspec = _swa_specs(S)

    def body(q_ref, k0, k1, k2, v0, v1, v2, b_ref, s_ref, o_ref):
        kc = jnp.concatenate([k0[...], k1[...], k2[...]], axis=0)
        vc = jnp.concatenate([v0[...], v1[...], v2[...]], axis=0)
        p, _ = _swa_probs(q_ref[...].reshape(SWA_ROWS, dh), kc, b_ref[...], s_ref[...])
        o = jnp.dot(p.astype(BF16), vc, preferred_element_type=F32)
        o_ref[...] = o.reshape(G, ATTN_BLOCK, dh).astype(o_ref.dtype)

    return pl.pallas_call(
        body, name=name, out_shape=jax.ShapeDtypeStruct(q.shape, BF16), grid=(KV, nb),
        in_specs=[qspec] + kspecs + kspecs + [bspec, sspec], out_specs=qspec,
        compiler_params=_params("parallel", "arbitrary"),
    )(q, k, k, k, v, v, v, bias, sink)


def swa_bwd(q, k, v, bias, sink, do, name):
    KV, G, S, dh = q.shape
    nb = S // ATTN_BLOCK
    qspec, kspecs, bspec, sspec = _swa_specs(S)
    scale = ATTN_HEAD_DIM ** -0.5
    kv_out = pl.BlockSpec((None, ATTN_BLOCK, dh), lambda c, n: (c, jnp.maximum(n - 1, 0), 0))

    def body(q_ref, k0, k1, k2, v0, v1, v2, b_ref, s_ref, do_ref, dq_ref, dk_ref, dv_ref, ds_ref, dk_acc, dv_acc):
        n = pl.program_id(1)

        @pl.when(n == 0)
        def _():
            dk_acc[...] = jnp.zeros_like(dk_acc)
            dv_acc[...] = jnp.zeros_like(dv_acc)
            ds_ref[...] = jnp.zeros_like(ds_ref)

        @pl.when(n > 0)
        def _():
            dk_acc[(n + 1) % 3] = jnp.zeros((ATTN_BLOCK, dh), F32)
            dv_acc[(n + 1) % 3] = jnp.zeros((ATTN_BLOCK, dh), F32)

        @pl.when(n < nb)
        def _():
            kc = jnp.concatenate([k0[...], k1[...], k2[...]], axis=0)
            vc = jnp.concatenate([v0[...], v1[...], v2[...]], axis=0)
            qa = q_ref[...].reshape(SWA_ROWS, dh)
            da = do_ref[...].reshape(SWA_ROWS, dh)
            p, psink = _swa_probs(qa, kc, b_ref[...], s_ref[...])
            dp = lax.dot_general(da, vc, (((1,), (1,)), ((), ())), preferred_element_type=F32)
            delta = jnp.sum(p * dp, axis=1, keepdims=True) + jnp.zeros((SWA_ROWS, LANES), F32)
            ds = ((p * (dp - jnp.tile(delta, (1, SWA_KEYS // LANES)))) * scale).astype(BF16)
            dq = jnp.dot(ds, kc, preferred_element_type=F32)
            dq_ref[...] = dq.reshape(G, ATTN_BLOCK, dh).astype(dq_ref.dtype)
            dkc = lax.dot_general(ds, qa, (((0,), (0,)), ((), ())), preferred_element_type=F32)
            dvc = lax.dot_general(p.astype(BF16), da, (((0,), (0,)), ((), ())), preferred_element_type=F32)
            ds_ref[...] -= psink * delta
            for o in range(3):
                slot = (n + 2 + o) % 3
                dk_acc[slot] += dkc[o * ATTN_BLOCK:(o + 1) * ATTN_BLOCK]
                dv_acc[slot] += dvc[o * ATTN_BLOCK:(o + 1) * ATTN_BLOCK]

        done = (n + 2) % 3
        dk_ref[...] = dk_acc[done].astype(dk_ref.dtype)
        dv_ref[...] = dv_acc[done].astype(dv_ref.dtype)

    return pl.pallas_call(
        body, name=name,
        out_shape=(jax.ShapeDtypeStruct(q.shape, BF16), jax.ShapeDtypeStruct(k.shape, BF16),
                   jax.ShapeDtypeStruct(v.shape, BF16), jax.ShapeDtypeStruct((KV, SWA_ROWS, LANES), F32)),
        grid=(KV, nb + 1),
        in_specs=[qspec] + kspecs + kspecs + [bspec, sspec, qspec],
        out_specs=(qspec, kv_out, kv_out, sspec),
        scratch_shapes=[pltpu.VMEM((3, ATTN_BLOCK, dh), F32), pltpu.VMEM((3, ATTN_BLOCK, dh), F32)],
        compiler_params=_params("parallel", "arbitrary"),
    )(q, k, k, k, v, v, v, bias, sink, do)


def _heads_major(x, n_heads):
    S = x.shape[0]
    return x.reshape(S, n_heads, -1).transpose(1, 0, 2)


def _heads_minor(x):
    H, S, dh = x.shape
    return x.transpose(1, 0, 2).reshape(S, H * dh)


def swa_layer_fwd(x, norm_g, w_qkv, w_o, sink, tag):
    S, D = x.shape
    H = _layered(w_o)[0].shape[2] // ATTN_HEAD_DIM
    KV = H // ATTN_GROUP
    h, rstd = rms_fwd(x, norm_g, BF16, f"swa_norm_{tag}")
    qkv = mm_nn(h, w_qkv, BF16, f"swa_qkv_{tag}", tn_cap=1280)
    q = _heads_major(qkv[:, :H * ATTN_HEAD_DIM], H).reshape(KV, ATTN_GROUP, S, ATTN_HEAD_DIM)
    k = _heads_major(qkv[:, H * ATTN_HEAD_DIM:(H + KV) * ATTN_HEAD_DIM], KV)
    v = _heads_major(qkv[:, (H + KV) * ATTN_HEAD_DIM:], KV)
    bias = _swa_bias(H)
    sinkb = jnp.broadcast_to((sink.astype(F32) * LOG2E).reshape(KV, ATTN_GROUP, 1, 1),
                             (KV, ATTN_GROUP, ATTN_BLOCK, LANES)).reshape(KV, SWA_ROWS, LANES)
    o = swa_fwd(q, k, v, bias, sinkb, f"swa_attn_{tag}")
    o2 = _heads_minor(o.reshape(H, S, ATTN_HEAD_DIM))
    y = mm_nn_res(o2, w_o, x, f"swa_out_{tag}")
    return y, (x, h, rstd, q, k, v, bias, sinkb, o2)


def swa_layer_bwd(dy, saved, norm_g, w_qkv, w_o, tag):
    x, h, rstd, q, k, v, bias, sinkb, o2 = saved
    KV, G, S, dh = q.shape
    H = KV * G
    dw_o = mm_tn(o2, dy, 1, BF16, f"swa_dwo_{tag}")
    do2 = mm_nt(dy, w_o, BF16, f"swa_do_{tag}")
    do = _heads_major(do2, H).reshape(KV, G, S, dh)
    dq, dk, dv, dsink = swa_bwd(q, k, v, bias, sinkb, do, f"swa_dattn_{tag}")
    dqkv = jnp.concatenate([_heads_minor(dq.reshape(H, S, dh)), _heads_minor(dk), _heads_minor(dv)], axis=1)
    dw_qkv = mm_tn(h, dqkv, 1, BF16, f"swa_dwqkv_{tag}", tn_cap=1280)
    dh_ = mm_nt(dqkv, w_qkv, F32, f"swa_dh_{tag}", tc_cap=2560)
    dx, dnorm = rms_bwd(dh_, x, rstd, norm_g, f"swa_dnorm_{tag}", dres=dy)
    return dx, (dnorm, dw_qkv, dw_o, dsink[:, :, 0].reshape(H, ATTN_BLOCK).sum(axis=1))


def flash_fwd(q, k, v, scale, name, tq=1024, tk=512):
    H, S, dk = q.shape
    dv = v.shape[-1]
    assert dv == LANES
    tq = _divisor(S, tq, 16)
    tk = _divisor(S, tk, LANES)
    nk = S // tk
    c = scale * LOG2E

    def body(q_ref, k_ref, v_ref, o_ref, lse_ref, m_sc, l_sc, acc_sc):
        m_sc[...] = jnp.full(m_sc.shape, MASK_VALUE, F32)
        l_sc[...] = jnp.zeros(l_sc.shape, F32)
        acc_sc[...] = jnp.zeros(acc_sc.shape, F32)

        def step(kt, carry):
            off = pl.multiple_of(kt * tk, tk)
            s = lax.dot_general(q_ref[...], k_ref[pl.ds(off, tk), :], (((1,), (1,)), ((), ())),
                                preferred_element_type=F32) * c
            m_prev = m_sc[...]
            m_new = jnp.maximum(m_prev, jnp.max(s, axis=1, keepdims=True))
            alpha = jnp.exp2(m_prev - m_new)
            p = jnp.exp2(s - jnp.tile(m_new, (1, tk // LANES)))
            l_sc[...] = alpha * l_sc[...] + jnp.sum(p, axis=1, keepdims=True)
            acc_sc[...] = alpha * acc_sc[...] + jnp.dot(p.astype(BF16), v_ref[pl.ds(off, tk), :],
                                                        preferred_element_type=F32)
            m_sc[...] = m_new
            return carry

        lax.fori_loop(0, nk, step, 0)
        l = l_sc[...]
        o_ref[...] = (acc_sc[...] / l).astype(o_ref.dtype)
        lse_ref[...] = (m_sc[...] + jnp.log2(l))[:, :1]

    return pl.pallas_call(
        body, name=name,
        out_shape=(jax.ShapeDtypeStruct((H, S, dv), BF16), jax.ShapeDtypeStruct((H, S, 1), F32)),
        grid=(H, S // tq),
        in_specs=[pl.BlockSpec((None, tq, dk), lambda h, i: (h, i, 0)),
                  pl.BlockSpec((None, S, dk), lambda h, i: (h, 0, 0)),
                  pl.BlockSpec((None, S, dv), lambda h, i: (h, 0, 0))],
        out_specs=(pl.BlockSpec((None, tq, dv), lambda h, i: (h, i, 0)),
                   pl.BlockSpec((None, tq, 1), lambda h, i: (h, i, 0))),
        scratch_shapes=[pltpu.VMEM((tq, LANES), F32), pltpu.VMEM((tq, LANES), F32), pltpu.VMEM((tq, dv), F32)],
        compiler_params=_params("parallel", "arbitrary"),
    )(q, k, v)


def flash_delta(o, do, name, ts=1024):
    H, S, dv = o.shape
    ts = _divisor(S, ts, 16)

    def body(o_ref, do_ref, d_ref):
        d_ref[...] = jnp.sum(o_ref[...].astype(F32) * do_ref[...].astype(F32), axis=-1, keepdims=True)

    spec = pl.BlockSpec((None, ts, dv), lambda h, i: (h, i, 0))
    return pl.pallas_call(
        body, name=name, out_shape=jax.ShapeDtypeStruct((H, S, 1), F32), grid=(H, S // ts),
        in_specs=[spec, spec], out_specs=pl.BlockSpec((None, ts, 1), lambda h, i: (h, i, 0)),
        compiler_params=_params("parallel", "parallel"),
    )(o, do)


def flash_bwd(q, k, v, do, lse_row, delta_row, scale, name, tq=1024, tk=512):
    H, S, dk = q.shape
    dv = v.shape[-1]
    tq = _divisor(S, tq, LANES)
    tk = _divisor(S, tk, 16)
    nq = S // tq
    c = scale * LOG2E

    def body(q_ref, k_ref, v_ref, do_ref, lse_ref, dl_ref, dq_ref, dk_ref, dv_ref, dk_sc, dv_sc):
        j = pl.program_id(1)

        @pl.when(j == 0)
        def _():
            dq_ref[...] = jnp.zeros(dq_ref.shape, F32)

        dk_sc[...] = jnp.zeros(dk_sc.shape, F32)
        dv_sc[...] = jnp.zeros(dv_sc.shape, F32)
        kk = k_ref[...]
        vv = v_ref[...]

        def step(t, carry):
            off = pl.multiple_of(t * tq, tq)
            qq = q_ref[pl.ds(off, tq), :]
            dd = do_ref[pl.ds(off, tq), :]
            st = lax.dot_general(kk, qq, (((1,), (1,)), ((), ())), preferred_element_type=F32) * c
            pt = jnp.exp2(st - lse_ref[:, pl.ds(off, tq)])
            dpt = lax.dot_general(vv, dd, (((1,), (1,)), ((), ())), preferred_element_type=F32)
            dst = ((pt * (dpt - dl_ref[:, pl.ds(off, tq)])) * scale).astype(BF16)
            dv_sc[...] += jnp.dot(pt.astype(BF16), dd, preferred_element_type=F32)
            dk_sc[...] += jnp.dot(dst, qq, preferred_element_type=F32)
            dq_ref[pl.ds(off, tq), :] += lax.dot_general(dst, kk, (((0,), (0,)), ((), ())),
                                                         preferred_element_type=F32)
            return carry

        lax.fori_loop(0, nq, step, 0)
        dk_ref[...] = dk_sc[...].astype(dk_ref.dtype)
        dv_ref[...] = dv_sc[...].astype(dv_ref.dtype)

    ks = pl.BlockSpec((None, tk, dk), lambda h, j: (h, j, 0))
    vs = pl.BlockSpec((None, tk, dv), lambda h, j: (h, j, 0))
    row = pl.BlockSpec((None, 1, S), lambda h, j: (h, 0, 0))
    qs = pl.BlockSpec((None, S, dk), lambda h, j: (h, 0, 0))
    return pl.pallas_call(
        body, name=name,
        out_shape=(jax.ShapeDtypeStruct((H, S, dk), F32), jax.ShapeDtypeStruct((H, S, dk), F32),
                   jax.ShapeDtypeStruct((H, S, dv), BF16)),
        grid=(H, S // tk),
        in_specs=[qs, ks, vs, pl.BlockSpec((None, S, dv), lambda h, j: (h, 0, 0)), row, row],
        out_specs=(qs, ks, vs), scratch_shapes=[pltpu.VMEM((tk, dk), F32), pltpu.VMEM((tk, dv), F32)],
        compiler_params=_params("parallel", "arbitrary"),
    )(q, k, v, do, lse_row, delta_row)


def _rope_tables(S, reps):
    half = MLA_ROPE // 2
    pos = jnp.arange(S, dtype=F32)
    inv = ROPE_THETA ** (-jnp.arange(half, dtype=F32) / half)
    ang = pos[:, None] * inv[None, :]
    return jnp.tile(jnp.cos(ang), (1, reps)), jnp.tile(jnp.sin(ang), (1, reps))


def _rotate(x1, x2, cos, sin, out_dtype, name):
    return ew(lambda a, b, c, s: (a * c - b * s, b * c + a * s), name, [x1, x2, cos, sin], [], [out_dtype, out_dtype])


def mla_layer_fwd(x, norm_g, w_dqkv, q_norm, kv_norm, w_uq, w_ukv, w_o, tag):
    S, D = x.shape
    QL, KL = q_norm.shape[1], kv_norm.shape[1]
    H = _layered(w_o)[0].shape[2] // MLA_V
    half = MLA_ROPE // 2
    h, rstd = rms_fwd(x, norm_g, BF16, f"mla_norm_{tag}")
    d = mm_nn(h, w_dqkv, F32, f"mla_down_{tag}")
    c_q, c_kv, k_rope = d[:, :QL], d[:, QL:QL + KL], d[:, QL + KL:]
    cq_n, rstd_q = rms_fwd(c_q, q_norm, BF16, f"mla_qnorm_{tag}")
    ckv_n, rstd_kv = rms_fwd(c_kv, kv_norm, BF16, f"mla_kvnorm_{tag}")
    q = mm_nn(cq_n, w_uq, F32, f"mla_uq_{tag}").reshape(S, H, MLA_NOPE + MLA_ROPE)
    kv = mm_nn(ckv_n, w_ukv, BF16, f"mla_ukv_{tag}").reshape(S, H, MLA_NOPE + MLA_V)
    cos, sin = _rope_tables(S, H + 1)
    x1 = jnp.concatenate([q[:, :, MLA_NOPE:MLA_NOPE + half].reshape(S, H * half), k_rope[:, :half]], axis=1)
    x2 = jnp.concatenate([q[:, :, MLA_NOPE + half:].reshape(S, H * half), k_rope[:, half:]], axis=1)
    r1, r2 = _rotate(x1, x2, cos, sin, BF16, f"mla_rope_{tag}")
    qr = jnp.concatenate([r1[:, :H * half].reshape(S, H, half), r2[:, :H * half].reshape(S, H, half)], axis=2)
    kr = jnp.concatenate([r1[:, H * half:], r2[:, H * half:]], axis=1)
    qh = jnp.concatenate([q[:, :, :MLA_NOPE].astype(BF16), qr], axis=2).transpose(1, 0, 2)
    kh = jnp.concatenate([kv[:, :, :MLA_NOPE], jnp.broadcast_to(kr[:, None, :], (S, H, MLA_ROPE))],
                         axis=2).transpose(1, 0, 2)
    vh = kv[:, :, MLA_NOPE:].transpose(1, 0, 2)
    scale = (MLA_NOPE + MLA_ROPE) ** -0.5
    oh, lse = flash_fwd(qh, kh, vh, scale, f"mla_attn_{tag}")
    o2 = _heads_minor(oh)
    y = mm_nn_res(o2, w_o, x, f"mla_out_{tag}")
    return y, (x, h, rstd, c_q, rstd_q, cq_n, c_kv, rstd_kv, ckv_n, qh, kh, vh, oh, lse, o2, cos, sin)


def mla_layer_bwd(dy, saved, norm_g, w_dqkv, q_norm, kv_norm, w_uq, w_ukv, w_o, tag):
    x, h, rstd, c_q, rstd_q, cq_n, c_kv, rstd_kv, ckv_n, qh, kh, vh, oh, lse, o2, cos, sin = saved
    H, S, _ = qh.shape
    half = MLA_ROPE // 2
    scale = (MLA_NOPE + MLA_ROPE) ** -0.5
    dw_o = mm_tn(o2, dy, 1, BF16, f"mla_dwo_{tag}")
    do2 = mm_nt(dy, w_o, BF16, f"mla_do_{tag}")
    doh = _heads_major(do2, H)
    delta = flash_delta(oh, doh, f"mla_delta_{tag}")
    dqh, dkh, dvh = flash_bwd(qh, kh, vh, doh, lse.reshape(H, 1, S), delta.reshape(H, 1, S), scale,
                              f"mla_dattn_{tag}")
    dq = dqh.transpose(1, 0, 2)
    dk = dkh.transpose(1, 0, 2)
    dkr = jnp.sum(dk[:, :, MLA_NOPE:], axis=1)
    g1 = jnp.concatenate([dq[:, :, MLA_NOPE:MLA_NOPE + half].reshape(S, H * half), dkr[:, :half]], axis=1)
    g2 = jnp.concatenate([dq[:, :, MLA_NOPE + half:].reshape(S, H * half), dkr[:, half:]], axis=1)
    b1, b2 = _rotate(g1, g2, cos, -sin, F32, f"mla_drope_{tag}")
    dq_rope = jnp.concatenate([b1[:, :H * half].reshape(S, H, half), b2[:, :H * half].reshape(S, H, half)], axis=2)
    dk_rope = jnp.concatenate([b1[:, H * half:], b2[:, H * half:]], axis=1)
    dq_full = jnp.concatenate([dq[:, :, :MLA_NOPE], dq_rope], axis=2).reshape(S, -1).astype(BF16)
    dkv = jnp.concatenate([dk[:, :, :MLA_NOPE].astype(BF16), dvh.transpose(1, 0, 2)], axis=2).reshape(S, -1)
    dw_uq = mm_tn(cq_n, dq_full, _nblocks(w_uq), BF16, f"mla_dwuq_{tag}")
    dw_ukv = mm_tn(ckv_n, dkv, _nblocks(w_ukv), BF16, f"mla_dwukv_{tag}")
    dcq_n = mm_nt(dq_full, w_uq, F32, f"mla_dcq_{tag}")
    dckv_n = mm_nt(dkv, w_ukv, F32, f"mla_dckv_{tag}")
    dc_q, dq_norm = rms_bwd(dcq_n, c_q, rstd_q, q_norm, f"mla_dqnorm_{tag}")
    dc_kv, dkv_norm = rms_bwd(dckv_n, c_kv, rstd_kv, kv_norm, f"mla_dkvnorm_{tag}")
    dd = jnp.concatenate([dc_q, dc_kv, dk_rope], axis=1).astype(BF16)
    dw_dqkv = mm_tn(h, dd, 1, BF16, f"mla_dwdown_{tag}")
    dh_ = mm_nt(dd, w_dqkv, F32, f"mla_dh_{tag}")
    dx, dnorm = rms_bwd(dh_, x, rstd, norm_g, f"mla_dnorm_{tag}", dres=dy)
    return dx, (dnorm, dw_dqkv, dq_norm, dkv_norm, dw_uq, dw_ukv, dw_o)


SLAB = LANES
SLAB_GROUPS = SLAB // SSM_GROUP_CH
SLAB_HALF = SLAB_GROUPS * SSM_STATE
SLAB_W = 2 * SLAB_HALF


def _scan_rows(st_ref, carry_ref, lam_ref, nt, rev):
    h = SLAB_HALF
    lr = lam_ref[:, :h]
    li = lam_ref[:, h:]

    def step(i, c):
        xr, xi = c
        ii = (nt - 1 - i) if rev else i
        row = pl.multiple_of(ii * SEGS, SEGS)
        nr = lr * xr - li * xi + st_ref[pl.ds(row, SEGS), :h]
        ni = lr * xi + li * xr + st_ref[pl.ds(row, SEGS), h:]
        st_ref[pl.ds(row, SEGS), :h] = nr
        st_ref[pl.ds(row, SEGS), h:] = ni
        return nr, ni

    xr, xi = lax.fori_loop(0, nt, step, (carry_ref[:, :h], carry_ref[:, h:]), unroll=4)
    carry_ref[:, :h] = xr
    carry_ref[:, h:] = xi


def s5_scan(mode, inp, win, lam, rev, name, init=None, wout=None, xs=None, xinit=None, u=None, rows=512):
    T, C = inp.shape
    K = win.shape[0]
    W = SLAB_W
    Tc = _divisor(T, rows, 16)
    nt = Tc // SEGS
    nT = T // Tc
    tiles = T // SEGS

    def chunk(jj):
        return (nT - 1 - jj) if rev else jj

    slab_in = pl.BlockSpec((Tc, SLAB), lambda k, jj: (chunk(jj), k))
    wspec = pl.BlockSpec((None, SLAB, W), lambda k, jj: (k, 0, 0))
    vspec = pl.BlockSpec((None, SEGS, W), lambda k, jj: (k, 0, 0))
    wospec = pl.BlockSpec((None, W, SLAB), lambda k, jj: (k, 0, 0))
    xspec = pl.BlockSpec((Tc, W), lambda k, jj: (chunk(jj), k))
    scratch = [pltpu.VMEM((Tc, W), F32), pltpu.VMEM((SEGS, W), F32)]
    sem = _params("parallel", "arbitrary")

    def project_in(in_ref, w_ref, st_ref):
        st_ref[...] = jnp.dot(in_ref[...].astype(BF16), w_ref[...], preferred_element_type=F32)

    if mode == "finals":
        def body(in_ref, w_ref, lam_ref, fin_ref, st_ref, carry_ref):
            jj = pl.program_id(1)

            @pl.when(jj == 0)
            def _():
                carry_ref[...] = jnp.zeros_like(carry_ref)

            project_in(in_ref, w_ref, st_ref)
            _scan_rows(st_ref, carry_ref, lam_ref, nt, rev)

            @pl.when(jj == nT - 1)
            def _():
                fin_ref[...] = carry_ref[...]

        return pl.pallas_call(
            body, name=name, out_shape=jax.ShapeDtypeStruct((K, SEGS, W), F32), grid=(K, nT),
            in_specs=[slab_in, wspec, vspec], out_specs=vspec, scratch_shapes=scratch, compiler_params=sem,
        )(inp, win, lam)

    if mode == "fwd":
        def body(in_ref, w_ref, lam_ref, init_ref, wo_ref, xs_ref, y_ref, st_ref, carry_ref):
            jj = pl.program_id(1)

            @pl.when(jj == 0)
            def _():
                carry_ref[...] = init_ref[...]

            project_in(in_ref, w_ref, st_ref)
            _scan_rows(st_ref, carry_ref, lam_ref, nt, rev)
            xs = st_ref[...]
            xs_ref[...] = xs
            y_ref[...] = jnp.dot(xs.astype(BF16), wo_ref[...], preferred_element_type=F32)

        return pl.pallas_call(
            body, name=name,
            out_shape=(jax.ShapeDtypeStruct((T, K * W), F32), jax.ShapeDtypeStruct((T, C), F32)), grid=(K, nT),
            in_specs=[slab_in, wspec, vspec, vspec, wospec], out_specs=(xspec, slab_in),
            scratch_shapes=scratch, compiler_params=sem,
        )(inp, win, lam, init, wout)

    assert mode == "bwd"
    x_rev = not rev

    def halo_index(k, jj):
        ch = chunk(jj)
        tile = jnp.minimum((ch + 1) * nt, tiles - 1) if x_rev else jnp.maximum(ch * nt - 1, 0)
        return (tile, k)

    halo = pl.BlockSpec((SEGS, W), halo_index)

    def body(in_ref, w_ref, lam_ref, init_ref, wo_ref, xs_ref, xh_ref, xi_ref, u_ref,
             du_ref, dwin_ref, dwout_ref, dlam_ref, st_ref, carry_ref):
        jj = pl.program_id(1)
        ch = chunk(jj)

        @pl.when(jj == 0)
        def _():
            carry_ref[...] = init_ref[...]

        g = in_ref[...].astype(BF16)
        st_ref[...] = jnp.dot(g, w_ref[...], preferred_element_type=F32)
        _scan_rows(st_ref, carry_ref, lam_ref, nt, rev)
        adj = st_ref[...]
        adj16 = adj.astype(BF16)
        du_ref[...] = jnp.dot(adj16, wo_ref[...], preferred_element_type=F32)
        xs = xs_ref[...]
        edge = (ch == nT - 1) if x_rev else (ch == 0)
        first = jnp.where(edge, xi_ref[...], xh_ref[...])
        if x_rev:
            xp = jnp.concatenate([xs[SEGS:], first], axis=0)
        else:
            xp = jnp.concatenate([first, xs[:Tc - SEGS]], axis=0)
        h = SLAB_HALF
        ar, ai, pr, pi = adj[:, :h], adj[:, h:], xp[:, :h], xp[:, h:]
        dlr = (ar * pr + ai * pi).reshape(nt, SEGS, h).sum(axis=0)
        dli = (ai * pr - ar * pi).reshape(nt, SEGS, h).sum(axis=0)
        dwin = lax.dot_general(u_ref[...].astype(BF16), adj16, (((0,), (0,)), ((), ())), preferred_element_type=F32)
        dwout = lax.dot_general(xs.astype(BF16), g, (((0,), (0,)), ((), ())), preferred_element_type=F32)

        @pl.when(jj == 0)
        def _():
            dwin_ref[...] = dwin
            dwout_ref[...] = dwout
            dlam_ref[:, :h] = dlr
            dlam_ref[:, h:] = dli

        @pl.when(jj > 0)
        def _():
            dwin_ref[...] += dwin
            dwout_ref[...] += dwout
            dlam_ref[:, :h] += dlr
            dlam_ref[:, h:] += dli

    return pl.pallas_call(
        body, name=name,
        out_shape=(jax.ShapeDtypeStruct((T, C), F32), jax.ShapeDtypeStruct((K, SLAB, W), F32),
                   jax.ShapeDtypeStruct((K, W, SLAB), F32), jax.ShapeDtypeStruct((K, SEGS, W), F32)),
        grid=(K, nT),
        in_specs=[slab_in, wspec, vspec, vspec, wospec, xspec, halo, vspec, slab_in],
        out_specs=(slab_in, wspec, wospec, vspec), scratch_shapes=scratch, compiler_params=sem,
    )(inp, win, lam, init, wout, xs, xs, xinit, u)


def _s5_discretize(a_re, a_im, log_step, b_re, b_im):
    step = jnp.exp(log_step)[:, None]
    mag = jnp.exp(step * a_re)
    lb_re = mag * jnp.cos(step * a_im)
    lb_im = mag * jnp.sin(step * a_im)
    n_re, n_im = lb_re - 1.0, lb_im
    den = a_re * a_re + a_im * a_im
    coef_re = (n_re * a_re + n_im * a_im) / den
    coef_im = (n_im * a_re - n_re * a_im) / den
    bb_re = coef_re[..., None] * b_re - coef_im[..., None] * b_im
    bb_im = coef_re[..., None] * b_im + coef_im[..., None] * b_re
    return lb_re, lb_im, bb_re, bb_im


def _slab_in_matrix(bb_re, bb_im):
    G, N, Cg = bb_re.shape
    K = G // SLAB_GROUPS
    eye = jnp.eye(SLAB_GROUPS, dtype=F32)
    parts = [jnp.einsum('kgnc,gh->kgchn', b.reshape(K, SLAB_GROUPS, N, Cg), eye).reshape(K, SLAB, SLAB_HALF)
             for b in (bb_re, bb_im)]
    return jnp.concatenate(parts, axis=2)


def _slab_in_unpack(m):
    K = m.shape[0]
    m6 = m.reshape(K, SLAB_GROUPS, SSM_GROUP_CH, 2, SLAB_GROUPS, SSM_STATE)
    d = jnp.einsum('kgcphn,gh->pkgnc', m6, jnp.eye(SLAB_GROUPS, dtype=F32))
    d = d.reshape(2, K * SLAB_GROUPS, SSM_STATE, SSM_GROUP_CH)
    return d[0], d[1]


def _slab_out_matrix(c_re, c_im):
    G, Cg, N = c_re.shape
    K = G // SLAB_GROUPS
    eye = jnp.eye(SLAB_GROUPS, dtype=F32)
    parts = [jnp.einsum('kgcn,gh->kgnhc', c.reshape(K, SLAB_GROUPS, Cg, N), eye).reshape(K, SLAB_HALF, SLAB)
             for c in (c_re, -c_im)]
    return jnp.concatenate(parts, axis=1)


def _slab_out_unpack(m):
    K = m.shape[0]
    m6 = m.reshape(K, 2, SLAB_GROUPS, SSM_STATE, SLAB_GROUPS, SSM_GROUP_CH)
    d = jnp.einsum('kpgnhc,gh->pkgcn', m6, jnp.eye(SLAB_GROUPS, dtype=F32))
    d = d.reshape(2, K * SLAB_GROUPS, SSM_GROUP_CH, SSM_STATE)
    return d[0], -d[1]


def _slab_vec(re, im):
    K = re.shape[0] // SLAB_GROUPS
    v = jnp.concatenate([re.reshape(K, SLAB_HALF), im.reshape(K, SLAB_HALF)], axis=1)
    return jnp.broadcast_to(v[:, None, :], (K, SEGS, SLAB_W))


def _segment_inits(fin, lam, seg_len, rev):
    h = SLAB_HALF
    pr, pi = lam[:, 0, :h], lam[:, 0, h:]
    steps = int(round(math.log2(seg_len)))
    assert 2 ** steps == seg_len
    for _ in range(steps):
        pr, pi = pr * pr - pi * pi, 2.0 * pr * pi
    cr = jnp.zeros_like(pr)
    ci = jnp.zeros_like(pi)
    inits = [None] * SEGS
    for s in (range(SEGS - 1, -1, -1) if rev else range(SEGS)):
        inits[s] = jnp.concatenate([cr, ci], axis=1)
        cr, ci = pr * cr - pi * ci + fin[:, s, :h], pr * ci + pi * cr + fin[:, s, h:]
    return jnp.stack(inits, axis=1)


def _time_permute(x):
    T, C = x.shape
    return x.reshape(SEGS, T // SEGS, C).transpose(1, 0, 2).reshape(T, C)


def _time_unpermute(x):
    T, C = x.shape
    return x.reshape(T // SEGS, SEGS, C).transpose(1, 0, 2).reshape(T, C)


_GELU_K = math.sqrt(2.0 / math.pi)
_GELU_A = 0.044715


def _gelu_grad(y):
    t = jnp.tanh(_GELU_K * (y + _GELU_A * y * y * y))
    return 0.5 * (1.0 + t) + 0.5 * y * (1.0 - t * t) * (_GELU_K * (1.0 + 3.0 * _GELU_A * y * y))


def _conj(lam):
    return jnp.concatenate([lam[:, :, :SLAB_HALF], -lam[:, :, SLAB_HALF:]], axis=2)


def s5_layer_fwd(x, norm_g, ssm, w_glu, tag):
    S, D = x.shape
    u_nat, rstd = rms_fwd(x, norm_g, F32, f"s5_norm_{tag}")
    u = _time_permute(u_nat)
    dirs = []
    ys = []
    for dr in range(2):
        rev = dr == 1
        lb_re, lb_im, bb_re, bb_im = _s5_discretize(ssm["a_re"][dr], ssm["a_im"][dr], ssm["log_step"][dr],
                                                    ssm["b_re"][dr], ssm["b_im"][dr])
        win = _slab_in_matrix(bb_re, bb_im).astype(BF16)
        wout = _slab_out_matrix(ssm["c_re"][dr], ssm["c_im"][dr]).astype(BF16)
        lam = _slab_vec(lb_re, lb_im)
        fin = s5_scan("finals", u, win, lam, rev, f"s5_fin{dr}_{tag}")
        init = _segment_inits(fin, lam, S // SEGS, rev)
        xs, y = s5_scan("fwd", u, win, lam, rev, f"s5_fwd{dr}_{tag}", init=init, wout=wout)
        dirs.append((win, wout, lam, init, xs))
        ys.append(y)
    yy, zb = ew(lambda uu, a, b, d: (d * uu + a + b, jax.nn.gelu(d * uu + a + b)), f"s5_y_{tag}",
                [u, ys[0], ys[1]], [ssm["d"]], [F32, BF16])
    lin = mm_nn(zb, w_glu, F32, f"s5_glu_{tag}", tn_cap=512)
    mix = ew(lambda y_, l_, b: jax.nn.gelu(y_) * jax.nn.sigmoid(l_ + b), f"s5_mix_{tag}",
             [yy, lin], [ssm["b_glu"]], [F32])[0]
    out = x + _time_unpermute(mix)
    return out, (x, rstd, u, dirs, yy, zb, lin)


def s5_layer_bwd(dy, saved, norm_g, ssm, w_glu, tag):
    x, rstd, u, dirs, yy, zb, lin = saved
    S, D = x.shape
    dmix = _time_permute(dy)

    def glu_back(dm, y_, l_, b):
        z = jax.nn.gelu(y_)
        sg = jax.nn.sigmoid(l_ + b)
        dlin = dm * z * (sg * (1.0 - sg))
        return dlin, dm * sg, dlin

    dlin, dz_direct, db_glu = ew(glu_back, f"s5_dmix_{tag}", [dmix, yy, lin], [ssm["b_glu"]], [BF16, F32], n_sums=1)
    dw_glu = mm_tn(zb, dlin, 1, BF16, f"s5_dwglu_{tag}", tn_cap=512)
    dz_mm = mm_nt(dlin, w_glu, F32, f"s5_dz_{tag}")

    def gelu_back(a, b, y_, uu):
        dyy = (a + b) * _gelu_grad(y_)
        return dyy, dyy * uu

    dyy, dd = ew(gelu_back, f"s5_dy_{tag}", [dz_direct, dz_mm, yy, u], [], [F32], n_sums=1)
    grads = {"d": dd, "b_glu": db_glu, "w_glu": dw_glu}
    dus = []
    per_dir = []
    for dr in range(2):
        rev = dr == 1
        win, wout, lam, xinit, xs = dirs[dr]
        lamc = _conj(lam)
        ein = wout.transpose(0, 2, 1)
        eout = win.transpose(0, 2, 1)
        fin = s5_scan("finals", dyy, ein, lamc, not rev, f"s5_bfin{dr}_{tag}")
        init = _segment_inits(fin, lamc, S // SEGS, not rev)
        du, dwin, dwout, dlam = s5_scan("bwd", dyy, ein, lamc, not rev, f"s5_bwd{dr}_{tag}", init=init, wout=eout,
                                        xs=xs, xinit=xinit, u=u)
        dus.append(du)
        dbb_re, dbb_im = _slab_in_unpack(dwin)
        dc_re, dc_im = _slab_out_unpack(dwout)
        dl = dlam.sum(axis=1)
        dlb_re = dl[:, :SLAB_HALF].reshape(-1, SSM_STATE)
        dlb_im = dl[:, SLAB_HALF:].reshape(-1, SSM_STATE)
        prm = (ssm["a_re"][dr], ssm["a_im"][dr], ssm["log_step"][dr], ssm["b_re"][dr], ssm["b_im"][dr])
        _, vjp = jax.vjp(_s5_discretize, *prm)
        per_dir.append(vjp((dlb_re, dlb_im, dbb_re, dbb_im)) + (dc_re, dc_im))
    for i, nm in enumerate(["a_re", "a_im", "log_step", "b_re", "b_im", "c_re", "c_im"]):
        grads[nm] = jnp.stack([per_dir[0][i], per_dir[1][i]], axis=0)
    du_p = ew(lambda g, a, b, d: d * g + a + b, f"s5_du_{tag}", [dyy, dus[0], dus[1]], [ssm["d"]], [F32])[0]
    dx, dnorm = rms_bwd(_time_unpermute(du_p), x, rstd, norm_g, f"s5_dnorm_{tag}", dres=dy)
    grads["norm"] = dnorm
    return dx, grads


def final_loss(x, g, target, name, ts=512):
    S, D = x.shape
    ts = _divisor(S, ts, 16)

    def body(x_ref, g_ref, t_ref, loss_ref, dx_ref, dg_ref):
        i = pl.program_id(0)
        x = x_ref[...]
        gg = g_ref[...]
        r = lax.rsqrt(jnp.mean(x * x, axis=-1, keepdims=True) + RMS_EPS)
        xhat = x * r
        err = xhat * gg - t_ref[...]
        row_loss = jnp.mean(err * err, axis=-1, keepdims=True)
        part = jnp.broadcast_to(0.5 * jnp.sum(row_loss, axis=0, keepdims=True), (1, LANES))
        dy = err * (1.0 / D)
        dhg = dy * gg
        c = jnp.mean(dhg * xhat, axis=-1, keepdims=True)
        dx_ref[...] = r * (dhg - xhat * c)
        dg = jnp.sum(dy * xhat, axis=0, keepdims=True)

        @pl.when(i == 0)
        def _():
            loss_ref[...] = part
            dg_ref[...] = dg

        @pl.when(i > 0)
        def _():
            loss_ref[...] += part
            dg_ref[...] += dg

    row = pl.BlockSpec((ts, D), lambda i: (i, 0))
    vec = pl.BlockSpec((1, D), lambda i: (0, 0))
    return pl.pallas_call(
        body, name=name,
        out_shape=(jax.ShapeDtypeStruct((1, LANES), F32), jax.ShapeDtypeStruct((S, D), F32),
                   jax.ShapeDtypeStruct((1, D), F32)),
        grid=(S // ts,), in_specs=[row, vec, row],
        out_specs=(pl.BlockSpec((1, LANES), lambda i: (0, 0)), row, vec),
        compiler_params=_params("arbitrary"),
    )(x, g, target)


FLAT_W = 8 * LANES


def _adamw_math(w, g, m, v):
    m = ADAM_B1 * m + (1.0 - ADAM_B1) * g
    v = ADAM_B2 * v + (1.0 - ADAM_B2) * (g * g)
    m_hat = m / (1.0 - ADAM_B1 ** ADAM_STEP)
    v_hat = v / (1.0 - ADAM_B2 ** ADAM_STEP)
    delta = -ADAM_LR * (m_hat / (jnp.sqrt(v_hat) + ADAM_EPS) + ADAM_WD * w)
    return delta, m, v


def _ordered_sum(parts_ref):
    total = parts_ref[0].astype(F32)
    for s in range(1, N_DEV):
        total = total + parts_ref[s].astype(F32)
    return total


ADAMW_BLOCK_ELEMS = 256 * 1024


def adamw_from_parts(parts, w, m, v, name):
    R, B = w.shape
    tr = _divisor(R, max(16, ADAMW_BLOCK_ELEMS // B), 16)

    def body(p_ref, w_ref, m_ref, v_ref, g_ref, d_ref, nm_ref, nv_ref):
        g = _ordered_sum(p_ref)
        delta, nm, nv = _adamw_math(w_ref[...], g, m_ref[...], v_ref[...])
        g_ref[...] = g
        d_ref[...] = delta
        nm_ref[...] = nm
        nv_ref[...] = nv

    flat = pl.BlockSpec((tr, B), lambda i: (i, 0))
    out = jax.ShapeDtypeStruct((R, B), F32)
    return pl.pallas_call(
        body, name=name, out_shape=(out, out, out, out), grid=(R // tr,),
        in_specs=[pl.BlockSpec((N_DEV, tr, B), lambda i: (0, i, 0)), flat, flat, flat],
        out_specs=(flat, flat, flat, flat), compiler_params=_params("parallel"),
    )(parts, w, m, v)


def sum_parts(parts, name, tr=512):
    R = parts.shape[1]
    tr = _divisor(R, tr, 16)

    def body(p_ref, o_ref):
        o_ref[...] = _ordered_sum(p_ref)

    return pl.pallas_call(
        body, name=name, out_shape=jax.ShapeDtypeStruct((R, FLAT_W), F32), grid=(R // tr,),
        in_specs=[pl.BlockSpec((N_DEV, tr, FLAT_W), lambda i: (0, i, 0))],
        out_specs=pl.BlockSpec((tr, FLAT_W), lambda i: (i, 0)), compiler_params=_params("parallel"),
    )(parts)


def adamw_flat(g, w, m, v, name, tr=512):
    R = w.shape[0]
    tr = _divisor(R, tr, 16)

    def body(g_ref, w_ref, m_ref, v_ref, d_ref, nm_ref, nv_ref):
        delta, nm, nv = _adamw_math(w_ref[...], g_ref[...], m_ref[...], v_ref[...])
        d_ref[...] = delta
        nm_ref[...] = nm
        nv_ref[...] = nv

    flat = pl.BlockSpec((tr, FLAT_W), lambda i: (i, 0))
    out = jax.ShapeDtypeStruct((R, FLAT_W), F32)
    return pl.pallas_call(
        body, name=name, out_shape=(out, out, out), grid=(R // tr,),
        in_specs=[flat, flat, flat, flat], out_specs=(flat, flat, flat), compiler_params=_params("parallel"),
    )(g, w, m, v)


MESH_ID = pl.DeviceIdType.MESH
HBM_SPEC = pl.BlockSpec(memory_space=pltpu.HBM)


def _position():
    x, y, c = lax.axis_index("x"), lax.axis_index("y"), lax.axis_index("c")
    return x, y, c


def _flat_index(px, py, pc):
    return 4 * px + 2 * py + pc


def all_gather(arrays, axes, name):
    n = len(arrays)

    def body(*refs):
        ins, outs = refs[:n], refs[n:2 * n]
        send_sems, recv_sems, local_sems = refs[2 * n:]
        x, y, c = _position()
        me, sibling = (x, y, c), (x, y, 1 - c)
        chips = [(1 - x, y), (x, 1 - y), (1 - x, 1 - y)]

        def block_of(a, pos):
            idx = _flat_index(*pos)
            return outs[a].at[:, idx] if axes[a] == 1 else outs[a].at[idx]

        def copy(a, k, block, to, src=None):
            rows = block_of(a, block)
            return pltpu.make_async_remote_copy(
                src_ref=rows if src is None else src, dst_ref=rows,
                send_sem=send_sems.at[7 * a + k], recv_sem=recv_sems.at[7 * a + k],
                device_id=to, device_id_type=MESH_ID)

        mine, first, passed = [], [], []
        for a in range(n):
            cp = pltpu.make_async_copy(ins[a], block_of(a, me), local_sems.at[a])
            cp.start()
            mine.append(cp)
            first.append(copy(a, 0, me, sibling, src=ins[a]))
            first += [copy(a, 1 + j, me, (*chip, c), src=ins[a]) for j, chip in enumerate(chips)]
        for cp in first:
            cp.start()
        for a in range(n):
            for j, chip in enumerate(chips):
                copy(a, 1 + j, (*chip, c), me).wait_recv()
                fwd = copy(a, 4 + j, (*chip, c), sibling)
                fwd.start()
                passed.append(fwd)
        for a in range(n):
            copy(a, 0, sibling, me).wait_recv()
            for j, chip in enumerate(chips):
                copy(a, 4 + j, (*chip, 1 - c), me).wait_recv()
        for cp in first + passed:
            cp.wait_send()
        for cp in mine:
            cp.wait()

    return pl.pallas_call(
        body, name=name,
        out_shape=tuple(jax.ShapeDtypeStruct(a.shape[:ax] + (N_DEV,) + a.shape[ax:], a.dtype)
                        for a, ax in zip(arrays, axes)),
        in_specs=[HBM_SPEC] * n, out_specs=tuple([HBM_SPEC] * n),
        scratch_shapes=[pltpu.SemaphoreType.DMA((7 * n,)), pltpu.SemaphoreType.DMA((7 * n,)),
                        pltpu.SemaphoreType.DMA((n,))],
    )(*arrays)


def exchange(slotted, whole, name):
    n = len(slotted) + 1

    def body(*refs):
        srcs, dsts = refs[:n], refs[n:2 * n]
        send_sems, recv_sems, local_sems = refs[2 * n:]
        x, y, c = _position()
        me = _flat_index(x, y, c)

        def source(a, slot):
            return srcs[a].at[slot] if a < n - 1 else srcs[a]

        own = [pltpu.make_async_copy(source(a, me), dsts[a].at[me], local_sems.at[a]) for a in range(n)]
        for cp in own:
            cp.start()
        sends, recvs = [], []
        for r in range(1, N_DEV):
            peer = (1 - x if r & 4 else x, 1 - y if r & 2 else y, 1 - c if r & 1 else c)
            pidx = _flat_index(*peer)
            for a in range(n):
                k = 7 * a + r - 1
                sends.append(pltpu.make_async_remote_copy(
                    src_ref=source(a, pidx), dst_ref=dsts[a].at[me], send_sem=send_sems.at[k],
                    recv_sem=recv_sems.at[k], device_id=peer, device_id_type=MESH_ID))
                recvs.append(pltpu.make_async_remote_copy(
                    src_ref=source(a, pidx), dst_ref=dsts[a].at[pidx], send_sem=send_sems.at[k],
                    recv_sem=recv_sems.at[k], device_id=peer, device_id_type=MESH_ID))
        for cp in sends:
            cp.start()
        for cp in recvs:
            cp.wait_recv()
        for cp in sends:
            cp.wait_send()
        for cp in own:
            cp.wait()

    return pl.pallas_call(
        body, name=name,
        out_shape=tuple(jax.ShapeDtypeStruct(s.shape, s.dtype) for s in slotted)
        + (jax.ShapeDtypeStruct((N_DEV,) + whole.shape, whole.dtype),),
        in_specs=[HBM_SPEC] * n, out_specs=tuple([HBM_SPEC] * n),
        scratch_shapes=[pltpu.SemaphoreType.DMA((7 * n,)), pltpu.SemaphoreType.DMA((7 * n,)),
                        pltpu.SemaphoreType.DMA((n,))],
    )(*slotted, whole)


WEIGHTS = ['mix_norm', 'ffn_norm', 'final_norm', 'attn_w_qkv', 'attn_w_o', 'attn_sink', 'ssm_a_re', 'ssm_a_im',
           'ssm_log_step', 'ssm_b_re', 'ssm_b_im', 'ssm_c_re', 'ssm_c_im', 'ssm_d', 'ssm_w_glu', 'ssm_b_glu',
           'mla_w_dqkv', 'mla_q_norm', 'mla_kv_norm', 'mla_w_uq', 'mla_w_ukv', 'mla_w_o', 'ffn_w_up', 'ffn_conv_w',
           'ffn_conv_b', 'ffn_w_down']
BIG = [('attn_w_qkv', 'col'), ('attn_w_o', 'row'), ('ssm_w_glu', 'row'), ('mla_w_dqkv', 'row'), ('mla_w_uq', 'col'),
       ('mla_w_ukv', 'col'), ('mla_w_o', 'row'), ('ffn_w_up', 'col'), ('ffn_w_down', 'row')]
BIG_NAMES = [n for n, _ in BIG]
SMALL_SHARDED = ['mla_q_norm', 'mla_kv_norm', 'ffn_conv_w']
SMALL = [n for n in WEIGHTS if n not in BIG_NAMES]


def _pack(arrays, dtype):
    flat = jnp.concatenate([a.astype(dtype).reshape(-1) for a in arrays])
    pad = (-flat.shape[0]) % (16 * FLAT_W)
    if pad:
        flat = jnp.concatenate([flat, jnp.zeros((pad,), dtype)])
    return flat.reshape(-1, FLAT_W)


def _unpack(flat, shapes):
    flat = flat.reshape(-1)
    out, off = [], 0
    for s in shapes:
        n = int(np.prod(s))
        out.append(flat[off:off + n].reshape(s))
        off += n
    return out


def _own_slice(full, idx):
    n = full.shape[-1] // N_DEV
    return lax.dynamic_slice_in_dim(full, idx * n, n, axis=full.ndim - 1)


def kernel(x, mix_norm, ffn_norm, final_norm, attn_w_qkv, attn_w_o, attn_sink, ssm_a_re, ssm_a_im, ssm_log_step, ssm_b_re, ssm_b_im, ssm_c_re, ssm_c_im, ssm_d, ssm_w_glu, ssm_b_glu, mla_w_dqkv, mla_q_norm, mla_kv_norm, mla_w_uq, mla_w_ukv, mla_w_o, ffn_w_up, ffn_conv_w, ffn_conv_b, ffn_w_down, loss_target, m_mix_norm, m_ffn_norm, m_final_norm, m_attn_w_qkv, m_attn_w_o, m_attn_sink, m_ssm_a_re, m_ssm_a_im, m_ssm_log_step, m_ssm_b_re, m_ssm_b_im, m_ssm_c_re, m_ssm_c_im, m_ssm_d, m_ssm_w_glu, m_ssm_b_glu, m_mla_w_dqkv, m_mla_q_norm, m_mla_kv_norm, m_mla_w_uq, m_mla_w_ukv, m_mla_w_o, m_ffn_w_up, m_ffn_conv_w, m_ffn_conv_b, m_ffn_w_down, v_mix_norm, v_ffn_norm, v_final_norm, v_attn_w_qkv, v_attn_w_o, v_attn_sink, v_ssm_a_re, v_ssm_a_im, v_ssm_log_step, v_ssm_b_re, v_ssm_b_im, v_ssm_c_re, v_ssm_c_im, v_ssm_d, v_ssm_w_glu, v_ssm_b_glu, v_mla_w_dqkv, v_mla_q_norm, v_mla_kv_norm, v_mla_w_uq, v_mla_w_ukv, v_mla_w_o, v_ffn_w_up, v_ffn_conv_w, v_ffn_conv_b, v_ffn_w_down):
    given = dict(locals())
    idx = _flat_index(*_position())
    depth = mix_norm.shape[0]
    xs = x[0]
    S, D = xs.shape

    small_flat = _pack([given[n] for n in SMALL_SHARDED], F32)
    gathered = all_gather([given[n].astype(BF16) for n in BIG_NAMES] + [small_flat],
                          [1] * len(BIG_NAMES) + [0], "gather_weights")
    W4 = {}
    for (name, kind), wg in zip(BIG, gathered):
        layers, _, a, b = wg.shape
        if kind == 'row':
            W4[name] = wg.reshape(layers, 1, N_DEV * a, b)
        elif b % LANES:
            W4[name] = wg.transpose(0, 2, 1, 3).reshape(layers, 1, a, N_DEV * b)
        else:
            W4[name] = wg
    W = {name: [(w4, j) for j in range(w4.shape[0])] for name, w4 in W4.items()}
    sm = _unpack_rows(gathered[-1], [given[n].shape for n in SMALL_SHARDED])
    q_norm_full = sm[0][:, 0].reshape(1, -1)
    kv_norm_full = sm[1][:, 0].reshape(1, -1)
    conv_w_full = sm[2].transpose(1, 2, 0, 3).reshape(depth, 3, -1)
    F = conv_w_full.shape[2] // 2

    def conv_params(i):
        cw = conv_w_full[i].reshape(3, 2, F).transpose(1, 0, 2)
        cb = ffn_conv_b[i].reshape(2, 1, F)
        return cw, cb

    ssm = lambda j: {"a_re": ssm_a_re[j], "a_im": ssm_a_im[j], "log_step": ssm_log_step[j], "b_re": ssm_b_re[j],
                     "b_im": ssm_b_im[j], "c_re": ssm_c_re[j], "c_im": ssm_c_im[j], "d": ssm_d[j][None],
                     "b_glu": ssm_b_glu[j][None]}

    cur = xs
    saved = []
    for i in range(depth):
        kind, j, tag = i % 3, i // 3, f"l{i}"
        if kind == 0:
            cur, sv = swa_layer_fwd(cur, mix_norm[i][None], W['attn_w_qkv'][j], W['attn_w_o'][j], attn_sink[j], tag)
        elif kind == 1:
            cur, sv = s5_layer_fwd(cur, mix_norm[i][None], ssm(j), W['ssm_w_glu'][j], tag)
        else:
            cur, sv = mla_layer_fwd(cur, mix_norm[i][None], W['mla_w_dqkv'][j], q_norm_full, kv_norm_full,
                                    W['mla_w_uq'][j], W['mla_w_ukv'][j], W['mla_w_o'][j], tag)
        cw, cb = conv_params(i)
        cur, fsv = ffn_fwd(cur, ffn_norm[i][None], W['ffn_w_up'][i], cw, cb, W['ffn_w_down'][i], tag)
        saved.append((sv, fsv))
    loss_part, dcur, dfinal = final_loss(cur, final_norm[None], loss_target[0], "loss_head")

    gb = {n: [None] * given[n].shape[0] for n in BIG_NAMES}
    gs = {n: [None] * given[n].shape[0] for n in SMALL if given[n].ndim > 1}
    gs['final_norm'] = dfinal[0]

    def blocked(g, kind, name):
        a, b = given[name].shape[1:]
        if kind == 'row':
            return g.reshape(N_DEV, a, b)
        if b % LANES:
            return g.reshape(a, N_DEV, b).transpose(1, 0, 2)
        return g

    for i in reversed(range(depth)):
        kind, j, tag = i % 3, i // 3, f"l{i}"
        sv, fsv = saved[i]
        cw, cb = conv_params(i)
        dcur, (dn, dwup, dcw, dcb, dwdn) = ffn_bwd(dcur, fsv, ffn_norm[i][None], W['ffn_w_up'][i], cw, cb,
                                                   W['ffn_w_down'][i], tag)
        gs['ffn_norm'][i] = dn[0]
        gb['ffn_w_up'][i] = blocked(dwup, 'col', 'ffn_w_up')
        gb['ffn_w_down'][i] = blocked(dwdn, 'row', 'ffn_w_down')
        gs['ffn_conv_w'][i] = dcw.transpose(1, 0, 2).reshape(3, 2 * F)
        gs['ffn_conv_b'][i] = dcb.reshape(2 * F)
        if kind == 0:
            dcur, (dn, dwqkv, dwo, dsink) = swa_layer_bwd(dcur, sv, mix_norm[i][None], W['attn_w_qkv'][j],
                                                          W['attn_w_o'][j], tag)
            gb['attn_w_qkv'][j] = blocked(dwqkv, 'col', 'attn_w_qkv')
            gb['attn_w_o'][j] = blocked(dwo, 'row', 'attn_w_o')
            gs['attn_sink'][j] = dsink
        elif kind == 1:
            dcur, g5 = s5_layer_bwd(dcur, sv, mix_norm[i][None], ssm(j), W['ssm_w_glu'][j], tag)
            dn = g5["norm"]
            gb['ssm_w_glu'][j] = blocked(g5["w_glu"], 'row', 'ssm_w_glu')
            for nm in ("a_re", "a_im", "log_step", "b_re", "b_im", "c_re", "c_im"):
                gs['ssm_' + nm][j] = g5[nm]
            gs['ssm_d'][j] = g5["d"][0]
            gs['ssm_b_glu'][j] = g5["b_glu"][0]
        else:
            dcur, (dn, dwd, dqn, dkn, dwuq, dwukv, dwo) = mla_layer_bwd(
                dcur, sv, mix_norm[i][None], W['mla_w_dqkv'][j], q_norm_full, kv_norm_full, W['mla_w_uq'][j],
                W['mla_w_ukv'][j], W['mla_w_o'][j], tag)
            gb['mla_w_dqkv'][j] = blocked(dwd, 'row', 'mla_w_dqkv')
            gb['mla_w_uq'][j] = blocked(dwuq, 'col', 'mla_w_uq')
            gb['mla_w_ukv'][j] = blocked(dwukv, 'col', 'mla_w_ukv')
            gb['mla_w_o'][j] = blocked(dwo, 'row', 'mla_w_o')
            gs['mla_q_norm'][j] = dqn[0]
            gs['mla_kv_norm'][j] = dkn[0]
        gs['mix_norm'][i] = dn[0]
    grad_x = dcur[None]

    slotted = [jnp.stack(gb[n], axis=1) for n in BIG_NAMES]
    small_full = [gs[n] if n == 'final_norm' else jnp.stack(gs[n], axis=0) for n in SMALL]
    whole = _pack([loss_part[0, :1]] + small_full, F32)
    received = exchange(slotted, whole, "exchange_grads")
    parts_small = received[-1]

    out = {}
    for n, parts in zip(BIG_NAMES, received):
        shape = given[n].shape
        rows = (shape[0] * shape[1], shape[2])
        res = adamw_from_parts(parts.reshape((N_DEV,) + rows), given[n].reshape(rows), given['m_' + n].reshape(rows),
                               given['v_' + n].reshape(rows), f"adamw_{n}")
        for key, val in zip(("grad", "delta", "new_m", "new_v"), res):
            out[key, n] = val.reshape(shape)
    summed = _unpack(sum_parts(parts_small, "sum_small"), [(1,)] + [g.shape for g in small_full])
    loss = summed[0][0]
    small_grads = []
    for n, g in zip(SMALL, summed[1:]):
        small_grads.append(_own_slice(g, idx) if n in SMALL_SHARDED else g)
    small_shapes = [given[n].shape for n in SMALL]
    d_s, nm_s, nv_s = adamw_flat(_pack(small_grads, F32), _pack([given[n] for n in SMALL], F32),
                                 _pack([given['m_' + n] for n in SMALL], F32),
                                 _pack([given['v_' + n] for n in SMALL], F32), "adamw_small")
    for n, g in zip(SMALL, small_grads):
        out["grad", n] = g
    for key, flat in (("delta", d_s), ("new_m", nm_s), ("new_v", nv_s)):
        for n, val in zip(SMALL, _unpack(flat, small_shapes)):
            out[key, n] = val
    return (loss, grad_x, *[out["grad", n] for n in WEIGHTS], *[out["delta", n] for n in WEIGHTS],
            *[out["new_m", n] for n in WEIGHTS], *[out["new_v", n] for n in WEIGHTS])


def _unpack_rows(gathered, shapes):
    flat = gathered.reshape(N_DEV, -1)
    out, off = [], 0
    for s in shapes:
        n = int(np.prod(s))
        out.append(flat[:, off:off + n].reshape((N_DEV,) + tuple(s)))
        off += n
    return out
```

```python
import functools
import math

import numpy as np
import jax
import jax.numpy as jnp
from jax import lax
from jax.experimental import pallas as pl
from jax.experimental.pallas import tpu as pltpu

F32 = jnp.float32
BF16 = jnp.bfloat16

RMS_EPS = 1e-6
ATTN_HEAD_DIM = 64
ATTN_GROUP = 8
ATTN_BLOCK = 128
SSM_GROUP_CH = 16
SSM_STATE = 64
MLA_NOPE = 128
MLA_ROPE = 64
MLA_V = 128
ROPE_THETA = 10000.0
ADAM_LR = 0.001
ADAM_B1 = 0.9
ADAM_B2 = 0.999
ADAM_EPS = 1e-08
ADAM_WD = 0.01
ADAM_STEP = 10

N_DEV = 8
LANES = 128
SUBLANES = 8
VMEM_LIMIT_BYTES = 50 * 2 ** 20
MASK_VALUE = -1e30
LOG2E = math.log2(math.e)
SEGS = SUBLANES


def _params(*sem):
    return pltpu.CompilerParams(dimension_semantics=sem, vmem_limit_bytes=VMEM_LIMIT_BYTES)


def _divisor(n, cap, align):
    best = None
    d = align
    while d <= min(n, cap):
        if n % d == 0:
            best = d
        d += align
    return best if best is not None else n


def _layered(w):
    return w if isinstance(w, tuple) else (w[None], 0)


def _nblocks(w):
    return _layered(w)[0].shape[1]


def mm_nn(a, w, out_dtype, name, tm=1024, tn_cap=1408, parts=1, rider=None):
    M, K = a.shape
    w, layer = _layered(w)
    _, nb, K2, n = w.shape
    assert K == K2
    tm = _divisor(M, tm, 16)
    tn = n if nb > 1 else _divisor(n // parts, tn_cap, LANES)
    jn = n // tn
    J = nb * jn
    assert J % parts == 0
    jp = J // parts

    def body(a_ref, w_ref, o_ref):
        o_ref[...] = jnp.dot(a_ref[...].astype(BF16), w_ref[...], preferred_element_type=F32).astype(o_ref.dtype)

    if parts == 1:
        out_shape = jax.ShapeDtypeStruct((M, nb * n), out_dtype)
        out_spec = pl.BlockSpec((tm, tn), lambda i, j: (i, j))
    else:
        out_shape = jax.ShapeDtypeStruct((parts, M, nb * n // parts), out_dtype)
        out_spec = pl.BlockSpec((None, tm, tn), lambda i, j: (j // jp, i, j % jp))
    res = carried_call(
        body, rider, name, (M // tm, J),
        [pl.BlockSpec((tm, K), lambda i, j: (i, 0)),
         pl.BlockSpec((None, None, K, tn), lambda i, j: (layer, j // jn, 0, j % jn))],
        [out_spec], [out_shape], [], [a, w], ("parallel", "arbitrary"))
    return res[0] if rider is None else (res[0], res[1:])


def mm_nt(a, w, out_dtype, name, tm=1024, tk=1024, tc_cap=2816):
    w, layer = _layered(w)
    _, nb, K, n = w.shape
    split = a.ndim == 3
    M = a.shape[-2]
    tm = _divisor(M, tm, 16)
    tk = _divisor(K, tk, LANES)
    P = a.shape[0] if split else 1
    tc = n if nb > 1 else _divisor(n // P, tc_cap, LANES)
    jn = n // tc
    J = nb * jn
    if split:
        P = a.shape[0]
        assert J % P == 0 and a.shape[2] * P == nb * n
        jp = J // P
        a_spec = pl.BlockSpec((None, tm, tc), lambda i, k, j: (j // jp, i, j % jp))
    else:
        assert a.shape[1] == nb * n
        a_spec = pl.BlockSpec((tm, tc), lambda i, k, j: (i, j))

    def body(a_ref, w_ref, o_ref, acc_ref):
        j = pl.program_id(2)
        part = lax.dot_general(a_ref[...].astype(BF16), w_ref[...], (((1,), (1,)), ((), ())),
                               preferred_element_type=F32)

        @pl.when(j == 0)
        def _():
            acc_ref[...] = part

        @pl.when(j > 0)
        def _():
            acc_ref[...] += part

        @pl.when(j == J - 1)
        def _():
            o_ref[...] = acc_ref[...].astype(o_ref.dtype)

    return pl.pallas_call(
        body, name=name, out_shape=jax.ShapeDtypeStruct((M, K), out_dtype), grid=(M // tm, K // tk, J),
        in_specs=[a_spec, pl.BlockSpec((None, None, tk, tc), lambda i, k, j: (layer, j // jn, k, j % jn))],
        out_specs=pl.BlockSpec((tm, tk), lambda i, k, j: (i, k)),
        scratch_shapes=[pltpu.VMEM((tm, tk), F32)],
        compiler_params=_params("parallel", "parallel", "arbitrary"),
    )(a, w)


def mm_tn(a, b, nb, out_dtype, name, tm=1024, tka=1024, tn_cap=1408):
    M, Ka = a.shape
    split = b.ndim == 3
    N = b.shape[-1] * (b.shape[0] if split else 1)
    n = N // nb
    tm = _divisor(M, tm, 16)
    tka = _divisor(Ka, tka, LANES)
    tn = n if nb > 1 else _divisor(n // (b.shape[0] if split else 1), tn_cap, LANES)
    jn = n // tn
    J = nb * jn
    steps = M // tm
    if split:
        P = b.shape[0]
        assert J % P == 0
        jp = J // P
        b_spec = pl.BlockSpec((None, tm, tn), lambda k, j, i: (j // jp, i, j % jp))
    else:
        b_spec = pl.BlockSpec((tm, tn), lambda k, j, i: (i, j))

    def body(a_ref, b_ref, o_ref, acc_ref):
        i = pl.program_id(2)
        part = lax.dot_general(a_ref[...].astype(BF16), b_ref[...].astype(BF16), (((0,), (0,)), ((), ())),
                               preferred_element_type=F32)

        @pl.when(i == 0)
        def _():
            acc_ref[...] = part

        @pl.when(i > 0)
        def _():
            acc_ref[...] += part

        @pl.when(i == steps - 1)
        def _():
            o_ref[...] = acc_ref[...].astype(o_ref.dtype)

    return pl.pallas_call(
        body, name=name, out_shape=jax.ShapeDtypeStruct((nb, Ka, n), out_dtype), grid=(Ka // tka, J, steps),
        in_specs=[pl.BlockSpec((tm, tka), lambda k, j, i: (i, k)), b_spec],
        out_specs=pl.BlockSpec((None, tka, tn), lambda k, j, i: (j // jn, k, j % jn)),
        scratch_shapes=[pltpu.VMEM((tka, tn), F32)],
        compiler_params=_params("parallel", "parallel", "arbitrary"),
    )(a, b)


def rms_fwd(x, g, out_dtype, name, ts=512):
    S, D = x.shape
    ts = _divisor(S, ts, 16)

    def body(x_ref, g_ref, h_ref, r_ref):
        x = x_ref[...]
        r = lax.rsqrt(jnp.mean(x * x, axis=-1, keepdims=True) + RMS_EPS)
        h_ref[...] = ((x * r) * g_ref[...]).astype(h_ref.dtype)
        r_ref[...] = r

    return pl.pallas_call(
        body, name=name,
        out_shape=(jax.ShapeDtypeStruct((S, D), out_dtype), jax.ShapeDtypeStruct((S, 1), F32)),
        grid=(S // ts,),
        in_specs=[pl.BlockSpec((ts, D), lambda i: (i, 0)), pl.BlockSpec((1, D), lambda i: (0, 0))],
        out_specs=(pl.BlockSpec((ts, D), lambda i: (i, 0)), pl.BlockSpec((ts, 1), lambda i: (i, 0))),
        compiler_params=_params("parallel"),
    )(x, g)


def rms_bwd(dh, x, rstd, g, name, dres=None, ts=512):
    S, D = x.shape
    ts = _divisor(S, ts, 16)
    has_res = dres is not None

    def body(*refs):
        if has_res:
            dh_ref, x_ref, r_ref, g_ref, res_ref, dx_ref, dg_ref = refs
        else:
            dh_ref, x_ref, r_ref, g_ref, dx_ref, dg_ref = refs
        i = pl.program_id(0)
        dh = dh_ref[...].astype(F32)
        r = r_ref[...]
        xhat = x_ref[...] * r
        dhg = dh * g_ref[...]
        c = jnp.mean(dhg * xhat, axis=-1, keepdims=True)
        dx = r * (dhg - xhat * c)
        if has_res:
            dx = dx + res_ref[...]
        dx_ref[...] = dx
        part = jnp.sum(dh * xhat, axis=0, keepdims=True)

        @pl.when(i == 0)
        def _():
            dg_ref[...] = part

        @pl.when(i > 0)
        def _():
            dg_ref[...] += part

    row = pl.BlockSpec((ts, D), lambda i: (i, 0))
    args = [dh, x, rstd, g] + ([dres] if has_res else [])
    specs = [row, row, pl.BlockSpec((ts, 1), lambda i: (i, 0)), pl.BlockSpec((1, D), lambda i: (0, 0))]
    specs += [row] if has_res else []
    return pl.pallas_call(
        body, name=name,
        out_shape=(jax.ShapeDtypeStruct((S, D), F32), jax.ShapeDtypeStruct((1, D), F32)),
        grid=(S // ts,), in_specs=specs,
        out_specs=(row, pl.BlockSpec((1, D), lambda i: (0, 0))),
        compiler_params=_params("arbitrary"),
    )(*args)


def ew(fn, name, mats, rows, out_dtypes, n_sums=0, ts=512, tc=1024):
    S, C = mats[0].shape
    ts = _divisor(S, ts, 16)
    tc = _divisor(C, tc, LANES)
    n_in = len(mats) + len(rows)
    n_out = len(out_dtypes)

    def body(*refs):
        i = pl.program_id(1)
        res = fn(*[r[...] for r in refs[:n_in]])
        if not isinstance(res, (tuple, list)):
            res = (res,)
        for o_ref, val in zip(refs[n_in:n_in + n_out], res[:n_out]):
            o_ref[...] = val.astype(o_ref.dtype)
        for s_ref, val in zip(refs[n_in + n_out:], res[n_out:]):
            part = jnp.sum(val, axis=0, keepdims=True)

            @pl.when(i == 0)
            def _():
                s_ref[...] = part

            @pl.when(i > 0)
            def _():
                s_ref[...] += part

    mat = pl.BlockSpec((ts, tc), lambda j, i: (i, j))
    row = pl.BlockSpec((1, tc), lambda j, i: (0, j))
    out_shape = tuple(jax.ShapeDtypeStruct((S, C), d) for d in out_dtypes)
    out_shape += tuple(jax.ShapeDtypeStruct((1, C), F32) for _ in range(n_sums))
    return pl.pallas_call(
        body, name=name, out_shape=out_shape, grid=(C // tc, S // ts),
        in_specs=[mat] * len(mats) + [row] * len(rows),
        out_specs=tuple([mat] * n_out + [row] * n_sums),
        compiler_params=_params("parallel", "arbitrary"),
    )(*mats, *rows)


def _halo_specs(ts, tc, S, lead):
    nblk = S // SUBLANES
    per = ts // SUBLANES
    pre = (None,) * 0
    if lead:
        main = pl.BlockSpec((lead, ts, tc), lambda j, i: (0, i, j))
        prev = pl.BlockSpec((lead, SUBLANES, tc), lambda j, i: (0, jnp.maximum(i * per - 1, 0), j))
        nxt = pl.BlockSpec((lead, SUBLANES, tc), lambda j, i: (0, jnp.minimum((i + 1) * per, nblk - 1), j))
    else:
        main = pl.BlockSpec((ts, tc), lambda j, i: (i, j))
        prev = pl.BlockSpec((SUBLANES, tc), lambda j, i: (jnp.maximum(i * per - 1, 0), j))
        nxt = pl.BlockSpec((SUBLANES, tc), lambda j, i: (jnp.minimum((i + 1) * per, nblk - 1), j))
    return main, prev, nxt


def _extended(prev, main, nxt, i, ts, S):
    before = jnp.where(i > 0, prev.astype(F32), 0.0)
    after = jnp.where((i + 1) * ts < S, nxt.astype(F32), 0.0)
    return jnp.concatenate([before, main.astype(F32), after], axis=0)


def _taps(ue):
    n = ue.shape[0]
    return pltpu.roll(ue, 1, axis=0), ue, pltpu.roll(ue, n - 1, axis=0)


def _conv3(taps, w, b):
    return b + (w[0:1] * taps[0] + w[1:2] * taps[1] + w[2:3] * taps[2])


def convgate_fwd(u3, conv_w, conv_b, name, ts=512, tc=512):
    _, S, F = u3.shape
    ts = _divisor(S, ts, 16)
    tc = _divisor(F, tc, LANES)
    main, prev, nxt = _halo_specs(ts, tc, S, 2)

    def body(m_ref, p_ref, n_ref, w_ref, b_ref, o_ref):
        i = pl.program_id(1)
        c = []
        for h in range(2):
            ue = _extended(p_ref[h], m_ref[h], n_ref[h], i, ts, S)
            c.append(_conv3(_taps(ue), w_ref[h], b_ref[h])[SUBLANES:SUBLANES + ts])
        o_ref[...] = (jax.nn.silu(c[0]) * c[1]).astype(o_ref.dtype)

    return pl.pallas_call(
        body, name=name, out_shape=jax.ShapeDtypeStruct((S, F), BF16), grid=(F // tc, S // ts),
        in_specs=[main, prev, nxt, pl.BlockSpec((2, 3, tc), lambda j, i: (0, 0, j)),
                  pl.BlockSpec((2, 1, tc), lambda j, i: (0, 0, j))],
        out_specs=pl.BlockSpec((ts, tc), lambda j, i: (i, j)),
        compiler_params=_params("parallel", "arbitrary"),
    )(u3, u3, u3, conv_w, conv_b)


def convgate_bwd(u3, da, conv_w, conv_b, name, ts=512, tc=512, rider=None):
    _, S, F = u3.shape
    ts = _divisor(S, ts, 16)
    tc = _divisor(F, tc, LANES)
    main, prev, nxt = _halo_specs(ts, tc, S, 2)
    amain, aprev, anxt = _halo_specs(ts, tc, S, 0)
    n = ts + 2 * SUBLANES
    mid = slice(SUBLANES, SUBLANES + ts)

    def body(m_ref, p_ref, n_ref, am_ref, ap_ref, an_ref, w_ref, b_ref, du_ref, dw_ref, db_ref):
        i = pl.program_id(1)
        ue = [_extended(p_ref[h], m_ref[h], n_ref[h], i, ts, S) for h in range(2)]
        shifted = [_taps(ue[h]) for h in range(2)]
        g = _conv3(shifted[0], w_ref[0], b_ref[0])
        v = _conv3(shifted[1], w_ref[1], b_ref[1])
        dae = _extended(ap_ref[...], am_ref[...], an_ref[...], i, ts, S)
        sg = jax.nn.sigmoid(g)
        dc = [dae * v * (sg * (1.0 + g * (1.0 - sg))), dae * (g * sg)]
        for h in range(2):
            w = w_ref[h]
            du = w[0:1] * pltpu.roll(dc[h], n - 1, axis=0) + w[1:2] * dc[h] + w[2:3] * pltpu.roll(dc[h], 1, axis=0)
            du_ref[h] = du[mid].astype(du_ref.dtype)
            dcm = dc[h][mid]
            sums = [jnp.sum(dcm * t[mid], axis=0, keepdims=True) for t in shifted[h]]
            db = jnp.sum(dcm, axis=0, keepdims=True)

            @pl.when(i == 0)
            def _():
                for t in range(3):
                    dw_ref[h, t:t + 1, :] = sums[t]
                db_ref[h] = db

            @pl.when(i > 0)
            def _():
                for t in range(3):
                    dw_ref[h, t:t + 1, :] += sums[t]
                db_ref[h] += db

    res = carried_call(
        body, rider, name, (F // tc, S // ts),
        [main, prev, nxt, amain, aprev, anxt, pl.BlockSpec((2, 3, tc), lambda j, i: (0, 0, j)),
         pl.BlockSpec((2, 1, tc), lambda j, i: (0, 0, j))],
        [pl.BlockSpec((2, ts, tc), lambda j, i: (0, i, j)), pl.BlockSpec((2, 3, tc), lambda j, i: (0, 0, j)),
         pl.BlockSpec((2, 1, tc), lambda j, i: (0, 0, j))],
        [jax.ShapeDtypeStruct((2, S, F), BF16), jax.ShapeDtypeStruct((2, 3, F), F32),
         jax.ShapeDtypeStruct((2, 1, F), F32)], [], [u3, u3, u3, da, da, da, conv_w, conv_b],
        ("parallel", "arbitrary"))
    return tuple(res[:3]) if rider is None else (res[0], res[1], res[2], res[3:])


def ffn_fwd(x, norm_g, w_up, conv_w, conv_b, w_down, tag, rider=None):
    h, rstd = rms_fwd(x, norm_g, BF16, f"ffn_norm_{tag}")
    u3 = mm_nn(h, w_up, F32, f"ffn_up_{tag}", parts=2, rider=rider)
    if rider is not None:
        u3, got = u3
    a = convgate_fwd(u3, conv_w, conv_b, f"ffn_gate_{tag}")
    y = mm_nn_res(a, w_down, x, f"ffn_down_{tag}")
    return (y, (x, h, rstd, u3, a)) if rider is None else (y, (x, h, rstd, u3, a), got)


def mm_nn_res(a, w, res, name, tm=1024, tn=512):
    M, K = a.shape
    w, layer = _layered(w)
    N = w.shape[3]
    assert w.shape[1] == 1
    tm = _divisor(M, tm, 16)
    tn = _divisor(N, tn, LANES)

    def body(a_ref, w_ref, r_ref, o_ref):
        o_ref[...] = r_ref[...] + jnp.dot(a_ref[...].astype(BF16), w_ref[...], preferred_element_type=F32)

    return pl.pallas_call(
        body, name=name, out_shape=jax.ShapeDtypeStruct((M, N), F32), grid=(M // tm, N // tn),
        in_specs=[pl.BlockSpec((tm, K), lambda i, j: (i, 0)),
                  pl.BlockSpec((None, None, K, tn), lambda i, j: (layer, 0, 0, j)),
                  pl.BlockSpec((tm, tn), lambda i, j: (i, j))],
        out_specs=pl.BlockSpec((tm, tn), lambda i, j: (i, j)),
        compiler_params=_params("parallel", "arbitrary"),
    )(a, w, res)


def ffn_bwd(dy, saved, norm_g, w_up, conv_w, conv_b, w_down, tag, rider=None):
    x, h, rstd, u3, a = saved
    nb = _nblocks(w_up)
    dw_down = mm_tn(a, dy, 1, BF16, f"ffn_dwdown_{tag}")
    da = mm_nt(dy, w_down, F32, f"ffn_da_{tag}", tk=512)
    du3, dconv_w, dconv_b, *got = convgate_bwd(u3, da, conv_w, conv_b, f"ffn_dgate_{tag}", rider=rider)
    dh = mm_nt(du3, w_up, F32, f"ffn_dh_{tag}")
    dw_up = mm_tn(h, du3, nb, BF16, f"ffn_dwup_{tag}")
    dx, dnorm = rms_bwd(dh, x, rstd, norm_g, f"ffn_dnorm_{tag}", dres=dy)
    grads = (dnorm, dw_up, dconv_w, dconv_b, dw_down)
    return (dx, grads) if rider is None else (dx, grads, got[0])


def _swa_bias(n_heads):
    kv = n_heads // ATTN_GROUP
    slopes = jnp.asarray((2.0 ** (-8.0 * np.arange(1, n_heads + 1) / n_heads)).astype(np.float32))
    rel = jnp.arange(SWA_KEYS)[None, :] - ATTN_BLOCK - jnp.arange(ATTN_BLOCK)[:, None]
    dist = jnp.abs(rel).astype(F32)
    bias = (-slopes.reshape(kv, ATTN_GROUP, 1, 1) * dist) * LOG2E
    col = jnp.arange(SWA_KEYS)
    out = []
    for dead in (col < ATTN_BLOCK, col < 0, col >= 2 * ATTN_BLOCK):
        keep = (jnp.abs(rel) <= ATTN_BLOCK) & ~dead[None, :]
        out.append(jnp.where(keep[None, None], bias, MASK_VALUE).reshape(kv, SWA_ROWS, SWA_KEYS))
    return jnp.stack(out)


SWA_ROWS = ATTN_GROUP * ATTN_BLOCK
SWA_KEYS = 3 * ATTN_BLOCK
SWA_LOGIT_SCALE = ATTN_HEAD_DIM ** -0.5 * LOG2E


def _swa_probs(q, kc, bias, sink):
    reps = (1, SWA_KEYS // LANES)
    s = lax.dot_general(q, kc, (((1,), (1,)), ((), ())), preferred_element_type=F32) * SWA_LOGIT_SCALE + bias
    m = jnp.maximum(jnp.max(s, axis=1, keepdims=True), sink)
    p = jnp.exp2(s - jnp.tile(m, reps))
    es = jnp.exp2(sink - m)
    r = 1.0 / (jnp.sum(p, axis=1, keepdims=True) + es)
    return p * jnp.tile(r, reps), es * r


def _swa_specs(S):
    nb = S // ATTN_BLOCK

    def at(off):
        return pl.BlockSpec((None, ATTN_BLOCK, ATTN_HEAD_DIM),
                            lambda c, n: (c, jnp.clip(jnp.minimum(n, nb - 1) + off, 0, nb - 1), 0))

    qspec = pl.BlockSpec((None, ATTN_GROUP, ATTN_BLOCK, ATTN_HEAD_DIM), lambda c, n: (c, 0, jnp.minimum(n, nb - 1), 0))
    bias = pl.BlockSpec((None, None, SWA_ROWS, SWA_KEYS),
                        lambda c, n: (jnp.where(n == 0, 0, jnp.where(n >= nb - 1, 2, 1)), c, 0, 0))
    sink = pl.BlockSpec((None, SWA_ROWS, LANES), lambda c, n: (c, 0, 0))
    return qspec, [at(-1), at(0), at(1)], bias, sink


def swa_fwd(q, k, v, bias, sink, name):
    KV, G, S, dh = q.shape
    nb = S // ATTN_BLOCK
    qspec, kspecs, bspec, sspec = _swa_specs(S)

    def body(q_ref, k0, k1, k2, v0, v1, v2, b_ref, s_ref, o_ref):
        kc = jnp.concatenate([k0[...], k1[...], k2[...]], axis=0)
        vc = jnp.concatenate([v0[...], v1[...], v2[...]], axis=0)
        p, _ = _swa_probs(q_ref[...].reshape(SWA_ROWS, dh), kc, b_ref[...], s_ref[...])
        o = jnp.dot(p.astype(BF16), vc, preferred_element_type=F32)
        o_ref[...] = o.reshape(G, ATTN_BLOCK, dh).astype(o_ref.dtype)

    return pl.pallas_call(
        body, name=name, out_shape=jax.ShapeDtypeStruct(q.shape, BF16), grid=(KV, nb),
        in_specs=[qspec] + kspecs + kspecs + [bspec, sspec], out_specs=qspec,
        compiler_params=_params("parallel", "arbitrary"),
    )(q, k, k, k, v, v, v, bias, sink)


def swa_bwd(q, k, v, bias, sink, do, name, rider=None):
    KV, G, S, dh = q.shape
    nb = S // ATTN_BLOCK
    qspec, kspecs, bspec, sspec = _swa_specs(S)
    scale = ATTN_HEAD_DIM ** -0.5
    kv_out = pl.BlockSpec((None, ATTN_BLOCK, dh), lambda c, n: (c, jnp.maximum(n - 1, 0), 0))

    def body(q_ref, k0, k1, k2, v0, v1, v2, b_ref, s_ref, do_ref, dq_ref, dk_ref, dv_ref, ds_ref, dk_acc, dv_acc):
        n = pl.program_id(1)

        @pl.when(n == 0)
        def _():
            dk_acc[...] = jnp.zeros_like(dk_acc)
            dv_acc[...] = jnp.zeros_like(dv_acc)
            ds_ref[...] = jnp.zeros_like(ds_ref)

        @pl.when(n > 0)
        def _():
            dk_acc[(n + 1) % 3] = jnp.zeros((ATTN_BLOCK, dh), F32)
            dv_acc[(n + 1) % 3] = jnp.zeros((ATTN_BLOCK, dh), F32)

        @pl.when(n < nb)
        def _():
            kc = jnp.concatenate([k0[...], k1[...], k2[...]], axis=0)
            vc = jnp.concatenate([v0[...], v1[...], v2[...]], axis=0)
            qa = q_ref[...].reshape(SWA_ROWS, dh)
            da = do_ref[...].reshape(SWA_ROWS, dh)
            p, psink = _swa_probs(qa, kc, b_ref[...], s_ref[...])
            dp = lax.dot_general(da, vc, (((1,), (1,)), ((), ())), preferred_element_type=F32)
            delta = jnp.sum(p * dp, axis=1, keepdims=True) + jnp.zeros((SWA_ROWS, LANES), F32)
            ds = ((p * (dp - jnp.tile(delta, (1, SWA_KEYS // LANES)))) * scale).astype(BF16)
            dq = jnp.dot(ds, kc, preferred_element_type=F32)
            dq_ref[...] = dq.reshape(G, ATTN_BLOCK, dh).astype(dq_ref.dtype)
            dkc = lax.dot_general(ds, qa, (((0,), (0,)), ((), ())), preferred_element_type=F32)
            dvc = lax.dot_general(p.astype(BF16), da, (((0,), (0,)), ((), ())), preferred_element_type=F32)
            ds_ref[...] -= psink * delta
            for o in range(3):
                slot = (n + 2 + o) % 3
                dk_acc[slot] += dkc[o * ATTN_BLOCK:(o + 1) * ATTN_BLOCK]
                dv_acc[slot] += dvc[o * ATTN_BLOCK:(o + 1) * ATTN_BLOCK]

        done = (n + 2) % 3
        dk_ref[...] = dk_acc[done].astype(dk_ref.dtype)
        dv_ref[...] = dv_acc[done].astype(dv_ref.dtype)

    res = carried_call(
        body, rider, name, (KV, nb + 1), [qspec] + kspecs + kspecs + [bspec, sspec, qspec],
        [qspec, kv_out, kv_out, sspec],
        [jax.ShapeDtypeStruct(q.shape, BF16), jax.ShapeDtypeStruct(k.shape, BF16),
         jax.ShapeDtypeStruct(v.shape, BF16), jax.ShapeDtypeStruct((KV, SWA_ROWS, LANES), F32)],
        [pltpu.VMEM((3, ATTN_BLOCK, dh), F32), pltpu.VMEM((3, ATTN_BLOCK, dh), F32)],
        [q, k, k, k, v, v, v, bias, sink, do], ("parallel", "arbitrary"))
    return tuple(res[:4]) if rider is None else (res[0], res[1], res[2], res[3], res[4:])


def _heads_major(x, n_heads):
    S = x.shape[0]
    return x.reshape(S, n_heads, -1).transpose(1, 0, 2)


def _heads_minor(x):
    H, S, dh = x.shape
    return x.transpose(1, 0, 2).reshape(S, H * dh)


def swa_layer_fwd(x, norm_g, w_qkv, w_o, sink, tag):
    S, D = x.shape
    H = _layered(w_o)[0].shape[2] // ATTN_HEAD_DIM
    KV = H // ATTN_GROUP
    h, rstd = rms_fwd(x, norm_g, BF16, f"swa_norm_{tag}")
    qkv = mm_nn(h, w_qkv, BF16, f"swa_qkv_{tag}", tn_cap=1280)
    q = _heads_major(qkv[:, :H * ATTN_HEAD_DIM], H).reshape(KV, ATTN_GROUP, S, ATTN_HEAD_DIM)
    k = _heads_major(qkv[:, H * ATTN_HEAD_DIM:(H + KV) * ATTN_HEAD_DIM], KV)
    v = _heads_major(qkv[:, (H + KV) * ATTN_HEAD_DIM:], KV)
    bias = _swa_bias(H)
    sinkb = jnp.broadcast_to((sink.astype(F32) * LOG2E).reshape(KV, ATTN_GROUP, 1, 1),
                             (KV, ATTN_GROUP, ATTN_BLOCK, LANES)).reshape(KV, SWA_ROWS, LANES)
    o = swa_fwd(q, k, v, bias, sinkb, f"swa_attn_{tag}")
    o2 = _heads_minor(o.reshape(H, S, ATTN_HEAD_DIM))
    y = mm_nn_res(o2, w_o, x, f"swa_out_{tag}")
    return y, (x, h, rstd, q, k, v, bias, sinkb, o2)


def swa_layer_bwd(dy, saved, norm_g, w_qkv, w_o, tag, rider=None):
    x, h, rstd, q, k, v, bias, sinkb, o2 = saved
    KV, G, S, dh = q.shape
    H = KV * G
    dw_o = mm_tn(o2, dy, 1, BF16, f"swa_dwo_{tag}")
    do2 = mm_nt(dy, w_o, BF16, f"swa_do_{tag}")
    do = _heads_major(do2, H).reshape(KV, G, S, dh)
    dq, dk, dv, dsink, *got = swa_bwd(q, k, v, bias, sinkb, do, f"swa_dattn_{tag}", rider=rider)
    dqkv = jnp.concatenate([_heads_minor(dq.reshape(H, S, dh)), _heads_minor(dk), _heads_minor(dv)], axis=1)
    dw_qkv = mm_tn(h, dqkv, 1, BF16, f"swa_dwqkv_{tag}", tn_cap=1280)
    dh_ = mm_nt(dqkv, w_qkv, F32, f"swa_dh_{tag}", tc_cap=2560)
    dx, dnorm = rms_bwd(dh_, x, rstd, norm_g, f"swa_dnorm_{tag}", dres=dy)
    grads = (dnorm, dw_qkv, dw_o, dsink[:, :, 0].reshape(H, ATTN_BLOCK).sum(axis=1))
    return (dx, grads) if rider is None else (dx, grads, got[0])


def flash_fwd(q, k, v, scale, name, tq=1024, tk=512, rider=None):
    H, S, dk = q.shape
    dv = v.shape[-1]
    assert dv == LANES
    tq = _divisor(S, tq, 16)
    tk = _divisor(S, tk, LANES)
    nk = S // tk
    c = scale * LOG2E

    def body(q_ref, k_ref, v_ref, o_ref, lse_ref, m_sc, l_sc, acc_sc):
        m_sc[...] = jnp.full(m_sc.shape, MASK_VALUE, F32)
        l_sc[...] = jnp.zeros(l_sc.shape, F32)
        acc_sc[...] = jnp.zeros(acc_sc.shape, F32)

        def step(kt, carry):
            off = pl.multiple_of(kt * tk, tk)
            s = lax.dot_general(q_ref[...], k_ref[pl.ds(off, tk), :], (((1,), (1,)), ((), ())),
                                preferred_element_type=F32) * c
            m_prev = m_sc[...]
            m_new = jnp.maximum(m_prev, jnp.max(s, axis=1, keepdims=True))
            alpha = jnp.exp2(m_prev - m_new)
            p = jnp.exp2(s - jnp.tile(m_new, (1, tk // LANES)))
            l_sc[...] = alpha * l_sc[...] + jnp.sum(p, axis=1, keepdims=True)
            acc_sc[...] = alpha * acc_sc[...] + jnp.dot(p.astype(BF16), v_ref[pl.ds(off, tk), :],
                                                        preferred_element_type=F32)
            m_sc[...] = m_new
            return carry

        lax.fori_loop(0, nk, step, 0)
        l = l_sc[...]
        o_ref[...] = (acc_sc[...] / l).astype(o_ref.dtype)
        lse_ref[...] = (m_sc[...] + jnp.log2(l))[:, :1]

    res = carried_call(
        body, rider, name, (H, S // tq),
        [pl.BlockSpec((None, tq, dk), lambda h, i: (h, i, 0)),
         pl.BlockSpec((None, S, dk), lambda h, i: (h, 0, 0)),
         pl.BlockSpec((None, S, dv), lambda h, i: (h, 0, 0))],
        [pl.BlockSpec((None, tq, dv), lambda h, i: (h, i, 0)), pl.BlockSpec((None, tq, 1), lambda h, i: (h, i, 0))],
        [jax.ShapeDtypeStruct((H, S, dv), BF16), jax.ShapeDtypeStruct((H, S, 1), F32)],
        [pltpu.VMEM((tq, LANES), F32), pltpu.VMEM((tq, LANES), F32), pltpu.VMEM((tq, dv), F32)],
        [q, k, v], ("parallel", "arbitrary"))
    return (res[0], res[1]) if rider is None else (res[0], res[1], res[2:])


def flash_delta(o, do, name, ts=1024):
    H, S, dv = o.shape
    ts = _divisor(S, ts, 16)

    def body(o_ref, do_ref, d_ref):
        d_ref[...] = jnp.sum(o_ref[...].astype(F32) * do_ref[...].astype(F32), axis=-1, keepdims=True)

    spec = pl.BlockSpec((None, ts, dv), lambda h, i: (h, i, 0))
    return pl.pallas_call(
        body, name=name, out_shape=jax.ShapeDtypeStruct((H, S, 1), F32), grid=(H, S // ts),
        in_specs=[spec, spec], out_specs=pl.BlockSpec((None, ts, 1), lambda h, i: (h, i, 0)),
        compiler_params=_params("parallel", "parallel"),
    )(o, do)


def flash_bwd(q, k, v, do, lse_row, delta_row, scale, name, tq=1024, tk=512, rider=None):
    H, S, dk = q.shape
    dv = v.shape[-1]
    tq = _divisor(S, tq, LANES)
    tk = _divisor(S, tk, 16)
    nq = S // tq
    c = scale * LOG2E

    def body(q_ref, k_ref, v_ref, do_ref, lse_ref, dl_ref, dq_ref, dk_ref, dv_ref, dk_sc, dv_sc):
        j = pl.program_id(1)

        @pl.when(j == 0)
        def _():
            dq_ref[...] = jnp.zeros(dq_ref.shape, F32)

        dk_sc[...] = jnp.zeros(dk_sc.shape, F32)
        dv_sc[...] = jnp.zeros(dv_sc.shape, F32)
        kk = k_ref[...]
        vv = v_ref[...]

        def step(t, carry):
            off = pl.multiple_of(t * tq, tq)
            qq = q_ref[pl.ds(off, tq), :]
            dd = do_ref[pl.ds(off, tq), :]
            st = lax.dot_general(kk, qq, (((1,), (1,)), ((), ())), preferred_element_type=F32) * c
            pt = jnp.exp2(st - lse_ref[:, pl.ds(off, tq)])
            dpt = lax.dot_general(vv, dd, (((1,), (1,)), ((), ())), preferred_element_type=F32)
            dst = ((pt * (dpt - dl_ref[:, pl.ds(off, tq)])) * scale).astype(BF16)
            dv_sc[...] += jnp.dot(pt.astype(BF16), dd, preferred_element_type=F32)
            dk_sc[...] += jnp.dot(dst, qq, preferred_element_type=F32)
            dq_ref[pl.ds(off, tq), :] += lax.dot_general(dst, kk, (((0,), (0,)), ((), ())),
                                                         preferred_element_type=F32)
            return carry

        lax.fori_loop(0, nq, step, 0)
        dk_ref[...] = dk_sc[...].astype(dk_ref.dtype)
        dv_ref[...] = dv_sc[...].astype(dv_ref.dtype)

    ks = pl.BlockSpec((None, tk, dk), lambda h, j: (h, j, 0))
    vs = pl.BlockSpec((None, tk, dv), lambda h, j: (h, j, 0))
    row = pl.BlockSpec((None, 1, S), lambda h, j: (h, 0, 0))
    qs = pl.BlockSpec((None, S, dk), lambda h, j: (h, 0, 0))
    res = carried_call(
        body, rider, name, (H, S // tk),
        [qs, ks, vs, pl.BlockSpec((None, S, dv), lambda h, j: (h, 0, 0)), row, row], [qs, ks, vs],
        [jax.ShapeDtypeStruct((H, S, dk), F32), jax.ShapeDtypeStruct((H, S, dk), F32),
         jax.ShapeDtypeStruct((H, S, dv), BF16)],
        [pltpu.VMEM((tk, dk), F32), pltpu.VMEM((tk, dv), F32)], [q, k, v, do, lse_row, delta_row],
        ("parallel", "arbitrary"))
    return tuple(res[:3]) if rider is None else (res[0], res[1], res[2], res[3:])


def _rope_tables(S, reps):
    half = MLA_ROPE // 2
    pos = jnp.arange(S, dtype=F32)
    inv = ROPE_THETA ** (-jnp.arange(half, dtype=F32) / half)
    ang = pos[:, None] * inv[None, :]
    return jnp.tile(jnp.cos(ang), (1, reps)), jnp.tile(jnp.sin(ang), (1, reps))


def _rotate(x1, x2, cos, sin, out_dtype, name):
    return ew(lambda a, b, c, s: (a * c - b * s, b * c + a * s), name, [x1, x2, cos, sin], [], [out_dtype, out_dtype])


def mla_layer_fwd(x, norm_g, w_dqkv, q_norm, kv_norm, w_uq, w_ukv, w_o, tag, rider=None):
    S, D = x.shape
    QL, KL = q_norm.shape[1], kv_norm.shape[1]
    H = _layered(w_o)[0].shape[2] // MLA_V
    half = MLA_ROPE // 2
    h, rstd = rms_fwd(x, norm_g, BF16, f"mla_norm_{tag}")
    d = mm_nn(h, w_dqkv, F32, f"mla_down_{tag}")
    c_q, c_kv, k_rope = d[:, :QL], d[:, QL:QL + KL], d[:, QL + KL:]
    cq_n, rstd_q = rms_fwd(c_q, q_norm, BF16, f"mla_qnorm_{tag}")
    ckv_n, rstd_kv = rms_fwd(c_kv, kv_norm, BF16, f"mla_kvnorm_{tag}")
    q = mm_nn(cq_n, w_uq, F32, f"mla_uq_{tag}").reshape(S, H, MLA_NOPE + MLA_ROPE)
    kv = mm_nn(ckv_n, w_ukv, BF16, f"mla_ukv_{tag}").reshape(S, H, MLA_NOPE + MLA_V)
    cos, sin = _rope_tables(S, H + 1)
    x1 = jnp.concatenate([q[:, :, MLA_NOPE:MLA_NOPE + half].reshape(S, H * half), k_rope[:, :half]], axis=1)
    x2 = jnp.concatenate([q[:, :, MLA_NOPE + half:].reshape(S, H * half), k_rope[:, half:]], axis=1)
    r1, r2 = _rotate(x1, x2, cos, sin, BF16, f"mla_rope_{tag}")
    qr = jnp.concatenate([r1[:, :H * half].reshape(S, H, half), r2[:, :H * half].reshape(S, H, half)], axis=2)
    kr = jnp.concatenate([r1[:, H * half:], r2[:, H * half:]], axis=1)
    qh = jnp.concatenate([q[:, :, :MLA_NOPE].astype(BF16), qr], axis=2).transpose(1, 0, 2)
    kh = jnp.concatenate([kv[:, :, :MLA_NOPE], jnp.broadcast_to(kr[:, None, :], (S, H, MLA_ROPE))],
                         axis=2).transpose(1, 0, 2)
    vh = kv[:, :, MLA_NOPE:].transpose(1, 0, 2)
    scale = (MLA_NOPE + MLA_ROPE) ** -0.5
    oh, lse, *got = flash_fwd(qh, kh, vh, scale, f"mla_attn_{tag}", rider=rider)
    o2 = _heads_minor(oh)
    y = mm_nn_res(o2, w_o, x, f"mla_out_{tag}")
    saved = (x, h, rstd, c_q, rstd_q, cq_n, c_kv, rstd_kv, ckv_n, qh, kh, vh, oh, lse, o2, cos, sin)
    return (y, saved) if rider is None else (y, saved, got[0])


def mla_layer_bwd(dy, saved, norm_g, w_dqkv, q_norm, kv_norm, w_uq, w_ukv, w_o, tag, rider=None):
    x, h, rstd, c_q, rstd_q, cq_n, c_kv, rstd_kv, ckv_n, qh, kh, vh, oh, lse, o2, cos, sin = saved
    H, S, _ = qh.shape
    half = MLA_ROPE // 2
    scale = (MLA_NOPE + MLA_ROPE) ** -0.5
    dw_o = mm_tn(o2, dy, 1, BF16, f"mla_dwo_{tag}")
    do2 = mm_nt(dy, w_o, BF16, f"mla_do_{tag}")
    doh = _heads_major(do2, H)
    delta = flash_delta(oh, doh, f"mla_delta_{tag}")
    dqh, dkh, dvh, *got = flash_bwd(qh, kh, vh, doh, lse.reshape(H, 1, S), delta.reshape(H, 1, S), scale,
                                    f"mla_dattn_{tag}", rider=rider)
    dq = dqh.transpose(1, 0, 2)
    dk = dkh.transpose(1, 0, 2)
    dkr = jnp.sum(dk[:, :, MLA_NOPE:], axis=1)
    g1 = jnp.concatenate([dq[:, :, MLA_NOPE:MLA_NOPE + half].reshape(S, H * half), dkr[:, :half]], axis=1)
    g2 = jnp.concatenate([dq[:, :, MLA_NOPE + half:].reshape(S, H * half), dkr[:, half:]], axis=1)
    b1, b2 = _rotate(g1, g2, cos, -sin, F32, f"mla_drope_{tag}")
    dq_rope = jnp.concatenate([b1[:, :H * half].reshape(S, H, half), b2[:, :H * half].reshape(S, H, half)], axis=2)
    dk_rope = jnp.concatenate([b1[:, H * half:], b2[:, H * half:]], axis=1)
    dq_full = jnp.concatenate([dq[:, :, :MLA_NOPE], dq_rope], axis=2).reshape(S, -1).astype(BF16)
    dkv = jnp.concatenate([dk[:, :, :MLA_NOPE].astype(BF16), dvh.transpose(1, 0, 2)], axis=2).reshape(S, -1)
    dw_uq = mm_tn(cq_n, dq_full, _nblocks(w_uq), BF16, f"mla_dwuq_{tag}")
    dw_ukv = mm_tn(ckv_n, dkv, _nblocks(w_ukv), BF16, f"mla_dwukv_{tag}")
    dcq_n = mm_nt(dq_full, w_uq, F32, f"mla_dcq_{tag}")
    dckv_n = mm_nt(dkv, w_ukv, F32, f"mla_dckv_{tag}")
    dc_q, dq_norm = rms_bwd(dcq_n, c_q, rstd_q, q_norm, f"mla_dqnorm_{tag}")
    dc_kv, dkv_norm = rms_bwd(dckv_n, c_kv, rstd_kv, kv_norm, f"mla_dkvnorm_{tag}")
    dd = jnp.concatenate([dc_q, dc_kv, dk_rope], axis=1).astype(BF16)
    dw_dqkv = mm_tn(h, dd, 1, BF16, f"mla_dwdown_{tag}")
    dh_ = mm_nt(dd, w_dqkv, F32, f"mla_dh_{tag}")
    dx, dnorm = rms_bwd(dh_, x, rstd, norm_g, f"mla_dnorm_{tag}", dres=dy)
    grads = (dnorm, dw_dqkv, dq_norm, dkv_norm, dw_uq, dw_ukv, dw_o)
    return (dx, grads) if rider is None else (dx, grads, got[0])


SLAB = LANES
SLAB_GROUPS = SLAB // SSM_GROUP_CH
SLAB_HALF = SLAB_GROUPS * SSM_STATE
SLAB_W = 2 * SLAB_HALF


def _scan_rows(st_ref, carry_ref, lam_ref, nt, rev):
    h = SLAB_HALF
    lr = lam_ref[:, :h]
    li = lam_ref[:, h:]

    def step(i, c):
        xr, xi = c
        ii = (nt - 1 - i) if rev else i
        row = pl.multiple_of(ii * SEGS, SEGS)
        nr = lr * xr - li * xi + st_ref[pl.ds(row, SEGS), :h]
        ni = lr * xi + li * xr + st_ref[pl.ds(row, SEGS), h:]
        st_ref[pl.ds(row, SEGS), :h] = nr
        st_ref[pl.ds(row, SEGS), h:] = ni
        return nr, ni

    xr, xi = lax.fori_loop(0, nt, step, (carry_ref[:, :h], carry_ref[:, h:]), unroll=4)
    carry_ref[:, :h] = xr
    carry_ref[:, h:] = xi


def s5_scan(mode, inp, win, lam, rev, name, init=None, wout=None, xs=None, xinit=None, u=None, rows=512,
            rider=None):
    T, C = inp.shape
    K = win.shape[0]
    W = SLAB_W
    Tc = _divisor(T, rows, 16)
    nt = Tc // SEGS
    nT = T // Tc
    tiles = T // SEGS

    def chunk(jj):
        return (nT - 1 - jj) if rev else jj

    slab_in = pl.BlockSpec((Tc, SLAB), lambda k, jj: (chunk(jj), k))
    wspec = pl.BlockSpec((None, SLAB, W), lambda k, jj: (k, 0, 0))
    vspec = pl.BlockSpec((None, SEGS, W), lambda k, jj: (k, 0, 0))
    wospec = pl.BlockSpec((None, W, SLAB), lambda k, jj: (k, 0, 0))
    xspec = pl.BlockSpec((Tc, W), lambda k, jj: (chunk(jj), k))
    scratch = [pltpu.VMEM((Tc, W), F32), pltpu.VMEM((SEGS, W), F32)]
    sem = _params("parallel", "arbitrary")

    def project_in(in_ref, w_ref, st_ref):
        st_ref[...] = jnp.dot(in_ref[...].astype(BF16), w_ref[...], preferred_element_type=F32)

    if mode == "finals":
        def body(in_ref, w_ref, lam_ref, fin_ref, st_ref, carry_ref):
            jj = pl.program_id(1)

            @pl.when(jj == 0)
            def _():
                carry_ref[...] = jnp.zeros_like(carry_ref)

            project_in(in_ref, w_ref, st_ref)
            _scan_rows(st_ref, carry_ref, lam_ref, nt, rev)

            @pl.when(jj == nT - 1)
            def _():
                fin_ref[...] = carry_ref[...]

        return pl.pallas_call(
            body, name=name, out_shape=jax.ShapeDtypeStruct((K, SEGS, W), F32), grid=(K, nT),
            in_specs=[slab_in, wspec, vspec], out_specs=vspec, scratch_shapes=scratch, compiler_params=sem,
        )(inp, win, lam)

    if mode == "fwd":
        def body(in_ref, w_ref, lam_ref, init_ref, wo_ref, xs_ref, y_ref, st_ref, carry_ref):
            jj = pl.program_id(1)

            @pl.when(jj == 0)
            def _():
                carry_ref[...] = init_ref[...]

            project_in(in_ref, w_ref, st_ref)
            _scan_rows(st_ref, carry_ref, lam_ref, nt, rev)
            xs = st_ref[...]
            xs_ref[...] = xs
            y_ref[...] = jnp.dot(xs.astype(BF16), wo_ref[...], preferred_element_type=F32)

        res = carried_call(
            body, rider, name, (K, nT), [slab_in, wspec, vspec, vspec, wospec], [xspec, slab_in],
            [jax.ShapeDtypeStruct((T, K * W), F32), jax.ShapeDtypeStruct((T, C), F32)], scratch,
            [inp, win, lam, init, wout], ("parallel", "arbitrary"))
        return (res[0], res[1]) if rider is None else (res[0], res[1], res[2:])

    assert mode == "bwd"
    x_rev = not rev

    def halo_index(k, jj):
        ch = chunk(jj)
        tile = jnp.minimum((ch + 1) * nt, tiles - 1) if x_rev else jnp.maximum(ch * nt - 1, 0)
        return (tile, k)

    halo = pl.BlockSpec((SEGS, W), halo_index)

    def body(in_ref, w_ref, lam_ref, init_ref, wo_ref, xs_ref, xh_ref, xi_ref, u_ref,
             du_ref, dwin_ref, dwout_ref, dlam_ref, st_ref, carry_ref):
        jj = pl.program_id(1)
        ch = chunk(jj)

        @pl.when(jj == 0)
        def _():
            carry_ref[...] = init_ref[...]

        g = in_ref[...].astype(BF16)
        st_ref[...] = jnp.dot(g, w_ref[...], preferred_element_type=F32)
        _scan_rows(st_ref, carry_ref, lam_ref, nt, rev)
        adj = st_ref[...]
        adj16 = adj.astype(BF16)
        du_ref[...] = jnp.dot(adj16, wo_ref[...], preferred_element_type=F32)
        xs = xs_ref[...]
        edge = (ch == nT - 1) if x_rev else (ch == 0)
        first = jnp.where(edge, xi_ref[...], xh_ref[...])
        if x_rev:
            xp = jnp.concatenate([xs[SEGS:], first], axis=0)
        else:
            xp = jnp.concatenate([first, xs[:Tc - SEGS]], axis=0)
        h = SLAB_HALF
        ar, ai, pr, pi = adj[:, :h], adj[:, h:], xp[:, :h], xp[:, h:]
        dlr = (ar * pr + ai * pi).reshape(nt, SEGS, h).sum(axis=0)
        dli = (ai * pr - ar * pi).reshape(nt, SEGS, h).sum(axis=0)
        dwin = lax.dot_general(u_ref[...].astype(BF16), adj16, (((0,), (0,)), ((), ())), preferred_element_type=F32)
        dwout = lax.dot_general(xs.astype(BF16), g, (((0,), (0,)), ((), ())), preferred_element_type=F32)

        @pl.when(jj == 0)
        def _():
            dwin_ref[...] = dwin
            dwout_ref[...] = dwout
            dlam_ref[:, :h] = dlr
            dlam_ref[:, h:] = dli

        @pl.when(jj > 0)
        def _():
            dwin_ref[...] += dwin
            dwout_ref[...] += dwout
            dlam_ref[:, :h] += dlr
            dlam_ref[:, h:] += dli

    res = carried_call(
        body, rider, name, (K, nT), [slab_in, wspec, vspec, vspec, wospec, xspec, halo, vspec, slab_in],
        [slab_in, wspec, wospec, vspec],
        [jax.ShapeDtypeStruct((T, C), F32), jax.ShapeDtypeStruct((K, SLAB, W), F32),
         jax.ShapeDtypeStruct((K, W, SLAB), F32), jax.ShapeDtypeStruct((K, SEGS, W), F32)], scratch,
        [inp, win, lam, init, wout, xs, xs, xinit, u], ("parallel", "arbitrary"))
    return tuple(res[:4]) if rider is None else (res[0], res[1], res[2], res[3], res[4:])


def _s5_discretize(a_re, a_im, log_step, b_re, b_im):
    step = jnp.exp(log_step)[:, None]
    mag = jnp.exp(step * a_re)
    lb_re = mag * jnp.cos(step * a_im)
    lb_im = mag * jnp.sin(step * a_im)
    n_re, n_im = lb_re - 1.0, lb_im
    den = a_re * a_re + a_im * a_im
    coef_re = (n_re * a_re + n_im * a_im) / den
    coef_im = (n_im * a_re - n_re * a_im) / den
    bb_re = coef_re[..., None] * b_re - coef_im[..., None] * b_im
    bb_im = coef_re[..., None] * b_im + coef_im[..., None] * b_re
    return lb_re, lb_im, bb_re, bb_im


def _slab_in_matrix(bb_re, bb_im):
    G, N, Cg = bb_re.shape
    K = G // SLAB_GROUPS
    eye = jnp.eye(SLAB_GROUPS, dtype=F32)
    parts = [jnp.einsum('kgnc,gh->kgchn', b.reshape(K, SLAB_GROUPS, N, Cg), eye).reshape(K, SLAB, SLAB_HALF)
             for b in (bb_re, bb_im)]
    return jnp.concatenate(parts, axis=2)


def _slab_in_unpack(m):
    K = m.shape[0]
    m6 = m.reshape(K, SLAB_GROUPS, SSM_GROUP_CH, 2, SLAB_GROUPS, SSM_STATE)
    d = jnp.einsum('kgcphn,gh->pkgnc', m6, jnp.eye(SLAB_GROUPS, dtype=F32))
    d = d.reshape(2, K * SLAB_GROUPS, SSM_STATE, SSM_GROUP_CH)
    return d[0], d[1]


def _slab_out_matrix(c_re, c_im):
    G, Cg, N = c_re.shape
    K = G // SLAB_GROUPS
    eye = jnp.eye(SLAB_GROUPS, dtype=F32)
    parts = [jnp.einsum('kgcn,gh->kgnhc', c.reshape(K, SLAB_GROUPS, Cg, N), eye).reshape(K, SLAB_HALF, SLAB)
             for c in (c_re, -c_im)]
    return jnp.concatenate(parts, axis=1)


def _slab_out_unpack(m):
    K = m.shape[0]
    m6 = m.reshape(K, 2, SLAB_GROUPS, SSM_STATE, SLAB_GROUPS, SSM_GROUP_CH)
    d = jnp.einsum('kpgnhc,gh->pkgcn', m6, jnp.eye(SLAB_GROUPS, dtype=F32))
    d = d.reshape(2, K * SLAB_GROUPS, SSM_GROUP_CH, SSM_STATE)
    return d[0], -d[1]


def _slab_vec(re, im):
    K = re.shape[0] // SLAB_GROUPS
    v = jnp.concatenate([re.reshape(K, SLAB_HALF), im.reshape(K, SLAB_HALF)], axis=1)
    return jnp.broadcast_to(v[:, None, :], (K, SEGS, SLAB_W))


def _segment_inits(fin, lam, seg_len, rev):
    h = SLAB_HALF
    pr, pi = lam[:, 0, :h], lam[:, 0, h:]
    steps = int(round(math.log2(seg_len)))
    assert 2 ** steps == seg_len
    for _ in range(steps):
        pr, pi = pr * pr - pi * pi, 2.0 * pr * pi
    cr = jnp.zeros_like(pr)
    ci = jnp.zeros_like(pi)
    inits = [None] * SEGS
    for s in (range(SEGS - 1, -1, -1) if rev else range(SEGS)):
        inits[s] = jnp.concatenate([cr, ci], axis=1)
        cr, ci = pr * cr - pi * ci + fin[:, s, :h], pr * ci + pi * cr + fin[:, s, h:]
    return jnp.stack(inits, axis=1)


def _time_permute(x):
    T, C = x.shape
    return x.reshape(SEGS, T // SEGS, C).transpose(1, 0, 2).reshape(T, C)


def _time_unpermute(x):
    T, C = x.shape
    return x.reshape(T // SEGS, SEGS, C).transpose(1, 0, 2).reshape(T, C)


_GELU_K = math.sqrt(2.0 / math.pi)
_GELU_A = 0.044715


def _gelu_grad(y):
    t = jnp.tanh(_GELU_K * (y + _GELU_A * y * y * y))
    return 0.5 * (1.0 + t) + 0.5 * y * (1.0 - t * t) * (_GELU_K * (1.0 + 3.0 * _GELU_A * y * y))


def _conj(lam):
    return jnp.concatenate([lam[:, :, :SLAB_HALF], -lam[:, :, SLAB_HALF:]], axis=2)


def s5_layer_fwd(x, norm_g, ssm, w_glu, tag, rider=None):
    S, D = x.shape
    u_nat, rstd = rms_fwd(x, norm_g, F32, f"s5_norm_{tag}")
    u = _time_permute(u_nat)
    dirs = []
    ys = []
    received = None
    for dr in range(2):
        rev = dr == 1
        lb_re, lb_im, bb_re, bb_im = _s5_discretize(ssm["a_re"][dr], ssm["a_im"][dr], ssm["log_step"][dr],
                                                    ssm["b_re"][dr], ssm["b_im"][dr])
        win = _slab_in_matrix(bb_re, bb_im).astype(BF16)
        wout = _slab_out_matrix(ssm["c_re"][dr], ssm["c_im"][dr]).astype(BF16)
        lam = _slab_vec(lb_re, lb_im)
        fin = s5_scan("finals", u, win, lam, rev, f"s5_fin{dr}_{tag}")
        init = _segment_inits(fin, lam, S // SEGS, rev)
        xs, y, *got = s5_scan("fwd", u, win, lam, rev, f"s5_fwd{dr}_{tag}", init=init, wout=wout,
                              rider=rider if dr == 0 else None)
        received = got[0] if got else received
        dirs.append((win, wout, lam, init, xs))
        ys.append(y)
    yy, zb = ew(lambda uu, a, b, d: (d * uu + a + b, jax.nn.gelu(d * uu + a + b)), f"s5_y_{tag}",
                [u, ys[0], ys[1]], [ssm["d"]], [F32, BF16])
    lin = mm_nn(zb, w_glu, F32, f"s5_glu_{tag}", tn_cap=512)
    mix = ew(lambda y_, l_, b: jax.nn.gelu(y_) * jax.nn.sigmoid(l_ + b), f"s5_mix_{tag}",
             [yy, lin], [ssm["b_glu"]], [F32])[0]
    out = x + _time_unpermute(mix)
    saved = (x, rstd, u, dirs, yy, zb, lin)
    return (out, saved) if rider is None else (out, saved, received)


def s5_layer_bwd(dy, saved, norm_g, ssm, w_glu, tag, rider=None):
    x, rstd, u, dirs, yy, zb, lin = saved
    S, D = x.shape
    dmix = _time_permute(dy)

    def glu_back(dm, y_, l_, b):
        z = jax.nn.gelu(y_)
        sg = jax.nn.sigmoid(l_ + b)
        dlin = dm * z * (sg * (1.0 - sg))
        return dlin, dm * sg, dlin

    dlin, dz_direct, db_glu = ew(glu_back, f"s5_dmix_{tag}", [dmix, yy, lin], [ssm["b_glu"]], [BF16, F32], n_sums=1)
    dw_glu = mm_tn(zb, dlin, 1, BF16, f"s5_dwglu_{tag}", tn_cap=512)
    dz_mm = mm_nt(dlin, w_glu, F32, f"s5_dz_{tag}")

    def gelu_back(a, b, y_, uu):
        dyy = (a + b) * _gelu_grad(y_)
        return dyy, dyy * uu

    dyy, dd = ew(gelu_back, f"s5_dy_{tag}", [dz_direct, dz_mm, yy, u], [], [F32], n_sums=1)
    grads = {"d": dd, "b_glu": db_glu, "w_glu": dw_glu}
    dus = []
    per_dir = []
    received = None
    for dr in range(2):
        rev = dr == 1
        win, wout, lam, xinit, xs = dirs[dr]
        lamc = _conj(lam)
        ein = wout.transpose(0, 2, 1)
        eout = win.transpose(0, 2, 1)
        fin = s5_scan("finals", dyy, ein, lamc, not rev, f"s5_bfin{dr}_{tag}")
        init = _segment_inits(fin, lamc, S // SEGS, not rev)
        du, dwin, dwout, dlam, *got = s5_scan("bwd", dyy, ein, lamc, not rev, f"s5_bwd{dr}_{tag}", init=init,
                                              wout=eout, xs=xs, xinit=xinit, u=u, rider=rider if dr == 0 else None)
        received = got[0] if got else received
        dus.append(du)
        dbb_re, dbb_im = _slab_in_unpack(dwin)
        dc_re, dc_im = _slab_out_unpack(dwout)
        dl = dlam.sum(axis=1)
        dlb_re = dl[:, :SLAB_HALF].reshape(-1, SSM_STATE)
        dlb_im = dl[:, SLAB_HALF:].reshape(-1, SSM_STATE)
        prm = (ssm["a_re"][dr], ssm["a_im"][dr], ssm["log_step"][dr], ssm["b_re"][dr], ssm["b_im"][dr])
        _, vjp = jax.vjp(_s5_discretize, *prm)
        per_dir.append(vjp((dlb_re, dlb_im, dbb_re, dbb_im)) + (dc_re, dc_im))
    for i, nm in enumerate(["a_re", "a_im", "log_step", "b_re", "b_im", "c_re", "c_im"]):
        grads[nm] = jnp.stack([per_dir[0][i], per_dir[1][i]], axis=0)
    du_p = ew(lambda g, a, b, d: d * g + a + b, f"s5_du_{tag}", [dyy, dus[0], dus[1]], [ssm["d"]], [F32])[0]
    dx, dnorm = rms_bwd(_time_unpermute(du_p), x, rstd, norm_g, f"s5_dnorm_{tag}", dres=dy)
    grads["norm"] = dnorm
    return (dx, grads) if rider is None else (dx, grads, received)


def final_loss(x, g, target, name, ts=512):
    S, D = x.shape
    ts = _divisor(S, ts, 16)

    def body(x_ref, g_ref, t_ref, loss_ref, dx_ref, dg_ref):
        i = pl.program_id(0)
        x = x_ref[...]
        gg = g_ref[...]
        r = lax.rsqrt(jnp.mean(x * x, axis=-1, keepdims=True) + RMS_EPS)
        xhat = x * r
        err = xhat * gg - t_ref[...]
        row_loss = jnp.mean(err * err, axis=-1, keepdims=True)
        part = jnp.broadcast_to(0.5 * jnp.sum(row_loss, axis=0, keepdims=True), (1, LANES))
        dy = err * (1.0 / D)
        dhg = dy * gg
        c = jnp.mean(dhg * xhat, axis=-1, keepdims=True)
        dx_ref[...] = r * (dhg - xhat * c)
        dg = jnp.sum(dy * xhat, axis=0, keepdims=True)

        @pl.when(i == 0)
        def _():
            loss_ref[...] = part
            dg_ref[...] = dg

        @pl.when(i > 0)
        def _():
            loss_ref[...] += part
            dg_ref[...] += dg

    row = pl.BlockSpec((ts, D), lambda i: (i, 0))
    vec = pl.BlockSpec((1, D), lambda i: (0, 0))
    return pl.pallas_call(
        body, name=name,
        out_shape=(jax.ShapeDtypeStruct((1, LANES), F32), jax.ShapeDtypeStruct((S, D), F32),
                   jax.ShapeDtypeStruct((1, D), F32)),
        grid=(S // ts,), in_specs=[row, vec, row],
        out_specs=(pl.BlockSpec((1, LANES), lambda i: (0, 0)), row, vec),
        compiler_params=_params("arbitrary"),
    )(x, g, target)


FLAT_W = 8 * LANES


def _adamw_math(w, g, m, v):
    m = ADAM_B1 * m + (1.0 - ADAM_B1) * g
    v = ADAM_B2 * v + (1.0 - ADAM_B2) * (g * g)
    m_hat = m / (1.0 - ADAM_B1 ** ADAM_STEP)
    v_hat = v / (1.0 - ADAM_B2 ** ADAM_STEP)
    delta = -ADAM_LR * (m_hat / (jnp.sqrt(v_hat) + ADAM_EPS) + ADAM_WD * w)
    return delta, m, v


def _ordered_sum(parts_ref):
    total = parts_ref[0].astype(F32)
    for s in range(1, N_DEV):
        total = total + parts_ref[s].astype(F32)
    return total


ADAMW_BLOCK_ELEMS = 256 * 1024


def adamw_from_parts(parts, w, m, v, name):
    R, B = w.shape
    tr = _divisor(R, max(16, ADAMW_BLOCK_ELEMS // B), 16)

    def body(p_ref, w_ref, m_ref, v_ref, g_ref, d_ref, nm_ref, nv_ref):
        g = _ordered_sum(p_ref)
        delta, nm, nv = _adamw_math(w_ref[...], g, m_ref[...], v_ref[...])
        g_ref[...] = g
        d_ref[...] = delta
        nm_ref[...] = nm
        nv_ref[...] = nv

    flat = pl.BlockSpec((tr, B), lambda i: (i, 0))
    out = jax.ShapeDtypeStruct((R, B), F32)
    return pl.pallas_call(
        body, name=name, out_shape=(out, out, out, out), grid=(R // tr,),
        in_specs=[pl.BlockSpec((N_DEV, tr, B), lambda i: (0, i, 0)), flat, flat, flat],
        out_specs=(flat, flat, flat, flat), compiler_params=_params("parallel"),
    )(parts, w, m, v)


def sum_parts(parts, name, tr=512):
    R = parts.shape[1]
    tr = _divisor(R, tr, 16)

    def body(p_ref, o_ref):
        o_ref[...] = _ordered_sum(p_ref)

    return pl.pallas_call(
        body, name=name, out_shape=jax.ShapeDtypeStruct((R, FLAT_W), F32), grid=(R // tr,),
        in_specs=[pl.BlockSpec((N_DEV, tr, FLAT_W), lambda i: (0, i, 0))],
        out_specs=pl.BlockSpec((tr, FLAT_W), lambda i: (i, 0)), compiler_params=_params("parallel"),
    )(parts)


def adamw_flat(g, w, m, v, name, tr=512):
    R = w.shape[0]
    tr = _divisor(R, tr, 16)

    def body(g_ref, w_ref, m_ref, v_ref, d_ref, nm_ref, nv_ref):
        delta, nm, nv = _adamw_math(w_ref[...], g_ref[...], m_ref[...], v_ref[...])
        d_ref[...] = delta
        nm_ref[...] = nm
        nv_ref[...] = nv

    flat = pl.BlockSpec((tr, FLAT_W), lambda i: (i, 0))
    out = jax.ShapeDtypeStruct((R, FLAT_W), F32)
    return pl.pallas_call(
        body, name=name, out_shape=(out, out, out), grid=(R // tr,),
        in_specs=[flat, flat, flat, flat], out_specs=(flat, flat, flat), compiler_params=_params("parallel"),
    )(g, w, m, v)


MESH_ID = pl.DeviceIdType.MESH
HBM_SPEC = pl.BlockSpec(memory_space=pltpu.HBM)


def _position():
    x, y, c = lax.axis_index("x"), lax.axis_index("y"), lax.axis_index("c")
    return x, y, c


def _flat_index(px, py, pc):
    return 4 * px + 2 * py + pc


def all_gather(arrays, axes, name):
    n = len(arrays)

    def body(*refs):
        ins, outs = refs[:n], refs[n:2 * n]
        send_sems, recv_sems, local_sems = refs[2 * n:]
        x, y, c = _position()
        me, sibling = (x, y, c), (x, y, 1 - c)
        chips = [(1 - x, y), (x, 1 - y), (1 - x, 1 - y)]

        def block_of(a, pos):
            idx = _flat_index(*pos)
            return outs[a].at[:, idx] if axes[a] == 1 else outs[a].at[idx]

        def copy(a, k, block, to, src=None):
            rows = block_of(a, block)
            return pltpu.make_async_remote_copy(
                src_ref=rows if src is None else src, dst_ref=rows,
                send_sem=send_sems.at[7 * a + k], recv_sem=recv_sems.at[7 * a + k],
                device_id=to, device_id_type=MESH_ID)

        mine, first, passed = [], [], []
        for a in range(n):
            cp = pltpu.make_async_copy(ins[a], block_of(a, me), local_sems.at[a])
            cp.start()
            mine.append(cp)
            first.append(copy(a, 0, me, sibling, src=ins[a]))
            first += [copy(a, 1 + j, me, (*chip, c), src=ins[a]) for j, chip in enumerate(chips)]
        for cp in first:
            cp.start()
        for a in range(n):
            for j, chip in enumerate(chips):
                copy(a, 1 + j, (*chip, c), me).wait_recv()
                fwd = copy(a, 4 + j, (*chip, c), sibling)
                fwd.start()
                passed.append(fwd)
        for a in range(n):
            copy(a, 0, sibling, me).wait_recv()
            for j, chip in enumerate(chips):
                copy(a, 4 + j, (*chip, 1 - c), me).wait_recv()
        for cp in first + passed:
            cp.wait_send()
        for cp in mine:
            cp.wait()

    return pl.pallas_call(
        body, name=name,
        out_shape=tuple(jax.ShapeDtypeStruct(a.shape[:ax] + (N_DEV,) + a.shape[ax:], a.dtype)
                        for a, ax in zip(arrays, axes)),
        in_specs=[HBM_SPEC] * n, out_specs=tuple([HBM_SPEC] * n),
        scratch_shapes=[pltpu.SemaphoreType.DMA((7 * n,)), pltpu.SemaphoreType.DMA((7 * n,)),
                        pltpu.SemaphoreType.DMA((n,))],
    )(*arrays)


def exchange(slotted, whole, name):
    n = len(slotted) + 1

    def body(*refs):
        srcs, dsts = refs[:n], refs[n:2 * n]
        send_sems, recv_sems, local_sems = refs[2 * n:]
        x, y, c = _position()
        me = _flat_index(x, y, c)

        def source(a, slot):
            return srcs[a].at[slot] if a < n - 1 else srcs[a]

        own = [pltpu.make_async_copy(source(a, me), dsts[a].at[me], local_sems.at[a]) for a in range(n)]
        for cp in own:
            cp.start()
        sends, recvs = [], []
        for r in range(1, N_DEV):
            peer = (1 - x if r & 4 else x, 1 - y if r & 2 else y, 1 - c if r & 1 else c)
            pidx = _flat_index(*peer)
            for a in range(n):
                k = 7 * a + r - 1
                sends.append(pltpu.make_async_remote_copy(
                    src_ref=source(a, pidx), dst_ref=dsts[a].at[me], send_sem=send_sems.at[k],
                    recv_sem=recv_sems.at[k], device_id=peer, device_id_type=MESH_ID))
                recvs.append(pltpu.make_async_remote_copy(
                    src_ref=source(a, pidx), dst_ref=dsts[a].at[pidx], send_sem=send_sems.at[k],
                    recv_sem=recv_sems.at[k], device_id=peer, device_id_type=MESH_ID))
        for cp in sends:
            cp.start()
        for cp in recvs:
            cp.wait_recv()
        for cp in sends:
            cp.wait_send()
        for cp in own:
            cp.wait()

    return pl.pallas_call(
        body, name=name,
        out_shape=tuple(jax.ShapeDtypeStruct(s.shape, s.dtype) for s in slotted)
        + (jax.ShapeDtypeStruct((N_DEV,) + whole.shape, whole.dtype),),
        in_specs=[HBM_SPEC] * n, out_specs=tuple([HBM_SPEC] * n),
        scratch_shapes=[pltpu.SemaphoreType.DMA((7 * n,)), pltpu.SemaphoreType.DMA((7 * n,)),
                        pltpu.SemaphoreType.DMA((n,))],
    )(*slotted, whole)


class RidingGather:
    def __init__(self, arrays):
        self.arrays = list(arrays)
        self.n = len(self.arrays)
        self.out_shapes = [jax.ShapeDtypeStruct((N_DEV,) + a.shape, a.dtype) for a in self.arrays]
        self.sems = [pltpu.SemaphoreType.DMA((7 * self.n,)), pltpu.SemaphoreType.DMA((7 * self.n,)),
                     pltpu.SemaphoreType.DMA((self.n,))]

    def _copies(self, ins, outs, sems):
        send_sems, recv_sems, local_sems = sems
        x, y, c = _position()
        me, sibling = (x, y, c), (x, y, 1 - c)
        chips = [(1 - x, y), (x, 1 - y), (1 - x, 1 - y)]

        def copy(a, k, block, to, src=None):
            rows = outs[a].at[_flat_index(*block)]
            return pltpu.make_async_remote_copy(
                src_ref=rows if src is None else src, dst_ref=rows,
                send_sem=send_sems.at[7 * a + k], recv_sem=recv_sems.at[7 * a + k],
                device_id=to, device_id_type=MESH_ID)

        mine = [pltpu.make_async_copy(ins[a], outs[a].at[_flat_index(*me)], local_sems.at[a]) for a in range(self.n)]
        first = []
        for a in range(self.n):
            first.append(copy(a, 0, me, sibling, src=ins[a]))
            first += [copy(a, 1 + j, me, (*chip, c), src=ins[a]) for j, chip in enumerate(chips)]
        return copy, mine, first, me, sibling, chips, c

    def start(self, ins, outs, sems):
        _, mine, first, *_ = self._copies(ins, outs, sems)
        for cp in mine + first:
            cp.start()

    def finish(self, ins, outs, sems):
        copy, mine, first, me, sibling, chips, c = self._copies(ins, outs, sems)
        passed = []
        for a in range(self.n):
            for j, chip in enumerate(chips):
                copy(a, 1 + j, (*chip, c), me).wait_recv()
                fwd = copy(a, 4 + j, (*chip, c), sibling)
                fwd.start()
                passed.append(fwd)
        for a in range(self.n):
            copy(a, 0, sibling, me).wait_recv()
            for j, chip in enumerate(chips):
                copy(a, 4 + j, (*chip, 1 - c), me).wait_recv()
        for cp in first + passed:
            cp.wait_send()
        for cp in mine:
            cp.wait()


class RidingExchange:
    def __init__(self, arrays):
        self.arrays = list(arrays)
        self.n = len(self.arrays)
        self.out_shapes = [jax.ShapeDtypeStruct(a.shape, a.dtype) for a in self.arrays]
        self.sems = [pltpu.SemaphoreType.DMA((7 * self.n,)), pltpu.SemaphoreType.DMA((7 * self.n,)),
                     pltpu.SemaphoreType.DMA((self.n,))]

    def _copies(self, ins, outs, sems):
        send_sems, recv_sems, local_sems = sems
        x, y, c = _position()
        me = _flat_index(x, y, c)
        own = [pltpu.make_async_copy(ins[a].at[me], outs[a].at[me], local_sems.at[a]) for a in range(self.n)]
        sends, recvs = [], []
        for r in range(1, N_DEV):
            peer = (1 - x if r & 4 else x, 1 - y if r & 2 else y, 1 - c if r & 1 else c)
            pidx = _flat_index(*peer)
            for a in range(self.n):
                k = 7 * a + r - 1
                sends.append(pltpu.make_async_remote_copy(
                    src_ref=ins[a].at[pidx], dst_ref=outs[a].at[me], send_sem=send_sems.at[k],
                    recv_sem=recv_sems.at[k], device_id=peer, device_id_type=MESH_ID))
                recvs.append(pltpu.make_async_remote_copy(
                    src_ref=ins[a].at[pidx], dst_ref=outs[a].at[pidx], send_sem=send_sems.at[k],
                    recv_sem=recv_sems.at[k], device_id=peer, device_id_type=MESH_ID))
        return own, sends, recvs

    def start(self, ins, outs, sems):
        own, sends, _ = self._copies(ins, outs, sems)
        for cp in own + sends:
            cp.start()

    def finish(self, ins, outs, sems):
        own, sends, recvs = self._copies(ins, outs, sems)
        for cp in recvs:
            cp.wait_recv()
        for cp in sends:
            cp.wait_send()
        for cp in own:
            cp.wait()


def carried_call(body, rider, name, grid, in_specs, out_specs, out_shape, scratch_shapes, args, semantics):
    in_specs, out_specs, out_shape = list(in_specs), list(out_specs), list(out_shape)
    scratch_shapes, args = list(scratch_shapes), list(args)
    if rider is None:
        return pl.pallas_call(body, name=name, grid=grid, in_specs=in_specs, out_specs=tuple(out_specs),
                              out_shape=tuple(out_shape), scratch_shapes=scratch_shapes,
                              compiler_params=_params(*semantics))(*args)
    n_in, n_out, n_scr, n = len(in_specs), len(out_specs), len(scratch_shapes), rider.n

    def riding(*refs):
        ins, srcs = refs[:n_in], refs[n_in:n_in + n]
        outs, dsts = refs[n_in + n:n_in + n + n_out], refs[n_in + n + n_out:n_in + 2 * n + n_out]
        scr = refs[n_in + 2 * n + n_out:]
        scratch, sems = scr[:n_scr], scr[n_scr:]
        first = functools.reduce(jnp.logical_and, [pl.program_id(d) == 0 for d in range(len(grid))])
        last = functools.reduce(jnp.logical_and, [pl.program_id(d) == g - 1 for d, g in enumerate(grid)])

        @pl.when(first)
        def _():
            rider.start(srcs, dsts, sems)

        body(*ins, *outs, *scratch)

        @pl.when(last)
        def _():
            rider.finish(srcs, dsts, sems)

    return pl.pallas_call(
        riding, name=name, grid=grid, in_specs=in_specs + [HBM_SPEC] * n,
        out_specs=tuple(out_specs + [HBM_SPEC] * n), out_shape=tuple(out_shape + rider.out_shapes),
        scratch_shapes=scratch_shapes + rider.sems,
        compiler_params=_params(*["arbitrary"] * len(grid)))(*args, *rider.arrays)


WEIGHTS = ['mix_norm', 'ffn_norm', 'final_norm', 'attn_w_qkv', 'attn_w_o', 'attn_sink', 'ssm_a_re', 'ssm_a_im',
           'ssm_log_step', 'ssm_b_re', 'ssm_b_im', 'ssm_c_re', 'ssm_c_im', 'ssm_d', 'ssm_w_glu', 'ssm_b_glu',
           'mla_w_dqkv', 'mla_q_norm', 'mla_kv_norm', 'mla_w_uq', 'mla_w_ukv', 'mla_w_o', 'ffn_w_up', 'ffn_conv_w',
           'ffn_conv_b', 'ffn_w_down']
BIG = [('attn_w_qkv', 'col'), ('attn_w_o', 'row'), ('ssm_w_glu', 'row'), ('mla_w_dqkv', 'row'), ('mla_w_uq', 'col'),
       ('mla_w_ukv', 'col'), ('mla_w_o', 'row'), ('ffn_w_up', 'col'), ('ffn_w_down', 'row')]
BIG_NAMES = [n for n, _ in BIG]
MIXER_WEIGHTS = {0: ['attn_w_qkv', 'attn_w_o'], 1: ['ssm_w_glu'],
                 2: ['mla_w_dqkv', 'mla_w_uq', 'mla_w_ukv', 'mla_w_o']}
FFN_WEIGHTS = ['ffn_w_up', 'ffn_w_down']
SMALL_SHARDED = ['mla_q_norm', 'mla_kv_norm', 'ffn_conv_w']
SMALL = [n for n in WEIGHTS if n not in BIG_NAMES]


def _pack(arrays, dtype):
    flat = jnp.concatenate([a.astype(dtype).reshape(-1) for a in arrays])
    pad = (-flat.shape[0]) % (16 * FLAT_W)
    if pad:
        flat = jnp.concatenate([flat, jnp.zeros((pad,), dtype)])
    return flat.reshape(-1, FLAT_W)


def _unpack(flat, shapes):
    flat = flat.reshape(-1)
    out, off = [], 0
    for s in shapes:
        n = int(np.prod(s))
        out.append(flat[off:off + n].reshape(s))
        off += n
    return out


def _own_slice(full, idx):
    n = full.shape[-1] // N_DEV
    return lax.dynamic_slice_in_dim(full, idx * n, n, axis=full.ndim - 1)


def kernel(x, mix_norm, ffn_norm, final_norm, attn_w_qkv, attn_w_o, attn_sink, ssm_a_re, ssm_a_im, ssm_log_step, ssm_b_re, ssm_b_im, ssm_c_re, ssm_c_im, ssm_d, ssm_w_glu, ssm_b_glu, mla_w_dqkv, mla_q_norm, mla_kv_norm, mla_w_uq, mla_w_ukv, mla_w_o, ffn_w_up, ffn_conv_w, ffn_conv_b, ffn_w_down, loss_target, m_mix_norm, m_ffn_norm, m_final_norm, m_attn_w_qkv, m_attn_w_o, m_attn_sink, m_ssm_a_re, m_ssm_a_im, m_ssm_log_step, m_ssm_b_re, m_ssm_b_im, m_ssm_c_re, m_ssm_c_im, m_ssm_d, m_ssm_w_glu, m_ssm_b_glu, m_mla_w_dqkv, m_mla_q_norm, m_mla_kv_norm, m_mla_w_uq, m_mla_w_ukv, m_mla_w_o, m_ffn_w_up, m_ffn_conv_w, m_ffn_conv_b, m_ffn_w_down, v_mix_norm, v_ffn_norm, v_final_norm, v_attn_w_qkv, v_attn_w_o, v_attn_sink, v_ssm_a_re, v_ssm_a_im, v_ssm_log_step, v_ssm_b_re, v_ssm_b_im, v_ssm_c_re, v_ssm_c_im, v_ssm_d, v_ssm_w_glu, v_ssm_b_glu, v_mla_w_dqkv, v_mla_q_norm, v_mla_kv_norm, v_mla_w_uq, v_mla_w_ukv, v_mla_w_o, v_ffn_w_up, v_ffn_conv_w, v_ffn_conv_b, v_ffn_w_down):
    given = dict(locals())
    idx = _flat_index(*_position())
    depth = mix_norm.shape[0]
    xs = x[0]
    S, D = xs.shape

    kinds = dict(BIG)

    def layer_weights(i):
        return [(n, i // 3) for n in MIXER_WEIGHTS[i % 3]] + [(n, i) for n in FFN_WEIGHTS]

    def shard(name, j):
        return given[name][j].astype(BF16)

    def whole_weight(name, g):
        a, b = g.shape[1:]
        if kinds[name] == 'row':
            return (g.reshape(1, 1, N_DEV * a, b), 0)
        if b % LANES:
            return (g.transpose(1, 0, 2).reshape(1, 1, a, N_DEV * b), 0)
        return (g[None], 0)

    small_flat = _pack([given[n] for n in SMALL_SHARDED], F32)
    first = layer_weights(0)
    gathered = all_gather([shard(n, j) for n, j in first] + [small_flat], [0] * (len(first) + 1), "gather_l0")
    W = {0: {n: whole_weight(n, g) for (n, _), g in zip(first, gathered)}}
    sm = _unpack_rows(gathered[-1], [given[n].shape for n in SMALL_SHARDED])
    q_norm_full = sm[0][:, 0].reshape(1, -1)
    kv_norm_full = sm[1][:, 0].reshape(1, -1)
    conv_w_full = sm[2].transpose(1, 2, 0, 3).reshape(depth, 3, -1)
    F = conv_w_full.shape[2] // 2

    def conv_params(i):
        cw = conv_w_full[i].reshape(3, 2, F).transpose(1, 0, 2)
        cb = ffn_conv_b[i].reshape(2, 1, F)
        return cw, cb

    ssm = lambda j: {"a_re": ssm_a_re[j], "a_im": ssm_a_im[j], "log_step": ssm_log_step[j], "b_re": ssm_b_re[j],
                     "b_im": ssm_b_im[j], "c_re": ssm_c_re[j], "c_im": ssm_c_im[j], "d": ssm_d[j][None],
                     "b_glu": ssm_b_glu[j][None]}

    def mixer_fwd(i, cur):
        kind, j, tag, w = i % 3, i // 3, f"l{i}", W[i]
        if kind == 0:
            return swa_layer_fwd(cur, mix_norm[i][None], w['attn_w_qkv'], w['attn_w_o'], attn_sink[j], tag)
        if kind == 1:
            return s5_layer_fwd(cur, mix_norm[i][None], ssm(j), w['ssm_w_glu'], tag)
        return mla_layer_fwd(cur, mix_norm[i][None], w['mla_w_dqkv'], q_norm_full, kv_norm_full, w['mla_w_uq'],
                             w['mla_w_ukv'], w['mla_w_o'], tag)

    cur = xs
    saved = []
    for i in range(depth):
        cur, sv = mixer_fwd(i, cur)
        cw, cb = conv_params(i)
        nxt = layer_weights(i + 1) if i + 1 < depth else []
        rider = RidingGather([shard(n, j) for n, j in nxt]) if nxt else None
        res = ffn_fwd(cur, ffn_norm[i][None], W[i]['ffn_w_up'], cw, cb, W[i]['ffn_w_down'], f"l{i}", rider=rider)
        cur, fsv = res[0], res[1]
        if nxt:
            W[i + 1] = {n: whole_weight(n, g) for (n, _), g in zip(nxt, res[2])}
        saved.append((sv, fsv))
    loss_part, dcur, dfinal = final_loss(cur, final_norm[None], loss_target[0], "loss_head")

    recv = {n: [None] * given[n].shape[0] for n in BIG_NAMES}
    gs = {n: [None] * given[n].shape[0] for n in SMALL if given[n].ndim > 1}
    gs['final_norm'] = dfinal[0]

    def blocked(g, name):
        a, b = given[name].shape[1:]
        if kinds[name] == 'row':
            return g.reshape(N_DEV, a, b)
        if b % LANES:
            return g.reshape(a, N_DEV, b).transpose(1, 0, 2)
        return g

    def mixer_bwd(i, dcur, sv, rider):
        kind, j, tag, w = i % 3, i // 3, f"l{i}", W[i]
        if kind == 0:
            dcur, (dn, dwqkv, dwo, dsink), got = swa_layer_bwd(dcur, sv, mix_norm[i][None], w['attn_w_qkv'],
                                                               w['attn_w_o'], tag, rider=rider)
            gs['attn_sink'][j] = dsink
            big = [('attn_w_qkv', j, dwqkv), ('attn_w_o', j, dwo)]
        elif kind == 1:
            dcur, g5, got = s5_layer_bwd(dcur, sv, mix_norm[i][None], ssm(j), w['ssm_w_glu'], tag, rider=rider)
            dn = g5["norm"]
            for nm in ("a_re", "a_im", "log_step", "b_re", "b_im", "c_re", "c_im"):
                gs['ssm_' + nm][j] = g5[nm]
            gs['ssm_d'][j] = g5["d"][0]
            gs['ssm_b_glu'][j] = g5["b_glu"][0]
            big = [('ssm_w_glu', j, g5["w_glu"])]
        else:
            dcur, (dn, dwd, dqn, dkn, dwuq, dwukv, dwo), got = mla_layer_bwd(
                dcur, sv, mix_norm[i][None], w['mla_w_dqkv'], q_norm_full, kv_norm_full, w['mla_w_uq'],
                w['mla_w_ukv'], w['mla_w_o'], tag, rider=rider)
            gs['mla_q_norm'][j] = dqn[0]
            gs['mla_kv_norm'][j] = dkn[0]
            big = [('mla_w_dqkv', j, dwd), ('mla_w_uq', j, dwuq), ('mla_w_ukv', j, dwukv), ('mla_w_o', j, dwo)]
        gs['mix_norm'][i] = dn[0]
        return dcur, [(n, jj, blocked(g, n)) for n, jj, g in big], got

    pending = []
    for i in reversed(range(depth)):
        sv, fsv = saved[i]
        cw, cb = conv_params(i)
        rider = RidingExchange([g for _, _, g in pending]) if pending else None
        res = ffn_bwd(dcur, fsv, ffn_norm[i][None], W[i]['ffn_w_up'], cw, cb, W[i]['ffn_w_down'], f"l{i}",
                      rider=rider)
        dcur, (dn, dwup, dcw, dcb, dwdn) = res[0], res[1]
        for (n, jj, _), got in zip(pending, res[2] if pending else []):
            recv[n][jj] = got
        gs['ffn_norm'][i] = dn[0]
        gs['ffn_conv_w'][i] = dcw.transpose(1, 0, 2).reshape(3, 2 * F)
        gs['ffn_conv_b'][i] = dcb.reshape(2 * F)
        rider = RidingExchange([blocked(dwup, 'ffn_w_up'), blocked(dwdn, 'ffn_w_down')])
        dcur, pending, got = mixer_bwd(i, dcur, sv, rider)
        recv['ffn_w_up'][i], recv['ffn_w_down'][i] = got
    grad_x = dcur[None]

    small_full = [gs[n] if n == 'final_norm' else jnp.stack(gs[n], axis=0) for n in SMALL]
    whole = _pack([loss_part[0, :1]] + small_full, F32)
    received = exchange([g for _, _, g in pending], whole, "exchange_last")
    for (n, jj, _), got in zip(pending, received):
        recv[n][jj] = got
    parts_small = received[-1]

    out = {}
    for n in BIG_NAMES:
        shape = given[n].shape
        rows = (shape[0] * shape[1], shape[2])
        parts = recv[n][0] if shape[0] == 1 else jnp.stack(recv[n], axis=1)
        res = adamw_from_parts(parts.reshape((N_DEV,) + rows), given[n].reshape(rows), given['m_' + n].reshape(rows),
                               given['v_' + n].reshape(rows), f"adamw_{n}")
        for key, val in zip(("grad", "delta", "new_m", "new_v"), res):
            out[key, n] = val.reshape(shape)
    summed = _unpack(sum_parts(parts_small, "sum_small"), [(1,)] + [g.shape for g in small_full])
    loss = summed[0][0]
    small_grads = []
    for n, g in zip(SMALL, summed[1:]):
        small_grads.append(_own_slice(g, idx) if n in SMALL_SHARDED else g)
    small_shapes = [given[n].shape for n in SMALL]
    d_s, nm_s, nv_s = adamw_flat(_pack(small_grads, F32), _pack([given[n] for n in SMALL], F32),
                                 _pack([given['m_' + n] for n in SMALL], F32),
                                 _pack([given['v_' + n] for n in SMALL], F32), "adamw_small")
    for n, g in zip(SMALL, small_grads):
        out["grad", n] = g
    for key, flat in (("delta", d_s), ("new_m", nm_s), ("new_v", nv_s)):
        for n, val in zip(SMALL, _unpack(flat, small_shapes)):
            out[key, n] = val
    return (loss, grad_x, *[out["grad", n] for n in WEIGHTS], *[out["delta", n] for n in WEIGHTS],
            *[out["new_m", n] for n in WEIGHTS], *[out["new_v", n] for n in WEIGHTS])


def _unpack_rows(gathered, shapes):
    flat = gathered.reshape(N_DEV, -1)
    out, off = [], 0
    for s in shapes:
        n = int(np.prod(s))
        out.append(flat[:, off:off + n].reshape((N_DEV,) + tuple(s)))
        off += n
    return out
```

```python
import functools
import math

import numpy as np
import jax
import jax.numpy as jnp
from jax import lax
from jax.experimental import pallas as pl
from jax.experimental.pallas import tpu as pltpu

F32 = jnp.float32
BF16 = jnp.bfloat16

RMS_EPS = 1e-6
ATTN_HEAD_DIM = 64
ATTN_GROUP = 8
ATTN_BLOCK = 128
SSM_GROUP_CH = 16
SSM_STATE = 64
MLA_NOPE = 128
MLA_ROPE = 64
MLA_V = 128
ROPE_THETA = 10000.0
ADAM_LR = 0.001
ADAM_B1 = 0.9
ADAM_B2 = 0.999
ADAM_EPS = 1e-08
ADAM_WD = 0.01
ADAM_STEP = 10

N_DEV = 8
LANES = 128
SUBLANES = 8
VMEM_LIMIT_BYTES = 50 * 2 ** 20
MASK_VALUE = -1e30
LOG2E = math.log2(math.e)
SEGS = SUBLANES


def _params(*sem):
    return pltpu.CompilerParams(dimension_semantics=sem, vmem_limit_bytes=VMEM_LIMIT_BYTES)


def _divisor(n, cap, align):
    best = None
    d = align
    while d <= min(n, cap):
        if n % d == 0:
            best = d
        d += align
    return best if best is not None else n


def _layered(w):
    return w if isinstance(w, tuple) else (w[None], 0)


def _nblocks(w):
    return _layered(w)[0].shape[1]


def mm_nn(a, w, out_dtype, name, tm=1024, tn_cap=1408, parts=1, rider=None):
    M, K = a.shape
    w, layer = _layered(w)
    _, nb, K2, n = w.shape
    assert K == K2
    tm = _divisor(M, tm, 16)
    tn = n if nb > 1 else _divisor(n // parts, tn_cap, LANES)
    jn = n // tn
    J = nb * jn
    assert J % parts == 0
    jp = J // parts

    def body(a_ref, w_ref, o_ref):
        o_ref[...] = jnp.dot(a_ref[...].astype(BF16), w_ref[...], preferred_element_type=F32).astype(o_ref.dtype)

    if parts == 1:
        out_shape = jax.ShapeDtypeStruct((M, nb * n), out_dtype)
        out_spec = pl.BlockSpec((tm, tn), lambda i, j: (i, j))
    else:
        out_shape = jax.ShapeDtypeStruct((parts, M, nb * n // parts), out_dtype)
        out_spec = pl.BlockSpec((None, tm, tn), lambda i, j: (j // jp, i, j % jp))
    res = carried_call(
        body, rider, name, (M // tm, J),
        [pl.BlockSpec((tm, K), lambda i, j: (i, 0)),
         pl.BlockSpec((None, None, K, tn), lambda i, j: (layer, j // jn, 0, j % jn))],
        [out_spec], [out_shape], [], [a, w], ("parallel", "arbitrary"))
    return res[0] if rider is None else (res[0], res[1:])


def mm_nt(a, w, out_dtype, name, tm=1024, tk=1024, tc_cap=2816):
    w, layer = _layered(w)
    _, nb, K, n = w.shape
    split = a.ndim == 3
    M = a.shape[-2]
    tm = _divisor(M, tm, 16)
    tk = _divisor(K, tk, LANES)
    P = a.shape[0] if split else 1
    tc = n if nb > 1 else _divisor(n // P, tc_cap, LANES)
    jn = n // tc
    J = nb * jn
    if split:
        P = a.shape[0]
        assert J % P == 0 and a.shape[2] * P == nb * n
        jp = J // P
        a_spec = pl.BlockSpec((None, tm, tc), lambda i, k, j: (j // jp, i, j % jp))
    else:
        assert a.shape[1] == nb * n
        a_spec = pl.BlockSpec((tm, tc), lambda i, k, j: (i, j))

    def body(a_ref, w_ref, o_ref, acc_ref):
        j = pl.program_id(2)
        part = lax.dot_general(a_ref[...].astype(BF16), w_ref[...], (((1,), (1,)), ((), ())),
                               preferred_element_type=F32)

        @pl.when(j == 0)
        def _():
            acc_ref[...] = part

        @pl.when(j > 0)
        def _():
            acc_ref[...] += part

        @pl.when(j == J - 1)
        def _():
            o_ref[...] = acc_ref[...].astype(o_ref.dtype)

    return pl.pallas_call(
        body, name=name, out_shape=jax.ShapeDtypeStruct((M, K), out_dtype), grid=(M // tm, K // tk, J),
        in_specs=[a_spec, pl.BlockSpec((None, None, tk, tc), lambda i, k, j: (layer, j // jn, k, j % jn))],
        out_specs=pl.BlockSpec((tm, tk), lambda i, k, j: (i, k)),
        scratch_shapes=[pltpu.VMEM((tm, tk), F32)],
        compiler_params=_params("parallel", "parallel", "arbitrary"),
    )(a, w)


def mm_tn(a, b, nb, out_dtype, name, tm=1024, tka=1024, tn_cap=1408):
    M, Ka = a.shape
    split = b.ndim == 3
    N = b.shape[-1] * (b.shape[0] if split else 1)
    n = N // nb
    tm = _divisor(M, tm, 16)
    tka = _divisor(Ka, tka, LANES)
    tn = n if nb > 1 else _divisor(n // (b.shape[0] if split else 1), tn_cap, LANES)
    jn = n // tn
    J = nb * jn
    steps = M // tm
    if split:
        P = b.shape[0]
        assert J % P == 0
        jp = J // P
        b_spec = pl.BlockSpec((None, tm, tn), lambda k, j, i: (j // jp, i, j % jp))
    else:
        b_spec = pl.BlockSpec((tm, tn), lambda k, j, i: (i, j))

    def body(a_ref, b_ref, o_ref, acc_ref):
        i = pl.program_id(2)
        part = lax.dot_general(a_ref[...].astype(BF16), b_ref[...].astype(BF16), (((0,), (0,)), ((), ())),
                               preferred_element_type=F32)

        @pl.when(i == 0)
        def _():
            acc_ref[...] = part

        @pl.when(i > 0)
        def _():
            acc_ref[...] += part

        @pl.when(i == steps - 1)
        def _():
            o_ref[...] = acc_ref[...].astype(o_ref.dtype)

    return pl.pallas_call(
        body, name=name, out_shape=jax.ShapeDtypeStruct((nb, Ka, n), out_dtype), grid=(Ka // tka, J, steps),
        in_specs=[pl.BlockSpec((tm, tka), lambda k, j, i: (i, k)), b_spec],
        out_specs=pl.BlockSpec((None, tka, tn), lambda k, j, i: (j // jn, k, j % jn)),
        scratch_shapes=[pltpu.VMEM((tka, tn), F32)],
        compiler_params=_params("parallel", "parallel", "arbitrary"),
    )(a, b)


def rms_fwd(x, g, out_dtype, name, ts=512):
    S, D = x.shape
    ts = _divisor(S, ts, 16)

    def body(x_ref, g_ref, h_ref, r_ref):
        x = x_ref[...]
        r = lax.rsqrt(jnp.mean(x * x, axis=-1, keepdims=True) + RMS_EPS)
        h_ref[...] = ((x * r) * g_ref[...]).astype(h_ref.dtype)
        r_ref[...] = r

    return pl.pallas_call(
        body, name=name,
        out_shape=(jax.ShapeDtypeStruct((S, D), out_dtype), jax.ShapeDtypeStruct((S, 1), F32)),
        grid=(S // ts,),
        in_specs=[pl.BlockSpec((ts, D), lambda i: (i, 0)), pl.BlockSpec((1, D), lambda i: (0, 0))],
        out_specs=(pl.BlockSpec((ts, D), lambda i: (i, 0)), pl.BlockSpec((ts, 1), lambda i: (i, 0))),
        compiler_params=_params("parallel"),
    )(x, g)


def rms_bwd(dh, x, rstd, g, name, dres=None, ts=512):
    S, D = x.shape
    ts = _divisor(S, ts, 16)
    has_res = dres is not None

    def body(*refs):
        if has_res:
            dh_ref, x_ref, r_ref, g_ref, res_ref, dx_ref, dg_ref = refs
        else:
            dh_ref, x_ref, r_ref, g_ref, dx_ref, dg_ref = refs
        i = pl.program_id(0)
        dh = dh_ref[...].astype(F32)
        r = r_ref[...]
        xhat = x_ref[...] * r
        dhg = dh * g_ref[...]
        c = jnp.mean(dhg * xhat, axis=-1, keepdims=True)
        dx = r * (dhg - xhat * c)
        if has_res:
            dx = dx + res_ref[...]
        dx_ref[...] = dx
        part = jnp.sum(dh * xhat, axis=0, keepdims=True)

        @pl.when(i == 0)
        def _():
            dg_ref[...] = part

        @pl.when(i > 0)
        def _():
            dg_ref[...] += part

    row = pl.BlockSpec((ts, D), lambda i: (i, 0))
    args = [dh, x, rstd, g] + ([dres] if has_res else [])
    specs = [row, row, pl.BlockSpec((ts, 1), lambda i: (i, 0)), pl.BlockSpec((1, D), lambda i: (0, 0))]
    specs += [row] if has_res else []
    return pl.pallas_call(
        body, name=name,
        out_shape=(jax.ShapeDtypeStruct((S, D), F32), jax.ShapeDtypeStruct((1, D), F32)),
        grid=(S // ts,), in_specs=specs,
        out_specs=(row, pl.BlockSpec((1, D), lambda i: (0, 0))),
        compiler_params=_params("arbitrary"),
    )(*args)


def ew(fn, name, mats, rows, out_dtypes, n_sums=0, ts=512, tc=1024):
    S, C = mats[0].shape
    ts = _divisor(S, ts, 16)
    tc = _divisor(C, tc, LANES)
    n_in = len(mats) + len(rows)
    n_out = len(out_dtypes)

    def body(*refs):
        i = pl.program_id(1)
        res = fn(*[r[...] for r in refs[:n_in]])
        if not isinstance(res, (tuple, list)):
            res = (res,)
        for o_ref, val in zip(refs[n_in:n_in + n_out], res[:n_out]):
            o_ref[...] = val.astype(o_ref.dtype)
        for s_ref, val in zip(refs[n_in + n_out:], res[n_out:]):
            part = jnp.sum(val, axis=0, keepdims=True)

            @pl.when(i == 0)
            def _():
                s_ref[...] = part

            @pl.when(i > 0)
            def _():
                s_ref[...] += part

    mat = pl.BlockSpec((ts, tc), lambda j, i: (i, j))
    row = pl.BlockSpec((1, tc), lambda j, i: (0, j))
    out_shape = tuple(jax.ShapeDtypeStruct((S, C), d) for d in out_dtypes)
    out_shape += tuple(jax.ShapeDtypeStruct((1, C), F32) for _ in range(n_sums))
    return pl.pallas_call(
        body, name=name, out_shape=out_shape, grid=(C // tc, S // ts),
        in_specs=[mat] * len(mats) + [row] * len(rows),
        out_specs=tuple([mat] * n_out + [row] * n_sums),
        compiler_params=_params("parallel", "arbitrary"),
    )(*mats, *rows)


def _halo_specs(ts, tc, S, lead):
    nblk = S // SUBLANES
    per = ts // SUBLANES
    pre = (None,) * 0
    if lead:
        main = pl.BlockSpec((lead, ts, tc), lambda j, i: (0, i, j))
        prev = pl.BlockSpec((lead, SUBLANES, tc), lambda j, i: (0, jnp.maximum(i * per - 1, 0), j))
        nxt = pl.BlockSpec((lead, SUBLANES, tc), lambda j, i: (0, jnp.minimum((i + 1) * per, nblk - 1), j))
    else:
        main = pl.BlockSpec((ts, tc), lambda j, i: (i, j))
        prev = pl.BlockSpec((SUBLANES, tc), lambda j, i: (jnp.maximum(i * per - 1, 0), j))
        nxt = pl.BlockSpec((SUBLANES, tc), lambda j, i: (jnp.minimum((i + 1) * per, nblk - 1), j))
    return main, prev, nxt


def _extended(prev, main, nxt, i, ts, S):
    before = jnp.where(i > 0, prev.astype(F32), 0.0)
    after = jnp.where((i + 1) * ts < S, nxt.astype(F32), 0.0)
    return jnp.concatenate([before, main.astype(F32), after], axis=0)


def _taps(ue):
    n = ue.shape[0]
    return pltpu.roll(ue, 1, axis=0), ue, pltpu.roll(ue, n - 1, axis=0)


def _conv3(taps, w, b):
    return b + (w[0:1] * taps[0] + w[1:2] * taps[1] + w[2:3] * taps[2])


def convgate_fwd(u3, conv_w, conv_b, name, ts=512, tc=512):
    _, S, F = u3.shape
    ts = _divisor(S, ts, 16)
    tc = _divisor(F, tc, LANES)
    main, prev, nxt = _halo_specs(ts, tc, S, 2)

    def body(m_ref, p_ref, n_ref, w_ref, b_ref, o_ref):
        i = pl.program_id(1)
        c = []
        for h in range(2):
            ue = _extended(p_ref[h], m_ref[h], n_ref[h], i, ts, S)
            c.append(_conv3(_taps(ue), w_ref[h], b_ref[h])[SUBLANES:SUBLANES + ts])
        o_ref[...] = (jax.nn.silu(c[0]) * c[1]).astype(o_ref.dtype)

    return pl.pallas_call(
        body, name=name, out_shape=jax.ShapeDtypeStruct((S, F), BF16), grid=(F // tc, S // ts),
        in_specs=[main, prev, nxt, pl.BlockSpec((2, 3, tc), lambda j, i: (0, 0, j)),
                  pl.BlockSpec((2, 1, tc), lambda j, i: (0, 0, j))],
        out_specs=pl.BlockSpec((ts, tc), lambda j, i: (i, j)),
        compiler_params=_params("parallel", "arbitrary"),
    )(u3, u3, u3, conv_w, conv_b)


def convgate_bwd(u3, da, conv_w, conv_b, name, ts=512, tc=512, rider=None):
    _, S, F = u3.shape
    ts = _divisor(S, ts, 16)
    tc = _divisor(F, tc, LANES)
    main, prev, nxt = _halo_specs(ts, tc, S, 2)
    amain, aprev, anxt = _halo_specs(ts, tc, S, 0)
    n = ts + 2 * SUBLANES
    mid = slice(SUBLANES, SUBLANES + ts)

    def body(m_ref, p_ref, n_ref, am_ref, ap_ref, an_ref, w_ref, b_ref, du_ref, dw_ref, db_ref):
        i = pl.program_id(1)
        ue = [_extended(p_ref[h], m_ref[h], n_ref[h], i, ts, S) for h in range(2)]
        shifted = [_taps(ue[h]) for h in range(2)]
        g = _conv3(shifted[0], w_ref[0], b_ref[0])
        v = _conv3(shifted[1], w_ref[1], b_ref[1])
        dae = _extended(ap_ref[...], am_ref[...], an_ref[...], i, ts, S)
        sg = jax.nn.sigmoid(g)
        dc = [dae * v * (sg * (1.0 + g * (1.0 - sg))), dae * (g * sg)]
        for h in range(2):
            w = w_ref[h]
            du = w[0:1] * pltpu.roll(dc[h], n - 1, axis=0) + w[1:2] * dc[h] + w[2:3] * pltpu.roll(dc[h], 1, axis=0)
            du_ref[h] = du[mid].astype(du_ref.dtype)
            dcm = dc[h][mid]
            sums = [jnp.sum(dcm * t[mid], axis=0, keepdims=True) for t in shifted[h]]
            db = jnp.sum(dcm, axis=0, keepdims=True)

            @pl.when(i == 0)
            def _():
                for t in range(3):
                    dw_ref[h, t:t + 1, :] = sums[t]
                db_ref[h] = db

            @pl.when(i > 0)
            def _():
                for t in range(3):
                    dw_ref[h, t:t + 1, :] += sums[t]
                db_ref[h] += db

    res = carried_call(
        body, rider, name, (F // tc, S // ts),
        [main, prev, nxt, amain, aprev, anxt, pl.BlockSpec((2, 3, tc), lambda j, i: (0, 0, j)),
         pl.BlockSpec((2, 1, tc), lambda j, i: (0, 0, j))],
        [pl.BlockSpec((2, ts, tc), lambda j, i: (0, i, j)), pl.BlockSpec((2, 3, tc), lambda j, i: (0, 0, j)),
         pl.BlockSpec((2, 1, tc), lambda j, i: (0, 0, j))],
        [jax.ShapeDtypeStruct((2, S, F), BF16), jax.ShapeDtypeStruct((2, 3, F), F32),
         jax.ShapeDtypeStruct((2, 1, F), F32)], [], [u3, u3, u3, da, da, da, conv_w, conv_b],
        ("parallel", "arbitrary"))
    return tuple(res[:3]) if rider is None else (res[0], res[1], res[2], res[3:])


def ffn_fwd(x, norm_g, w_up, conv_w, conv_b, w_down, tag, rider=None):
    h, rstd = rms_fwd(x, norm_g, BF16, f"ffn_norm_{tag}")
    u3 = mm_nn(h, w_up, F32, f"ffn_up_{tag}", parts=2, rider=rider)
    if rider is not None:
        u3, got = u3
    a = convgate_fwd(u3, conv_w, conv_b, f"ffn_gate_{tag}")
    y = mm_nn_res(a, w_down, x, f"ffn_down_{tag}")
    return (y, (x, h, rstd, u3, a)) if rider is None else (y, (x, h, rstd, u3, a), got)


def mm_nn_res(a, w, res, name, tm=1024, tn=512):
    M, K = a.shape
    w, layer = _layered(w)
    N = w.shape[3]
    assert w.shape[1] == 1
    tm = _divisor(M, tm, 16)
    tn = _divisor(N, tn, LANES)

    def body(a_ref, w_ref, r_ref, o_ref):
        o_ref[...] = r_ref[...] + jnp.dot(a_ref[...].astype(BF16), w_ref[...], preferred_element_type=F32)

    return pl.pallas_call(
        body, name=name, out_shape=jax.ShapeDtypeStruct((M, N), F32), grid=(M // tm, N // tn),
        in_specs=[pl.BlockSpec((tm, K), lambda i, j: (i, 0)),
                  pl.BlockSpec((None, None, K, tn), lambda i, j: (layer, 0, 0, j)),
                  pl.BlockSpec((tm, tn), lambda i, j: (i, j))],
        out_specs=pl.BlockSpec((tm, tn), lambda i, j: (i, j)),
        compiler_params=_params("parallel", "arbitrary"),
    )(a, w, res)


def ffn_bwd(dy, saved, norm_g, w_up, conv_w, conv_b, w_down, tag, rider=None):
    x, h, rstd, u3, a = saved
    nb = _nblocks(w_up)
    dw_down = mm_tn(a, dy, 1, BF16, f"ffn_dwdown_{tag}", tka=1408)
    da = mm_nt(dy, w_down, F32, f"ffn_da_{tag}", tk=512)
    du3, dconv_w, dconv_b, *got = convgate_bwd(u3, da, conv_w, conv_b, f"ffn_dgate_{tag}", rider=rider)
    dh = mm_nt(du3, w_up, F32, f"ffn_dh_{tag}")
    dw_up = mm_tn(h, du3, nb, BF16, f"ffn_dwup_{tag}")
    dx, dnorm = rms_bwd(dh, x, rstd, norm_g, f"ffn_dnorm_{tag}", dres=dy)
    grads = (dnorm, dw_up, dconv_w, dconv_b, dw_down)
    return (dx, grads) if rider is None else (dx, grads, got[0])


def _swa_bias(n_heads):
    kv = n_heads // ATTN_GROUP
    slopes = jnp.asarray((2.0 ** (-8.0 * np.arange(1, n_heads + 1) / n_heads)).astype(np.float32))
    rel = jnp.arange(SWA_KEYS)[None, :] - ATTN_BLOCK - jnp.arange(ATTN_BLOCK)[:, None]
    dist = jnp.abs(rel).astype(F32)
    bias = (-slopes.reshape(kv, ATTN_GROUP, 1, 1) * dist) * LOG2E
    col = jnp.arange(SWA_KEYS)
    out = []
    for dead in (col < ATTN_BLOCK, col < 0, col >= 2 * ATTN_BLOCK):
        keep = (jnp.abs(rel) <= ATTN_BLOCK) & ~dead[None, :]
        out.append(jnp.where(keep[None, None], bias, MASK_VALUE).reshape(kv, SWA_ROWS, SWA_KEYS))
    return jnp.stack(out)


SWA_ROWS = ATTN_GROUP * ATTN_BLOCK
SWA_KEYS = 3 * ATTN_BLOCK
SWA_LOGIT_SCALE = ATTN_HEAD_DIM ** -0.5 * LOG2E


def _swa_probs(q, kc, bias, sink):
    reps = (1, SWA_KEYS // LANES)
    s = lax.dot_general(q, kc, (((1,), (1,)), ((), ())), preferred_element_type=F32) * SWA_LOGIT_SCALE + bias
    m = jnp.maximum(jnp.max(s, axis=1, keepdims=True), sink)
    p = jnp.exp2(s - jnp.tile(m, reps))
    es = jnp.exp2(sink - m)
    r = 1.0 / (jnp.sum(p, axis=1, keepdims=True) + es)
    return p * jnp.tile(r, reps), es * r


def _swa_specs(S):
    nb = S // ATTN_BLOCK

    def at(off):
        return pl.BlockSpec((None, ATTN_BLOCK, ATTN_HEAD_DIM),
                            lambda c, n: (c, jnp.clip(jnp.minimum(n, nb - 1) + off, 0, nb - 1), 0))

    qspec = pl.BlockSpec((None, ATTN_GROUP, ATTN_BLOCK, ATTN_HEAD_DIM), lambda c, n: (c, 0, jnp.minimum(n, nb - 1), 0))
    bias = pl.BlockSpec((None, None, SWA_ROWS, SWA_KEYS),
                        lambda c, n: (jnp.where(n == 0, 0, jnp.where(n >= nb - 1, 2, 1)), c, 0, 0))
    sink = pl.BlockSpec((None, SWA_ROWS, LANES), lambda c, n: (c, 0, 0))
    return qspec, [at(-1), at(0), at(1)], bias, sink


def swa_fwd(q, k, v, bias, sink, name, rider=None):
    KV, G, S, dh = q.shape
    nb = S // ATTN_BLOCK
    qspec, kspecs, bspec, sspec = _swa_specs(S)

    def body(q_ref, k0, k1, k2, v0, v1, v2, b_ref, s_ref, o_ref):
        kc = jnp.concatenate([k0[...], k1[...], k2[...]], axis=0)
        vc = jnp.concatenate([v0[...], v1[...], v2[...]], axis=0)
        p, _ = _swa_probs(q_ref[...].reshape(SWA_ROWS, dh), kc, b_ref[...], s_ref[...])
        o = jnp.dot(p.astype(BF16), vc, preferred_element_type=F32)
        o_ref[...] = o.reshape(G, ATTN_BLOCK, dh).astype(o_ref.dtype)

    res = carried_call(
        body, rider, name, (KV, nb), [qspec] + kspecs + kspecs + [bspec, sspec], [qspec],
        [jax.ShapeDtypeStruct(q.shape, BF16)], [], [q, k, k, k, v, v, v, bias, sink], ("parallel", "arbitrary"))
    return res[0] if rider is None else (res[0], res[1:])


def swa_bwd(q, k, v, bias, sink, do, name, rider=None):
    KV, G, S, dh = q.shape
    nb = S // ATTN_BLOCK
    qspec, kspecs, bspec, sspec = _swa_specs(S)
    scale = ATTN_HEAD_DIM ** -0.5
    kv_out = pl.BlockSpec((None, ATTN_BLOCK, dh), lambda c, n: (c, jnp.maximum(n - 1, 0), 0))

    def body(q_ref, k0, k1, k2, v0, v1, v2, b_ref, s_ref, do_ref, dq_ref, dk_ref, dv_ref, ds_ref, dk_acc, dv_acc):
        n = pl.program_id(1)

        @pl.when(n == 0)
        def _():
            dk_acc[...] = jnp.zeros_like(dk_acc)
            dv_acc[...] = jnp.zeros_like(dv_acc)
            ds_ref[...] = jnp.zeros_like(ds_ref)

        @pl.when(n > 0)
        def _():
            dk_acc[(n + 1) % 3] = jnp.zeros((ATTN_BLOCK, dh), F32)
            dv_acc[(n + 1) % 3] = jnp.zeros((ATTN_BLOCK, dh), F32)

        @pl.when(n < nb)
        def _():
            kc = jnp.concatenate([k0[...], k1[...], k2[...]], axis=0)
            vc = jnp.concatenate([v0[...], v1[...], v2[...]], axis=0)
            qa = q_ref[...].reshape(SWA_ROWS, dh)
            da = do_ref[...].reshape(SWA_ROWS, dh)
            p, psink = _swa_probs(qa, kc, b_ref[...], s_ref[...])
            dp = lax.dot_general(da, vc, (((1,), (1,)), ((), ())), preferred_element_type=F32)
            delta = jnp.sum(p * dp, axis=1, keepdims=True) + jnp.zeros((SWA_ROWS, LANES), F32)
            ds = ((p * (dp - jnp.tile(delta, (1, SWA_KEYS // LANES)))) * scale).astype(BF16)
            dq = jnp.dot(ds, kc, preferred_element_type=F32)
            dq_ref[...] = dq.reshape(G, ATTN_BLOCK, dh).astype(dq_ref.dtype)
            dkc = lax.dot_general(ds, qa, (((0,), (0,)), ((), ())), preferred_element_type=F32)
            dvc = lax.dot_general(p.astype(BF16), da, (((0,), (0,)), ((), ())), preferred_element_type=F32)
            ds_ref[...] -= psink * delta
            for o in range(3):
                slot = (n + 2 + o) % 3
                dk_acc[slot] += dkc[o * ATTN_BLOCK:(o + 1) * ATTN_BLOCK]
                dv_acc[slot] += dvc[o * ATTN_BLOCK:(o + 1) * ATTN_BLOCK]

        done = (n + 2) % 3
        dk_ref[...] = dk_acc[done].astype(dk_ref.dtype)
        dv_ref[...] = dv_acc[done].astype(dv_ref.dtype)

    res = carried_call(
        body, rider, name, (KV, nb + 1), [qspec] + kspecs + kspecs + [bspec, sspec, qspec],
        [qspec, kv_out, kv_out, sspec],
        [jax.ShapeDtypeStruct(q.shape, BF16), jax.ShapeDtypeStruct(k.shape, BF16),
         jax.ShapeDtypeStruct(v.shape, BF16), jax.ShapeDtypeStruct((KV, SWA_ROWS, LANES), F32)],
        [pltpu.VMEM((3, ATTN_BLOCK, dh), F32), pltpu.VMEM((3, ATTN_BLOCK, dh), F32)],
        [q, k, k, k, v, v, v, bias, sink, do], ("parallel", "arbitrary"))
    return tuple(res[:4]) if rider is None else (res[0], res[1], res[2], res[3], res[4:])


def _heads_major(x, n_heads):
    S = x.shape[0]
    return x.reshape(S, n_heads, -1).transpose(1, 0, 2)


def _heads_minor(x):
    H, S, dh = x.shape
    return x.transpose(1, 0, 2).reshape(S, H * dh)


def swa_layer_fwd(x, norm_g, w_qkv, w_o, sink, tag, rider=None):
    S, D = x.shape
    H = _layered(w_o)[0].shape[2] // ATTN_HEAD_DIM
    KV = H // ATTN_GROUP
    h, rstd = rms_fwd(x, norm_g, BF16, f"swa_norm_{tag}")
    qkv = mm_nn(h, w_qkv, BF16, f"swa_qkv_{tag}", tn_cap=1280)
    q = _heads_major(qkv[:, :H * ATTN_HEAD_DIM], H).reshape(KV, ATTN_GROUP, S, ATTN_HEAD_DIM)
    k = _heads_major(qkv[:, H * ATTN_HEAD_DIM:(H + KV) * ATTN_HEAD_DIM], KV)
    v = _heads_major(qkv[:, (H + KV) * ATTN_HEAD_DIM:], KV)
    bias = _swa_bias(H)
    sinkb = jnp.broadcast_to((sink.astype(F32) * LOG2E).reshape(KV, ATTN_GROUP, 1, 1),
                             (KV, ATTN_GROUP, ATTN_BLOCK, LANES)).reshape(KV, SWA_ROWS, LANES)
    o = swa_fwd(q, k, v, bias, sinkb, f"swa_attn_{tag}", rider=rider)
    if rider is not None:
        o, got = o
    o2 = _heads_minor(o.reshape(H, S, ATTN_HEAD_DIM))
    y = mm_nn_res(o2, w_o, x, f"swa_out_{tag}")
    saved = (x, h, rstd, q, k, v, bias, sinkb, o2)
    return (y, saved) if rider is None else (y, saved, got)


def swa_layer_bwd(dy, saved, norm_g, w_qkv, w_o, tag, rider=None):
    x, h, rstd, q, k, v, bias, sinkb, o2 = saved
    KV, G, S, dh = q.shape
    H = KV * G
    dw_o = mm_tn(o2, dy, 1, BF16, f"swa_dwo_{tag}")
    do2 = mm_nt(dy, w_o, BF16, f"swa_do_{tag}")
    do = _heads_major(do2, H).reshape(KV, G, S, dh)
    dq, dk, dv, dsink, *got = swa_bwd(q, k, v, bias, sinkb, do, f"swa_dattn_{tag}", rider=rider)
    dqkv = jnp.concatenate([_heads_minor(dq.reshape(H, S, dh)), _heads_minor(dk), _heads_minor(dv)], axis=1)
    dw_qkv = mm_tn(h, dqkv, 1, BF16, f"swa_dwqkv_{tag}", tn_cap=1280)
    dh_ = mm_nt(dqkv, w_qkv, F32, f"swa_dh_{tag}", tc_cap=2560)
    dx, dnorm = rms_bwd(dh_, x, rstd, norm_g, f"swa_dnorm_{tag}", dres=dy)
    grads = (dnorm, dw_qkv, dw_o, dsink[:, :, 0].reshape(H, ATTN_BLOCK).sum(axis=1))
    return (dx, grads) if rider is None else (dx, grads, got[0])


def flash_fwd(q, k, v, scale, name, tq=1024, tk=512, rider=None):
    H, S, dk = q.shape
    dv = v.shape[-1]
    assert dv == LANES
    tq = _divisor(S, tq, 16)
    tk = _divisor(S, tk, LANES)
    nk = S // tk
    c = scale * LOG2E

    def body(q_ref, k_ref, v_ref, o_ref, lse_ref, m_sc, l_sc, acc_sc):
        m_sc[...] = jnp.full(m_sc.shape, MASK_VALUE, F32)
        l_sc[...] = jnp.zeros(l_sc.shape, F32)
        acc_sc[...] = jnp.zeros(acc_sc.shape, F32)

        def step(kt, carry):
            off = pl.multiple_of(kt * tk, tk)
            s = lax.dot_general(q_ref[...], k_ref[pl.ds(off, tk), :], (((1,), (1,)), ((), ())),
                                preferred_element_type=F32) * c
            m_prev = m_sc[...]
            m_new = jnp.maximum(m_prev, jnp.max(s, axis=1, keepdims=True))
            alpha = jnp.exp2(m_prev - m_new)
            p = jnp.exp2(s - jnp.tile(m_new, (1, tk // LANES)))
            l_sc[...] = alpha * l_sc[...] + jnp.sum(p, axis=1, keepdims=True)
            acc_sc[...] = alpha * acc_sc[...] + jnp.dot(p.astype(BF16), v_ref[pl.ds(off, tk), :],
                                                        preferred_element_type=F32)
            m_sc[...] = m_new
            return carry

        lax.fori_loop(0, nk, step, 0)
        l = l_sc[...]
        o_ref[...] = (acc_sc[...] / l).astype(o_ref.dtype)
        lse_ref[...] = (m_sc[...] + jnp.log2(l))[:, :1]

    res = carried_call(
        body, rider, name, (H, S // tq),
        [pl.BlockSpec((None, tq, dk), lambda h, i: (h, i, 0)),
         pl.BlockSpec((None, S, dk), lambda h, i: (h, 0, 0)),
         pl.BlockSpec((None, S, dv), lambda h, i: (h, 0, 0))],
        [pl.BlockSpec((None, tq, dv), lambda h, i: (h, i, 0)), pl.BlockSpec((None, tq, 1), lambda h, i: (h, i, 0))],
        [jax.ShapeDtypeStruct((H, S, dv), BF16), jax.ShapeDtypeStruct((H, S, 1), F32)],
        [pltpu.VMEM((tq, LANES), F32), pltpu.VMEM((tq, LANES), F32), pltpu.VMEM((tq, dv), F32)],
        [q, k, v], ("parallel", "arbitrary"))
    return (res[0], res[1]) if rider is None else (res[0], res[1], res[2:])


def flash_delta(o, do, name, ts=1024):
    H, S, dv = o.shape
    ts = _divisor(S, ts, 16)

    def body(o_ref, do_ref, d_ref):
        d_ref[...] = jnp.sum(o_ref[...].astype(F32) * do_ref[...].astype(F32), axis=-1, keepdims=True)

    spec = pl.BlockSpec((None, ts, dv), lambda h, i: (h, i, 0))
    return pl.pallas_call(
        body, name=name, out_shape=jax.ShapeDtypeStruct((H, S, 1), F32), grid=(H, S // ts),
        in_specs=[spec, spec], out_specs=pl.BlockSpec((None, ts, 1), lambda h, i: (h, i, 0)),
        compiler_params=_params("parallel", "parallel"),
    )(o, do)


def flash_bwd(q, k, v, do, lse_row, delta_row, scale, name, tq=1024, tk=512, rider=None):
    H, S, dk = q.shape
    dv = v.shape[-1]
    tq = _divisor(S, tq, LANES)
    tk = _divisor(S, tk, 16)
    nq = S // tq
    c = scale * LOG2E

    def body(q_ref, k_ref, v_ref, do_ref, lse_ref, dl_ref, dq_ref, dk_ref, dv_ref, dk_sc, dv_sc):
        j = pl.program_id(1)

        @pl.when(j == 0)
        def _():
            dq_ref[...] = jnp.zeros(dq_ref.shape, F32)

        dk_sc[...] = jnp.zeros(dk_sc.shape, F32)
        dv_sc[...] = jnp.zeros(dv_sc.shape, F32)
        kk = k_ref[...]
        vv = v_ref[...]

        def step(t, carry):
            off = pl.multiple_of(t * tq, tq)
            qq = q_ref[pl.ds(off, tq), :]
            dd = do_ref[pl.ds(off, tq), :]
            st = lax.dot_general(kk, qq, (((1,), (1,)), ((), ())), preferred_element_type=F32) * c
            pt = jnp.exp2(st - lse_ref[:, pl.ds(off, tq)])
            dpt = lax.dot_general(vv, dd, (((1,), (1,)), ((), ())), preferred_element_type=F32)
            dst = ((pt * (dpt - dl_ref[:, pl.ds(off, tq)])) * scale).astype(BF16)
            dv_sc[...] += jnp.dot(pt.astype(BF16), dd, preferred_element_type=F32)
            dk_sc[...] += jnp.dot(dst, qq, preferred_element_type=F32)
            dq_ref[pl.ds(off, tq), :] += lax.dot_general(dst, kk, (((0,), (0,)), ((), ())),
                                                         preferred_element_type=F32)
            return carry

        lax.fori_loop(0, nq, step, 0)
        dk_ref[...] = dk_sc[...].astype(dk_ref.dtype)
        dv_ref[...] = dv_sc[...].astype(dv_ref.dtype)

    ks = pl.BlockSpec((None, tk, dk), lambda h, j: (h, j, 0))
    vs = pl.BlockSpec((None, tk, dv), lambda h, j: (h, j, 0))
    row = pl.BlockSpec((None, 1, S), lambda h, j: (h, 0, 0))
    qs = pl.BlockSpec((None, S, dk), lambda h, j: (h, 0, 0))
    res = carried_call(
        body, rider, name, (H, S // tk),
        [qs, ks, vs, pl.BlockSpec((None, S, dv), lambda h, j: (h, 0, 0)), row, row], [qs, ks, vs],
        [jax.ShapeDtypeStruct((H, S, dk), F32), jax.ShapeDtypeStruct((H, S, dk), F32),
         jax.ShapeDtypeStruct((H, S, dv), BF16)],
        [pltpu.VMEM((tk, dk), F32), pltpu.VMEM((tk, dv), F32)], [q, k, v, do, lse_row, delta_row],
        ("parallel", "arbitrary"))
    return tuple(res[:3]) if rider is None else (res[0], res[1], res[2], res[3:])


def _rope_tables(S, reps):
    half = MLA_ROPE // 2
    pos = jnp.arange(S, dtype=F32)
    inv = ROPE_THETA ** (-jnp.arange(half, dtype=F32) / half)
    ang = pos[:, None] * inv[None, :]
    return jnp.tile(jnp.cos(ang), (1, reps)), jnp.tile(jnp.sin(ang), (1, reps))


def _rotate(x1, x2, cos, sin, out_dtype, name):
    return ew(lambda a, b, c, s: (a * c - b * s, b * c + a * s), name, [x1, x2, cos, sin], [], [out_dtype, out_dtype])


def mla_layer_fwd(x, norm_g, w_dqkv, q_norm, kv_norm, w_uq, w_ukv, w_o, tag, rider=None):
    S, D = x.shape
    QL, KL = q_norm.shape[1], kv_norm.shape[1]
    H = _layered(w_o)[0].shape[2] // MLA_V
    half = MLA_ROPE // 2
    h, rstd = rms_fwd(x, norm_g, BF16, f"mla_norm_{tag}")
    d = mm_nn(h, w_dqkv, F32, f"mla_down_{tag}")
    c_q, c_kv, k_rope = d[:, :QL], d[:, QL:QL + KL], d[:, QL + KL:]
    cq_n, rstd_q = rms_fwd(c_q, q_norm, BF16, f"mla_qnorm_{tag}")
    ckv_n, rstd_kv = rms_fwd(c_kv, kv_norm, BF16, f"mla_kvnorm_{tag}")
    q = mm_nn(cq_n, w_uq, F32, f"mla_uq_{tag}").reshape(S, H, MLA_NOPE + MLA_ROPE)
    kv = mm_nn(ckv_n, w_ukv, BF16, f"mla_ukv_{tag}").reshape(S, H, MLA_NOPE + MLA_V)
    cos, sin = _rope_tables(S, H + 1)
    x1 = jnp.concatenate([q[:, :, MLA_NOPE:MLA_NOPE + half].reshape(S, H * half), k_rope[:, :half]], axis=1)
    x2 = jnp.concatenate([q[:, :, MLA_NOPE + half:].reshape(S, H * half), k_rope[:, half:]], axis=1)
    r1, r2 = _rotate(x1, x2, cos, sin, BF16, f"mla_rope_{tag}")
    qr = jnp.concatenate([r1[:, :H * half].reshape(S, H, half), r2[:, :H * half].reshape(S, H, half)], axis=2)
    kr = jnp.concatenate([r1[:, H * half:], r2[:, H * half:]], axis=1)
    qh = jnp.concatenate([q[:, :, :MLA_NOPE].astype(BF16), qr], axis=2).transpose(1, 0, 2)
    kh = jnp.concatenate([kv[:, :, :MLA_NOPE], jnp.broadcast_to(kr[:, None, :], (S, H, MLA_ROPE))],
                         axis=2).transpose(1, 0, 2)
    vh = kv[:, :, MLA_NOPE:].transpose(1, 0, 2)
    scale = (MLA_NOPE + MLA_ROPE) ** -0.5
    oh, lse, *got = flash_fwd(qh, kh, vh, scale, f"mla_attn_{tag}", rider=rider)
    o2 = _heads_minor(oh)
    y = mm_nn_res(o2, w_o, x, f"mla_out_{tag}")
    saved = (x, h, rstd, c_q, rstd_q, cq_n, c_kv, rstd_kv, ckv_n, qh, kh, vh, oh, lse, o2, cos, sin)
    return (y, saved) if rider is None else (y, saved, got[0])


def mla_layer_bwd(dy, saved, norm_g, w_dqkv, q_norm, kv_norm, w_uq, w_ukv, w_o, tag, rider=None):
    x, h, rstd, c_q, rstd_q, cq_n, c_kv, rstd_kv, ckv_n, qh, kh, vh, oh, lse, o2, cos, sin = saved
    H, S, _ = qh.shape
    half = MLA_ROPE // 2
    scale = (MLA_NOPE + MLA_ROPE) ** -0.5
    dw_o = mm_tn(o2, dy, 1, BF16, f"mla_dwo_{tag}")
    do2 = mm_nt(dy, w_o, BF16, f"mla_do_{tag}")
    doh = _heads_major(do2, H)
    delta = flash_delta(oh, doh, f"mla_delta_{tag}")
    dqh, dkh, dvh, *got = flash_bwd(qh, kh, vh, doh, lse.reshape(H, 1, S), delta.reshape(H, 1, S), scale,
                                    f"mla_dattn_{tag}", rider=rider)
    dq = dqh.transpose(1, 0, 2)
    dk = dkh.transpose(1, 0, 2)
    dkr = jnp.sum(dk[:, :, MLA_NOPE:], axis=1)
    g1 = jnp.concatenate([dq[:, :, MLA_NOPE:MLA_NOPE + half].reshape(S, H * half), dkr[:, :half]], axis=1)
    g2 = jnp.concatenate([dq[:, :, MLA_NOPE + half:].reshape(S, H * half), dkr[:, half:]], axis=1)
    b1, b2 = _rotate(g1, g2, cos, -sin, F32, f"mla_drope_{tag}")
    dq_rope = jnp.concatenate([b1[:, :H * half].reshape(S, H, half), b2[:, :H * half].reshape(S, H, half)], axis=2)
    dk_rope = jnp.concatenate([b1[:, H * half:], b2[:, H * half:]], axis=1)
    dq_full = jnp.concatenate([dq[:, :, :MLA_NOPE], dq_rope], axis=2).reshape(S, -1).astype(BF16)
    dkv = jnp.concatenate([dk[:, :, :MLA_NOPE].astype(BF16), dvh.transpose(1, 0, 2)], axis=2).reshape(S, -1)
    dw_uq = mm_tn(cq_n, dq_full, _nblocks(w_uq), BF16, f"mla_dwuq_{tag}")
    dw_ukv = mm_tn(ckv_n, dkv, _nblocks(w_ukv), BF16, f"mla_dwukv_{tag}")
    dcq_n = mm_nt(dq_full, w_uq, F32, f"mla_dcq_{tag}")
    dckv_n = mm_nt(dkv, w_ukv, F32, f"mla_dckv_{tag}")
    dc_q, dq_norm = rms_bwd(dcq_n, c_q, rstd_q, q_norm, f"mla_dqnorm_{tag}")
    dc_kv, dkv_norm = rms_bwd(dckv_n, c_kv, rstd_kv, kv_norm, f"mla_dkvnorm_{tag}")
    dd = jnp.concatenate([dc_q, dc_kv, dk_rope], axis=1).astype(BF16)
    dw_dqkv = mm_tn(h, dd, 1, BF16, f"mla_dwdown_{tag}")
    dh_ = mm_nt(dd, w_dqkv, F32, f"mla_dh_{tag}")
    dx, dnorm = rms_bwd(dh_, x, rstd, norm_g, f"mla_dnorm_{tag}", dres=dy)
    grads = (dnorm, dw_dqkv, dq_norm, dkv_norm, dw_uq, dw_ukv, dw_o)
    return (dx, grads) if rider is None else (dx, grads, got[0])


SLAB = LANES
SLAB_GROUPS = SLAB // SSM_GROUP_CH
SLAB_HALF = SLAB_GROUPS * SSM_STATE
SLAB_W = 2 * SLAB_HALF


def _scan_rows(st_ref, carry_ref, lam_ref, nt, rev):
    h = SLAB_HALF
    lr = lam_ref[:, :h]
    li = lam_ref[:, h:]

    def step(i, c):
        xr, xi = c
        ii = (nt - 1 - i) if rev else i
        row = pl.multiple_of(ii * SEGS, SEGS)
        nr = lr * xr - li * xi + st_ref[pl.ds(row, SEGS), :h]
        ni = lr * xi + li * xr + st_ref[pl.ds(row, SEGS), h:]
        st_ref[pl.ds(row, SEGS), :h] = nr
        st_ref[pl.ds(row, SEGS), h:] = ni
        return nr, ni

    xr, xi = lax.fori_loop(0, nt, step, (carry_ref[:, :h], carry_ref[:, h:]), unroll=4)
    carry_ref[:, :h] = xr
    carry_ref[:, h:] = xi


def s5_scan(mode, inp, win, lam, rev, name, init=None, wout=None, xs=None, xinit=None, u=None, rows=1024,
            rider=None):
    T, C = inp.shape
    K = win.shape[0]
    W = SLAB_W
    Tc = _divisor(T, rows, 16)
    nt = Tc // SEGS
    nT = T // Tc
    tiles = T // SEGS

    def chunk(jj):
        return (nT - 1 - jj) if rev else jj

    slab_in = pl.BlockSpec((Tc, SLAB), lambda k, jj: (chunk(jj), k))
    wspec = pl.BlockSpec((None, SLAB, W), lambda k, jj: (k, 0, 0))
    vspec = pl.BlockSpec((None, SEGS, W), lambda k, jj: (k, 0, 0))
    wospec = pl.BlockSpec((None, W, SLAB), lambda k, jj: (k, 0, 0))
    xspec = pl.BlockSpec((Tc, W), lambda k, jj: (chunk(jj), k))
    scratch = [pltpu.VMEM((Tc, W), F32), pltpu.VMEM((SEGS, W), F32)]
    sem = _params("parallel", "arbitrary")

    def project_in(in_ref, w_ref, st_ref):
        st_ref[...] = jnp.dot(in_ref[...].astype(BF16), w_ref[...], preferred_element_type=F32)

    if mode == "finals":
        def body(in_ref, w_ref, lam_ref, fin_ref, st_ref, carry_ref):
            jj = pl.program_id(1)

            @pl.when(jj == 0)
            def _():
                carry_ref[...] = jnp.zeros_like(carry_ref)

            project_in(in_ref, w_ref, st_ref)
            _scan_rows(st_ref, carry_ref, lam_ref, nt, rev)

            @pl.when(jj == nT - 1)
            def _():
                fin_ref[...] = carry_ref[...]

        return pl.pallas_call(
            body, name=name, out_shape=jax.ShapeDtypeStruct((K, SEGS, W), F32), grid=(K, nT),
            in_specs=[slab_in, wspec, vspec], out_specs=vspec, scratch_shapes=scratch, compiler_params=sem,
        )(inp, win, lam)

    if mode == "fwd":
        def body(in_ref, w_ref, lam_ref, init_ref, wo_ref, xs_ref, y_ref, st_ref, carry_ref):
            jj = pl.program_id(1)

            @pl.when(jj == 0)
            def _():
                carry_ref[...] = init_ref[...]

            project_in(in_ref, w_ref, st_ref)
            _scan_rows(st_ref, carry_ref, lam_ref, nt, rev)
            xs = st_ref[...]
            xs_ref[...] = xs
            y_ref[...] = jnp.dot(xs.astype(BF16), wo_ref[...], preferred_element_type=F32)

        res = carried_call(
            body, rider, name, (K, nT), [slab_in, wspec, vspec, vspec, wospec], [xspec, slab_in],
            [jax.ShapeDtypeStruct((T, K * W), F32), jax.ShapeDtypeStruct((T, C), F32)], scratch,
            [inp, win, lam, init, wout], ("parallel", "arbitrary"))
        return (res[0], res[1]) if rider is None else (res[0], res[1], res[2:])

    assert mode == "bwd"
    x_rev = not rev

    def halo_index(k, jj):
        ch = chunk(jj)
        tile = jnp.minimum((ch + 1) * nt, tiles - 1) if x_rev else jnp.maximum(ch * nt - 1, 0)
        return (tile, k)

    halo = pl.BlockSpec((SEGS, W), halo_index)

    def body(in_ref, w_ref, lam_ref, init_ref, wo_ref, xs_ref, xh_ref, xi_ref, u_ref,
             du_ref, dwin_ref, dwout_ref, dlam_ref, st_ref, carry_ref):
        jj = pl.program_id(1)
        ch = chunk(jj)

        @pl.when(jj == 0)
        def _():
            carry_ref[...] = init_ref[...]

        g = in_ref[...].astype(BF16)
        st_ref[...] = jnp.dot(g, w_ref[...], preferred_element_type=F32)
        _scan_rows(st_ref, carry_ref, lam_ref, nt, rev)
        adj = st_ref[...]
        adj16 = adj.astype(BF16)
        du_ref[...] = jnp.dot(adj16, wo_ref[...], preferred_element_type=F32)
        xs = xs_ref[...]
        edge = (ch == nT - 1) if x_rev else (ch == 0)
        first = jnp.where(edge, xi_ref[...], xh_ref[...])
        if x_rev:
            xp = jnp.concatenate([xs[SEGS:], first], axis=0)
        else:
            xp = jnp.concatenate([first, xs[:Tc - SEGS]], axis=0)
        h = SLAB_HALF
        ar, ai, pr, pi = adj[:, :h], adj[:, h:], xp[:, :h], xp[:, h:]
        dlr = (ar * pr + ai * pi).reshape(nt, SEGS, h).sum(axis=0)
        dli = (ai * pr - ar * pi).reshape(nt, SEGS, h).sum(axis=0)
        dwin = lax.dot_general(u_ref[...].astype(BF16), adj16, (((0,), (0,)), ((), ())), preferred_element_type=F32)
        dwout = lax.dot_general(xs.astype(BF16), g, (((0,), (0,)), ((), ())), preferred_element_type=F32)

        @pl.when(jj == 0)
        def _():
            dwin_ref[...] = dwin
            dwout_ref[...] = dwout
            dlam_ref[:, :h] = dlr
            dlam_ref[:, h:] = dli

        @pl.when(jj > 0)
        def _():
            dwin_ref[...] += dwin
            dwout_ref[...] += dwout
            dlam_ref[:, :h] += dlr
            dlam_ref[:, h:] += dli

    res = carried_call(
        body, rider, name, (K, nT), [slab_in, wspec, vspec, vspec, wospec, xspec, halo, vspec, slab_in],
        [slab_in, wspec, wospec, vspec],
        [jax.ShapeDtypeStruct((T, C), F32), jax.ShapeDtypeStruct((K, SLAB, W), F32),
         jax.ShapeDtypeStruct((K, W, SLAB), F32), jax.ShapeDtypeStruct((K, SEGS, W), F32)], scratch,
        [inp, win, lam, init, wout, xs, xs, xinit, u], ("parallel", "arbitrary"))
    return tuple(res[:4]) if rider is None else (res[0], res[1], res[2], res[3], res[4:])


def _s5_discretize(a_re, a_im, log_step, b_re, b_im):
    step = jnp.exp(log_step)[:, None]
    mag = jnp.exp(step * a_re)
    lb_re = mag * jnp.cos(step * a_im)
    lb_im = mag * jnp.sin(step * a_im)
    n_re, n_im = lb_re - 1.0, lb_im
    den = a_re * a_re + a_im * a_im
    coef_re = (n_re * a_re + n_im * a_im) / den
    coef_im = (n_im * a_re - n_re * a_im) / den
    bb_re = coef_re[..., None] * b_re - coef_im[..., None] * b_im
    bb_im = coef_re[..., None] * b_im + coef_im[..., None] * b_re
    return lb_re, lb_im, bb_re, bb_im


def _slab_in_matrix(bb_re, bb_im):
    G, N, Cg = bb_re.shape
    K = G // SLAB_GROUPS
    eye = jnp.eye(SLAB_GROUPS, dtype=F32)
    parts = [jnp.einsum('kgnc,gh->kgchn', b.reshape(K, SLAB_GROUPS, N, Cg), eye).reshape(K, SLAB, SLAB_HALF)
             for b in (bb_re, bb_im)]
    return jnp.concatenate(parts, axis=2)


def _slab_in_unpack(m):
    K = m.shape[0]
    m6 = m.reshape(K, SLAB_GROUPS, SSM_GROUP_CH, 2, SLAB_GROUPS, SSM_STATE)
    d = jnp.einsum('kgcphn,gh->pkgnc', m6, jnp.eye(SLAB_GROUPS, dtype=F32))
    d = d.reshape(2, K * SLAB_GROUPS, SSM_STATE, SSM_GROUP_CH)
    return d[0], d[1]


def _slab_out_matrix(c_re, c_im):
    G, Cg, N = c_re.shape
    K = G // SLAB_GROUPS
    eye = jnp.eye(SLAB_GROUPS, dtype=F32)
    parts = [jnp.einsum('kgcn,gh->kgnhc', c.reshape(K, SLAB_GROUPS, Cg, N), eye).reshape(K, SLAB_HALF, SLAB)
             for c in (c_re, -c_im)]
    return jnp.concatenate(parts, axis=1)


def _slab_out_unpack(m):
    K = m.shape[0]
    m6 = m.reshape(K, 2, SLAB_GROUPS, SSM_STATE, SLAB_GROUPS, SSM_GROUP_CH)
    d = jnp.einsum('kpgnhc,gh->pkgcn', m6, jnp.eye(SLAB_GROUPS, dtype=F32))
    d = d.reshape(2, K * SLAB_GROUPS, SSM_GROUP_CH, SSM_STATE)
    return d[0], -d[1]


def _slab_vec(re, im):
    K = re.shape[0] // SLAB_GROUPS
    v = jnp.concatenate([re.reshape(K, SLAB_HALF), im.reshape(K, SLAB_HALF)], axis=1)
    return jnp.broadcast_to(v[:, None, :], (K, SEGS, SLAB_W))


def _segment_inits(fin, lam, seg_len, rev):
    h = SLAB_HALF
    pr, pi = lam[:, 0, :h], lam[:, 0, h:]
    steps = int(round(math.log2(seg_len)))
    assert 2 ** steps == seg_len
    for _ in range(steps):
        pr, pi = pr * pr - pi * pi, 2.0 * pr * pi
    cr = jnp.zeros_like(pr)
    ci = jnp.zeros_like(pi)
    inits = [None] * SEGS
    for s in (range(SEGS - 1, -1, -1) if rev else range(SEGS)):
        inits[s] = jnp.concatenate([cr, ci], axis=1)
        cr, ci = pr * cr - pi * ci + fin[:, s, :h], pr * ci + pi * cr + fin[:, s, h:]
    return jnp.stack(inits, axis=1)


def _time_permute(x):
    T, C = x.shape
    return x.reshape(SEGS, T // SEGS, C).transpose(1, 0, 2).reshape(T, C)


def _time_unpermute(x):
    T, C = x.shape
    return x.reshape(T // SEGS, SEGS, C).transpose(1, 0, 2).reshape(T, C)


_GELU_K = math.sqrt(2.0 / math.pi)
_GELU_A = 0.044715


def _gelu_grad(y):
    t = jnp.tanh(_GELU_K * (y + _GELU_A * y * y * y))
    return 0.5 * (1.0 + t) + 0.5 * y * (1.0 - t * t) * (_GELU_K * (1.0 + 3.0 * _GELU_A * y * y))


def _conj(lam):
    return jnp.concatenate([lam[:, :, :SLAB_HALF], -lam[:, :, SLAB_HALF:]], axis=2)


def s5_layer_fwd(x, norm_g, ssm, w_glu, tag, rider=None):
    S, D = x.shape
    u_nat, rstd = rms_fwd(x, norm_g, F32, f"s5_norm_{tag}")
    u = _time_permute(u_nat)
    dirs = []
    ys = []
    received = None
    for dr in range(2):
        rev = dr == 1
        lb_re, lb_im, bb_re, bb_im = _s5_discretize(ssm["a_re"][dr], ssm["a_im"][dr], ssm["log_step"][dr],
                                                    ssm["b_re"][dr], ssm["b_im"][dr])
        win = _slab_in_matrix(bb_re, bb_im).astype(BF16)
        wout = _slab_out_matrix(ssm["c_re"][dr], ssm["c_im"][dr]).astype(BF16)
        lam = _slab_vec(lb_re, lb_im)
        fin = s5_scan("finals", u, win, lam, rev, f"s5_fin{dr}_{tag}")
        init = _segment_inits(fin, lam, S // SEGS, rev)
        xs, y, *got = s5_scan("fwd", u, win, lam, rev, f"s5_fwd{dr}_{tag}", init=init, wout=wout,
                              rider=rider if dr == 0 else None)
        received = got[0] if got else received
        dirs.append((win, wout, lam, init, xs))
        ys.append(y)
    yy, zb = ew(lambda uu, a, b, d: (d * uu + a + b, jax.nn.gelu(d * uu + a + b)), f"s5_y_{tag}",
                [u, ys[0], ys[1]], [ssm["d"]], [F32, BF16])
    lin = mm_nn(zb, w_glu, F32, f"s5_glu_{tag}", tn_cap=512)
    mix = ew(lambda y_, l_, b: jax.nn.gelu(y_) * jax.nn.sigmoid(l_ + b), f"s5_mix_{tag}",
             [yy, lin], [ssm["b_glu"]], [F32])[0]
    out = x + _time_unpermute(mix)
    saved = (x, rstd, u, dirs, yy, zb, lin)
    return (out, saved) if rider is None else (out, saved, received)


def s5_layer_bwd(dy, saved, norm_g, ssm, w_glu, tag, rider=None):
    x, rstd, u, dirs, yy, zb, lin = saved
    S, D = x.shape
    dmix = _time_permute(dy)

    def glu_back(dm, y_, l_, b):
        z = jax.nn.gelu(y_)
        sg = jax.nn.sigmoid(l_ + b)
        dlin = dm * z * (sg * (1.0 - sg))
        return dlin, dm * sg, dlin

    dlin, dz_direct, db_glu = ew(glu_back, f"s5_dmix_{tag}", [dmix, yy, lin], [ssm["b_glu"]], [BF16, F32], n_sums=1)
    dw_glu = mm_tn(zb, dlin, 1, BF16, f"s5_dwglu_{tag}", tn_cap=512)
    dz_mm = mm_nt(dlin, w_glu, F32, f"s5_dz_{tag}")

    def gelu_back(a, b, y_, uu):
        dyy = (a + b) * _gelu_grad(y_)
        return dyy, dyy * uu

    dyy, dd = ew(gelu_back, f"s5_dy_{tag}", [dz_direct, dz_mm, yy, u], [], [F32], n_sums=1)
    grads = {"d": dd, "b_glu": db_glu, "w_glu": dw_glu}
    dus = []
    per_dir = []
    received = None
    for dr in range(2):
        rev = dr == 1
        win, wout, lam, xinit, xs = dirs[dr]
        lamc = _conj(lam)
        ein = wout.transpose(0, 2, 1)
        eout = win.transpose(0, 2, 1)
        fin = s5_scan("finals", dyy, ein, lamc, not rev, f"s5_bfin{dr}_{tag}")
        init = _segment_inits(fin, lamc, S // SEGS, not rev)
        du, dwin, dwout, dlam, *got = s5_scan("bwd", dyy, ein, lamc, not rev, f"s5_bwd{dr}_{tag}", init=init,
                                              wout=eout, xs=xs, xinit=xinit, u=u, rider=rider if dr == 0 else None)
        received = got[0] if got else received
        dus.append(du)
        dbb_re, dbb_im = _slab_in_unpack(dwin)
        dc_re, dc_im = _slab_out_unpack(dwout)
        dl = dlam.sum(axis=1)
        dlb_re = dl[:, :SLAB_HALF].reshape(-1, SSM_STATE)
        dlb_im = dl[:, SLAB_HALF:].reshape(-1, SSM_STATE)
        prm = (ssm["a_re"][dr], ssm["a_im"][dr], ssm["log_step"][dr], ssm["b_re"][dr], ssm["b_im"][dr])
        _, vjp = jax.vjp(_s5_discretize, *prm)
        per_dir.append(vjp((dlb_re, dlb_im, dbb_re, dbb_im)) + (dc_re, dc_im))
    for i, nm in enumerate(["a_re", "a_im", "log_step", "b_re", "b_im", "c_re", "c_im"]):
        grads[nm] = jnp.stack([per_dir[0][i], per_dir[1][i]], axis=0)
    du_p = ew(lambda g, a, b, d: d * g + a + b, f"s5_du_{tag}", [dyy, dus[0], dus[1]], [ssm["d"]], [F32])[0]
    dx, dnorm = rms_bwd(_time_unpermute(du_p), x, rstd, norm_g, f"s5_dnorm_{tag}", dres=dy)
    grads["norm"] = dnorm
    return (dx, grads) if rider is None else (dx, grads, received)


def final_loss(x, g, target, name, ts=512):
    S, D = x.shape
    ts = _divisor(S, ts, 16)

    def body(x_ref, g_ref, t_ref, loss_ref, dx_ref, dg_ref):
        i = pl.program_id(0)
        x = x_ref[...]
        gg = g_ref[...]
        r = lax.rsqrt(jnp.mean(x * x, axis=-1, keepdims=True) + RMS_EPS)
        xhat = x * r
        err = xhat * gg - t_ref[...]
        row_loss = jnp.mean(err * err, axis=-1, keepdims=True)
        part = jnp.broadcast_to(0.5 * jnp.sum(row_loss, axis=0, keepdims=True), (1, LANES))
        dy = err * (1.0 / D)
        dhg = dy * gg
        c = jnp.mean(dhg * xhat, axis=-1, keepdims=True)
        dx_ref[...] = r * (dhg - xhat * c)
        dg = jnp.sum(dy * xhat, axis=0, keepdims=True)

        @pl.when(i == 0)
        def _():
            loss_ref[...] = part
            dg_ref[...] = dg

        @pl.when(i > 0)
        def _():
            loss_ref[...] += part
            dg_ref[...] += dg

    row = pl.BlockSpec((ts, D), lambda i: (i, 0))
    vec = pl.BlockSpec((1, D), lambda i: (0, 0))
    return pl.pallas_call(
        body, name=name,
        out_shape=(jax.ShapeDtypeStruct((1, LANES), F32), jax.ShapeDtypeStruct((S, D), F32),
                   jax.ShapeDtypeStruct((1, D), F32)),
        grid=(S // ts,), in_specs=[row, vec, row],
        out_specs=(pl.BlockSpec((1, LANES), lambda i: (0, 0)), row, vec),
        compiler_params=_params("arbitrary"),
    )(x, g, target)


FLAT_W = 8 * LANES


def _adamw_math(w, g, m, v):
    m = ADAM_B1 * m + (1.0 - ADAM_B1) * g
    v = ADAM_B2 * v + (1.0 - ADAM_B2) * (g * g)
    m_hat = m / (1.0 - ADAM_B1 ** ADAM_STEP)
    v_hat = v / (1.0 - ADAM_B2 ** ADAM_STEP)
    delta = -ADAM_LR * (m_hat / (jnp.sqrt(v_hat) + ADAM_EPS) + ADAM_WD * w)
    return delta, m, v


def _ordered_sum(parts_ref):
    total = parts_ref[0].astype(F32)
    for s in range(1, N_DEV):
        total = total + parts_ref[s].astype(F32)
    return total


ADAMW_BLOCK_ELEMS = 256 * 1024


def adamw_from_parts(parts, w, m, v, name):
    R, B = w.shape
    tr = _divisor(R, max(16, ADAMW_BLOCK_ELEMS // B), 16)

    def body(p_ref, w_ref, m_ref, v_ref, g_ref, d_ref, nm_ref, nv_ref):
        g = _ordered_sum(p_ref)
        delta, nm, nv = _adamw_math(w_ref[...], g, m_ref[...], v_ref[...])
        g_ref[...] = g
        d_ref[...] = delta
        nm_ref[...] = nm
        nv_ref[...] = nv

    flat = pl.BlockSpec((tr, B), lambda i: (i, 0))
    out = jax.ShapeDtypeStruct((R, B), F32)
    return pl.pallas_call(
        body, name=name, out_shape=(out, out, out, out), grid=(R // tr,),
        in_specs=[pl.BlockSpec((N_DEV, tr, B), lambda i: (0, i, 0)), flat, flat, flat],
        out_specs=(flat, flat, flat, flat), compiler_params=_params("parallel"),
    )(parts, w, m, v)


def sum_parts(parts, name, tr=512):
    R = parts.shape[1]
    tr = _divisor(R, tr, 16)

    def body(p_ref, o_ref):
        o_ref[...] = _ordered_sum(p_ref)

    return pl.pallas_call(
        body, name=name, out_shape=jax.ShapeDtypeStruct((R, FLAT_W), F32), grid=(R // tr,),
        in_specs=[pl.BlockSpec((N_DEV, tr, FLAT_W), lambda i: (0, i, 0))],
        out_specs=pl.BlockSpec((tr, FLAT_W), lambda i: (i, 0)), compiler_params=_params("parallel"),
    )(parts)


def adamw_flat(g, w, m, v, name, tr=512):
    R = w.shape[0]
    tr = _divisor(R, tr, 16)

    def body(g_ref, w_ref, m_ref, v_ref, d_ref, nm_ref, nv_ref):
        delta, nm, nv = _adamw_math(w_ref[...], g_ref[...], m_ref[...], v_ref[...])
        d_ref[...] = delta
        nm_ref[...] = nm
        nv_ref[...] = nv

    flat = pl.BlockSpec((tr, FLAT_W), lambda i: (i, 0))
    out = jax.ShapeDtypeStruct((R, FLAT_W), F32)
    return pl.pallas_call(
        body, name=name, out_shape=(out, out, out), grid=(R // tr,),
        in_specs=[flat, flat, flat, flat], out_specs=(flat, flat, flat), compiler_params=_params("parallel"),
    )(g, w, m, v)


MESH_ID = pl.DeviceIdType.MESH
HBM_SPEC = pl.BlockSpec(memory_space=pltpu.HBM)


def _position():
    x, y, c = lax.axis_index("x"), lax.axis_index("y"), lax.axis_index("c")
    return x, y, c


def _flat_index(px, py, pc):
    return 4 * px + 2 * py + pc


def all_gather(arrays, axes, name):
    n = len(arrays)

    def body(*refs):
        ins, outs = refs[:n], refs[n:2 * n]
        send_sems, recv_sems, local_sems = refs[2 * n:]
        x, y, c = _position()
        me, sibling = (x, y, c), (x, y, 1 - c)
        chips = [(1 - x, y), (x, 1 - y), (1 - x, 1 - y)]

        def block_of(a, pos):
            idx = _flat_index(*pos)
            return outs[a].at[:, idx] if axes[a] == 1 else outs[a].at[idx]

        def copy(a, k, block, to, src=None):
            rows = block_of(a, block)
            return pltpu.make_async_remote_copy(
                src_ref=rows if src is None else src, dst_ref=rows,
                send_sem=send_sems.at[7 * a + k], recv_sem=recv_sems.at[7 * a + k],
                device_id=to, device_id_type=MESH_ID)

        mine, first, passed = [], [], []
        for a in range(n):
            cp = pltpu.make_async_copy(ins[a], block_of(a, me), local_sems.at[a])
            cp.start()
            mine.append(cp)
            first.append(copy(a, 0, me, sibling, src=ins[a]))
            first += [copy(a, 1 + j, me, (*chip, c), src=ins[a]) for j, chip in enumerate(chips)]
        for cp in first:
            cp.start()
        for a in range(n):
            for j, chip in enumerate(chips):
                copy(a, 1 + j, (*chip, c), me).wait_recv()
                fwd = copy(a, 4 + j, (*chip, c), sibling)
                fwd.start()
                passed.append(fwd)
        for a in range(n):
            copy(a, 0, sibling, me).wait_recv()
            for j, chip in enumerate(chips):
                copy(a, 4 + j, (*chip, 1 - c), me).wait_recv()
        for cp in first + passed:
            cp.wait_send()
        for cp in mine:
            cp.wait()

    return pl.pallas_call(
        body, name=name,
        out_shape=tuple(jax.ShapeDtypeStruct(a.shape[:ax] + (N_DEV,) + a.shape[ax:], a.dtype)
                        for a, ax in zip(arrays, axes)),
        in_specs=[HBM_SPEC] * n, out_specs=tuple([HBM_SPEC] * n),
        scratch_shapes=[pltpu.SemaphoreType.DMA((7 * n,)), pltpu.SemaphoreType.DMA((7 * n,)),
                        pltpu.SemaphoreType.DMA((n,))],
    )(*arrays)


def exchange(slotted, whole, name):
    n = len(slotted) + 1

    def body(*refs):
        srcs, dsts = refs[:n], refs[n:2 * n]
        send_sems, recv_sems, local_sems = refs[2 * n:]
        x, y, c = _position()
        me = _flat_index(x, y, c)

        def source(a, slot):
            return srcs[a].at[slot] if a < n - 1 else srcs[a]

        own = [pltpu.make_async_copy(source(a, me), dsts[a].at[me], local_sems.at[a]) for a in range(n)]
        for cp in own:
            cp.start()
        sends, recvs = [], []
        for r in range(1, N_DEV):
            peer = (1 - x if r & 4 else x, 1 - y if r & 2 else y, 1 - c if r & 1 else c)
            pidx = _flat_index(*peer)
            for a in range(n):
                k = 7 * a + r - 1
                sends.append(pltpu.make_async_remote_copy(
                    src_ref=source(a, pidx), dst_ref=dsts[a].at[me], send_sem=send_sems.at[k],
                    recv_sem=recv_sems.at[k], device_id=peer, device_id_type=MESH_ID))
                recvs.append(pltpu.make_async_remote_copy(
                    src_ref=source(a, pidx), dst_ref=dsts[a].at[pidx], send_sem=send_sems.at[k],
                    recv_sem=recv_sems.at[k], device_id=peer, device_id_type=MESH_ID))
        for cp in sends:
            cp.start()
        for cp in recvs:
            cp.wait_recv()
        for cp in sends:
            cp.wait_send()
        for cp in own:
            cp.wait()

    return pl.pallas_call(
        body, name=name,
        out_shape=tuple(jax.ShapeDtypeStruct(s.shape, s.dtype) for s in slotted)
        + (jax.ShapeDtypeStruct((N_DEV,) + whole.shape, whole.dtype),),
        in_specs=[HBM_SPEC] * n, out_specs=tuple([HBM_SPEC] * n),
        scratch_shapes=[pltpu.SemaphoreType.DMA((7 * n,)), pltpu.SemaphoreType.DMA((7 * n,)),
                        pltpu.SemaphoreType.DMA((n,))],
    )(*slotted, whole)


class RidingGather:
    def __init__(self, arrays):
        self.arrays = list(arrays)
        self.n = len(self.arrays)
        self.out_shapes = [jax.ShapeDtypeStruct((N_DEV,) + a.shape, a.dtype) for a in self.arrays]
        self.sems = [pltpu.SemaphoreType.DMA((7 * self.n,)), pltpu.SemaphoreType.DMA((7 * self.n,)),
                     pltpu.SemaphoreType.DMA((self.n,))]

    def _copies(self, ins, outs, sems):
        send_sems, recv_sems, local_sems = sems
        x, y, c = _position()
        me, sibling = (x, y, c), (x, y, 1 - c)
        chips = [(1 - x, y), (x, 1 - y), (1 - x, 1 - y)]

        def copy(a, k, block, to, src=None):
            rows = outs[a].at[_flat_index(*block)]
            return pltpu.make_async_remote_copy(
                src_ref=rows if src is None else src, dst_ref=rows,
                send_sem=send_sems.at[7 * a + k], recv_sem=recv_sems.at[7 * a + k],
                device_id=to, device_id_type=MESH_ID)

        mine = [pltpu.make_async_copy(ins[a], outs[a].at[_flat_index(*me)], local_sems.at[a]) for a in range(self.n)]
        first = []
        for a in range(self.n):
            first.append(copy(a, 0, me, sibling, src=ins[a]))
            first += [copy(a, 1 + j, me, (*chip, c), src=ins[a]) for j, chip in enumerate(chips)]
        return copy, mine, first, me, sibling, chips, c

    def start(self, ins, outs, sems):
        _, mine, first, *_ = self._copies(ins, outs, sems)
        for cp in mine + first:
            cp.start()

    def finish(self, ins, outs, sems):
        copy, mine, first, me, sibling, chips, c = self._copies(ins, outs, sems)
        passed = []
        for a in range(self.n):
            for j, chip in enumerate(chips):
                copy(a, 1 + j, (*chip, c), me).wait_recv()
                fwd = copy(a, 4 + j, (*chip, c), sibling)
                fwd.start()
                passed.append(fwd)
        for a in range(self.n):
            copy(a, 0, sibling, me).wait_recv()
            for j, chip in enumerate(chips):
                copy(a, 4 + j, (*chip, 1 - c), me).wait_recv()
        for cp in first + passed:
            cp.wait_send()
        for cp in mine:
            cp.wait()


class RidingExchange:
    def __init__(self, arrays):
        self.arrays = list(arrays)
        self.n = len(self.arrays)
        self.out_shapes = [jax.ShapeDtypeStruct(a.shape, a.dtype) for a in self.arrays]
        self.sems = [pltpu.SemaphoreType.DMA((7 * self.n,)), pltpu.SemaphoreType.DMA((7 * self.n,)),
                     pltpu.SemaphoreType.DMA((self.n,))]

    def _copies(self, ins, outs, sems):
        send_sems, recv_sems, local_sems = sems
        x, y, c = _position()
        me = _flat_index(x, y, c)
        own = [pltpu.make_async_copy(ins[a].at[me], outs[a].at[me], local_sems.at[a]) for a in range(self.n)]
        sends, recvs = [], []
        for r in range(1, N_DEV):
            peer = (1 - x if r & 4 else x, 1 - y if r & 2 else y, 1 - c if r & 1 else c)
            pidx = _flat_index(*peer)
            for a in range(self.n):
                k = 7 * a + r - 1
                sends.append(pltpu.make_async_remote_copy(
                    src_ref=ins[a].at[pidx], dst_ref=outs[a].at[me], send_sem=send_sems.at[k],
                    recv_sem=recv_sems.at[k], device_id=peer, device_id_type=MESH_ID))
                recvs.append(pltpu.make_async_remote_copy(
                    src_ref=ins[a].at[pidx], dst_ref=outs[a].at[pidx], send_sem=send_sems.at[k],
                    recv_sem=recv_sems.at[k], device_id=peer, device_id_type=MESH_ID))
        return own, sends, recvs

    def start(self, ins, outs, sems):
        own, sends, _ = self._copies(ins, outs, sems)
        for cp in own + sends:
            cp.start()

    def finish(self, ins, outs, sems):
        own, sends, recvs = self._copies(ins, outs, sems)
        for cp in recvs:
            cp.wait_recv()
        for cp in sends:
            cp.wait_send()
        for cp in own:
            cp.wait()


def carried_call(body, rider, name, grid, in_specs, out_specs, out_shape, scratch_shapes, args, semantics):
    in_specs, out_specs, out_shape = list(in_specs), list(out_specs), list(out_shape)
    scratch_shapes, args = list(scratch_shapes), list(args)
    if rider is None:
        return pl.pallas_call(body, name=name, grid=grid, in_specs=in_specs, out_specs=tuple(out_specs),
                              out_shape=tuple(out_shape), scratch_shapes=scratch_shapes,
                              compiler_params=_params(*semantics))(*args)
    n_in, n_out, n_scr, n = len(in_specs), len(out_specs), len(scratch_shapes), rider.n

    def riding(*refs):
        ins, srcs = refs[:n_in], refs[n_in:n_in + n]
        outs, dsts = refs[n_in + n:n_in + n + n_out], refs[n_in + n + n_out:n_in + 2 * n + n_out]
        scr = refs[n_in + 2 * n + n_out:]
        scratch, sems = scr[:n_scr], scr[n_scr:]
        first = functools.reduce(jnp.logical_and, [pl.program_id(d) == 0 for d in range(len(grid))])
        last = functools.reduce(jnp.logical_and, [pl.program_id(d) == g - 1 for d, g in enumerate(grid)])

        @pl.when(first)
        def _():
            rider.start(srcs, dsts, sems)

        body(*ins, *outs, *scratch)

        @pl.when(last)
        def _():
            rider.finish(srcs, dsts, sems)

    return pl.pallas_call(
        riding, name=name, grid=grid, in_specs=in_specs + [HBM_SPEC] * n,
        out_specs=tuple(out_specs + [HBM_SPEC] * n), out_shape=tuple(out_shape + rider.out_shapes),
        scratch_shapes=scratch_shapes + rider.sems,
        compiler_params=_params(*["arbitrary"] * len(grid)))(*args, *rider.arrays)


WEIGHTS = ['mix_norm', 'ffn_norm', 'final_norm', 'attn_w_qkv', 'attn_w_o', 'attn_sink', 'ssm_a_re', 'ssm_a_im',
           'ssm_log_step', 'ssm_b_re', 'ssm_b_im', 'ssm_c_re', 'ssm_c_im', 'ssm_d', 'ssm_w_glu', 'ssm_b_glu',
           'mla_w_dqkv', 'mla_q_norm', 'mla_kv_norm', 'mla_w_uq', 'mla_w_ukv', 'mla_w_o', 'ffn_w_up', 'ffn_conv_w',
           'ffn_conv_b', 'ffn_w_down']
BIG = [('attn_w_qkv', 'col'), ('attn_w_o', 'row'), ('ssm_w_glu', 'row'), ('mla_w_dqkv', 'row'), ('mla_w_uq', 'col'),
       ('mla_w_ukv', 'col'), ('mla_w_o', 'row'), ('ffn_w_up', 'col'), ('ffn_w_down', 'row')]
BIG_NAMES = [n for n, _ in BIG]
MIXER_WEIGHTS = {0: ['attn_w_qkv', 'attn_w_o'], 1: ['ssm_w_glu'],
                 2: ['mla_w_dqkv', 'mla_w_uq', 'mla_w_ukv', 'mla_w_o']}
FFN_WEIGHTS = ['ffn_w_up', 'ffn_w_down']
SMALL_SHARDED = ['mla_q_norm', 'mla_kv_norm', 'ffn_conv_w']
SMALL = [n for n in WEIGHTS if n not in BIG_NAMES]


def _pack(arrays, dtype):
    flat = jnp.concatenate([a.astype(dtype).reshape(-1) for a in arrays])
    pad = (-flat.shape[0]) % (16 * FLAT_W)
    if pad:
        flat = jnp.concatenate([flat, jnp.zeros((pad,), dtype)])
    return flat.reshape(-1, FLAT_W)


def _unpack(flat, shapes):
    flat = flat.reshape(-1)
    out, off = [], 0
    for s in shapes:
        n = int(np.prod(s))
        out.append(flat[off:off + n].reshape(s))
        off += n
    return out


def _own_slice(full, idx):
    n = full.shape[-1] // N_DEV
    return lax.dynamic_slice_in_dim(full, idx * n, n, axis=full.ndim - 1)


def kernel(x, mix_norm, ffn_norm, final_norm, attn_w_qkv, attn_w_o, attn_sink, ssm_a_re, ssm_a_im, ssm_log_step, ssm_b_re, ssm_b_im, ssm_c_re, ssm_c_im, ssm_d, ssm_w_glu, ssm_b_glu, mla_w_dqkv, mla_q_norm, mla_kv_norm, mla_w_uq, mla_w_ukv, mla_w_o, ffn_w_up, ffn_conv_w, ffn_conv_b, ffn_w_down, loss_target, m_mix_norm, m_ffn_norm, m_final_norm, m_attn_w_qkv, m_attn_w_o, m_attn_sink, m_ssm_a_re, m_ssm_a_im, m_ssm_log_step, m_ssm_b_re, m_ssm_b_im, m_ssm_c_re, m_ssm_c_im, m_ssm_d, m_ssm_w_glu, m_ssm_b_glu, m_mla_w_dqkv, m_mla_q_norm, m_mla_kv_norm, m_mla_w_uq, m_mla_w_ukv, m_mla_w_o, m_ffn_w_up, m_ffn_conv_w, m_ffn_conv_b, m_ffn_w_down, v_mix_norm, v_ffn_norm, v_final_norm, v_attn_w_qkv, v_attn_w_o, v_attn_sink, v_ssm_a_re, v_ssm_a_im, v_ssm_log_step, v_ssm_b_re, v_ssm_b_im, v_ssm_c_re, v_ssm_c_im, v_ssm_d, v_ssm_w_glu, v_ssm_b_glu, v_mla_w_dqkv, v_mla_q_norm, v_mla_kv_norm, v_mla_w_uq, v_mla_w_ukv, v_mla_w_o, v_ffn_w_up, v_ffn_conv_w, v_ffn_conv_b, v_ffn_w_down):
    given = dict(locals())
    idx = _flat_index(*_position())
    depth = mix_norm.shape[0]
    xs = x[0]
    S, D = xs.shape

    kinds = dict(BIG)

    def layer_weights(i):
        return [(n, i // 3) for n in MIXER_WEIGHTS[i % 3]] + [(n, i) for n in FFN_WEIGHTS]

    def shard(name, j):
        return given[name][j].astype(BF16)

    def whole_weight(name, g):
        a, b = g.shape[1:]
        if kinds[name] == 'row':
            return (g.reshape(1, 1, N_DEV * a, b), 0)
        if b % LANES:
            return (g.transpose(1, 0, 2).reshape(1, 1, a, N_DEV * b), 0)
        return (g[None], 0)

    small_flat = _pack([given[n] for n in SMALL_SHARDED], F32)
    first = [(n, 0) for n in MIXER_WEIGHTS[0]]
    gathered = all_gather([shard(n, j) for n, j in first] + [small_flat], [0] * (len(first) + 1), "gather_l0")
    W = {0: {n: whole_weight(n, g) for (n, _), g in zip(first, gathered)}}
    sm = _unpack_rows(gathered[-1], [given[n].shape for n in SMALL_SHARDED])
    q_norm_full = sm[0][:, 0].reshape(1, -1)
    kv_norm_full = sm[1][:, 0].reshape(1, -1)
    conv_w_full = sm[2].transpose(1, 2, 0, 3).reshape(depth, 3, -1)
    F = conv_w_full.shape[2] // 2

    def conv_params(i):
        cw = conv_w_full[i].reshape(3, 2, F).transpose(1, 0, 2)
        cb = ffn_conv_b[i].reshape(2, 1, F)
        return cw, cb

    ssm = lambda j: {"a_re": ssm_a_re[j], "a_im": ssm_a_im[j], "log_step": ssm_log_step[j], "b_re": ssm_b_re[j],
                     "b_im": ssm_b_im[j], "c_re": ssm_c_re[j], "c_im": ssm_c_im[j], "d": ssm_d[j][None],
                     "b_glu": ssm_b_glu[j][None]}

    def mixer_fwd(i, cur, rider=None):
        kind, j, tag, w = i % 3, i // 3, f"l{i}", W[i]
        if kind == 0:
            return swa_layer_fwd(cur, mix_norm[i][None], w['attn_w_qkv'], w['attn_w_o'], attn_sink[j], tag,
                                 rider=rider)
        if kind == 1:
            return s5_layer_fwd(cur, mix_norm[i][None], ssm(j), w['ssm_w_glu'], tag)
        return mla_layer_fwd(cur, mix_norm[i][None], w['mla_w_dqkv'], q_norm_full, kv_norm_full, w['mla_w_uq'],
                             w['mla_w_ukv'], w['mla_w_o'], tag)

    cur = xs
    saved = []
    for i in range(depth):
        if i == 0:
            ffn0 = [(n, 0) for n in FFN_WEIGHTS]
            cur, sv, got = mixer_fwd(0, cur, RidingGather([shard(n, j) for n, j in ffn0]))
            W[0].update({n: whole_weight(n, g) for (n, _), g in zip(ffn0, got)})
        else:
            cur, sv = mixer_fwd(i, cur)
        cw, cb = conv_params(i)
        nxt = layer_weights(i + 1) if i + 1 < depth else []
        rider = RidingGather([shard(n, j) for n, j in nxt]) if nxt else None
        res = ffn_fwd(cur, ffn_norm[i][None], W[i]['ffn_w_up'], cw, cb, W[i]['ffn_w_down'], f"l{i}", rider=rider)
        cur, fsv = res[0], res[1]
        if nxt:
            W[i + 1] = {n: whole_weight(n, g) for (n, _), g in zip(nxt, res[2])}
        saved.append((sv, fsv))
    loss_part, dcur, dfinal = final_loss(cur, final_norm[None], loss_target[0], "loss_head")

    recv = {n: [None] * given[n].shape[0] for n in BIG_NAMES}
    gs = {n: [None] * given[n].shape[0] for n in SMALL if given[n].ndim > 1}
    gs['final_norm'] = dfinal[0]

    def blocked(g, name):
        a, b = given[name].shape[1:]
        if kinds[name] == 'row':
            return g.reshape(N_DEV, a, b)
        if b % LANES:
            return g.reshape(a, N_DEV, b).transpose(1, 0, 2)
        return g

    def mixer_bwd(i, dcur, sv, rider):
        kind, j, tag, w = i % 3, i // 3, f"l{i}", W[i]
        if kind == 0:
            dcur, (dn, dwqkv, dwo, dsink), got = swa_layer_bwd(dcur, sv, mix_norm[i][None], w['attn_w_qkv'],
                                                               w['attn_w_o'], tag, rider=rider)
            gs['attn_sink'][j] = dsink
            big = [('attn_w_qkv', j, dwqkv), ('attn_w_o', j, dwo)]
        elif kind == 1:
            dcur, g5, got = s5_layer_bwd(dcur, sv, mix_norm[i][None], ssm(j), w['ssm_w_glu'], tag, rider=rider)
            dn = g5["norm"]
            for nm in ("a_re", "a_im", "log_step", "b_re", "b_im", "c_re", "c_im"):
                gs['ssm_' + nm][j] = g5[nm]
            gs['ssm_d'][j] = g5["d"][0]
            gs['ssm_b_glu'][j] = g5["b_glu"][0]
            big = [('ssm_w_glu', j, g5["w_glu"])]
        else:
            dcur, (dn, dwd, dqn, dkn, dwuq, dwukv, dwo), got = mla_layer_bwd(
                dcur, sv, mix_norm[i][None], w['mla_w_dqkv'], q_norm_full, kv_norm_full, w['mla_w_uq'],
                w['mla_w_ukv'], w['mla_w_o'], tag, rider=rider)
            gs['mla_q_norm'][j] = dqn[0]
            gs['mla_kv_norm'][j] = dkn[0]
            big = [('mla_w_dqkv', j, dwd), ('mla_w_uq', j, dwuq), ('mla_w_ukv', j, dwukv), ('mla_w_o', j, dwo)]
        gs['mix_norm'][i] = dn[0]
        return dcur, [(n, jj, blocked(g, n)) for n, jj, g in big], got

    pending = []
    early = [n for n in SMALL if n.startswith('ssm_')]
    early_parts = None
    for i in reversed(range(depth)):
        sv, fsv = saved[i]
        cw, cb = conv_params(i)
        arrays = [g for _, _, g in pending]
        rides_small = any(n == 'ssm_w_glu' for n, _, _ in pending)
        if rides_small:
            flat = _pack([jnp.stack(gs[n], axis=0) for n in early], F32)
            arrays.append(jnp.broadcast_to(flat[None], (N_DEV,) + flat.shape))
        rider = RidingExchange(arrays) if arrays else None
        res = ffn_bwd(dcur, fsv, ffn_norm[i][None], W[i]['ffn_w_up'], cw, cb, W[i]['ffn_w_down'], f"l{i}",
                      rider=rider)
        dcur, (dn, dwup, dcw, dcb, dwdn) = res[0], res[1]
        for (n, jj, _), got in zip(pending, res[2] if pending else []):
            recv[n][jj] = got
        if rides_small:
            early_parts = res[2][-1]
        gs['ffn_norm'][i] = dn[0]
        gs['ffn_conv_w'][i] = dcw.transpose(1, 0, 2).reshape(3, 2 * F)
        gs['ffn_conv_b'][i] = dcb.reshape(2 * F)
        rider = RidingExchange([blocked(dwup, 'ffn_w_up'), blocked(dwdn, 'ffn_w_down')])
        dcur, pending, got = mixer_bwd(i, dcur, sv, rider)
        recv['ffn_w_up'][i], recv['ffn_w_down'][i] = got
    grad_x = dcur[None]

    full = {n: gs[n] if n == 'final_norm' else jnp.stack(gs[n], axis=0) for n in SMALL}
    late = [n for n in SMALL if early_parts is None or n not in early]
    whole = _pack([loss_part[0, :1]] + [full[n] for n in late], F32)
    received = exchange([g for _, _, g in pending], whole, "exchange_last")
    for (n, jj, _), got in zip(pending, received):
        recv[n][jj] = got
    summed = _unpack(sum_parts(received[-1], "sum_small"), [(1,)] + [full[n].shape for n in late])
    loss = summed[0][0]
    total = dict(zip(late, summed[1:]))
    if early_parts is not None:
        total.update(zip(early, _unpack(sum_parts(early_parts, "sum_small_early"), [full[n].shape for n in early])))

    out = {}
    for n in BIG_NAMES:
        shape = given[n].shape
        rows = (shape[0] * shape[1], shape[2])
        parts = recv[n][0] if shape[0] == 1 else jnp.stack(recv[n], axis=1)
        res = adamw_from_parts(parts.reshape((N_DEV,) + rows), given[n].reshape(rows), given['m_' + n].reshape(rows),
                               given['v_' + n].reshape(rows), f"adamw_{n}")
        for key, val in zip(("grad", "delta", "new_m", "new_v"), res):
            out[key, n] = val.reshape(shape)
    small_grads = [_own_slice(total[n], idx) if n in SMALL_SHARDED else total[n] for n in SMALL]
    small_shapes = [given[n].shape for n in SMALL]
    d_s, nm_s, nv_s = adamw_flat(_pack(small_grads, F32), _pack([given[n] for n in SMALL], F32),
                                 _pack([given['m_' + n] for n in SMALL], F32),
                                 _pack([given['v_' + n] for n in SMALL], F32), "adamw_small")
    for n, g in zip(SMALL, small_grads):
        out["grad", n] = g
    for key, flat in (("delta", d_s), ("new_m", nm_s), ("new_v", nv_s)):
        for n, val in zip(SMALL, _unpack(flat, small_shapes)):
            out[key, n] = val
    return (loss, grad_x, *[out["grad", n] for n in WEIGHTS], *[out["delta", n] for n in WEIGHTS],
            *[out["new_m", n] for n in WEIGHTS], *[out["new_v", n] for n in WEIGHTS])


def _unpack_rows(gathered, shapes):
    flat = gathered.reshape(N_DEV, -1)
    out, off = [], 0
    for s in shapes:
        n = int(np.prod(s))
        out.append(flat[:, off:off + n].reshape((N_DEV,) + tuple(s)))
        off += n
    return out
```

```python
import functools
import math

import numpy as np
import jax
import jax.numpy as jnp
from jax import lax
from jax.experimental import pallas as pl
from jax.experimental.pallas import tpu as pltpu

F32 = jnp.float32
BF16 = jnp.bfloat16

RMS_EPS = 1e-6
ATTN_HEAD_DIM = 64
ATTN_GROUP = 8
ATTN_BLOCK = 128
SSM_GROUP_CH = 16
SSM_STATE = 64
MLA_NOPE = 128
MLA_ROPE = 64
MLA_V = 128
ROPE_THETA = 10000.0
ADAM_LR = 0.001
ADAM_B1 = 0.9
ADAM_B2 = 0.999
ADAM_EPS = 1e-08
ADAM_WD = 0.01
ADAM_STEP = 10

N_DEV = 8
LANES = 128
SUBLANES = 8
VMEM_LIMIT_BYTES = 50 * 2 ** 20
MASK_VALUE = -1e30
LOG2E = math.log2(math.e)
SEGS = SUBLANES


def _params(*sem):
    return pltpu.CompilerParams(dimension_semantics=sem, vmem_limit_bytes=VMEM_LIMIT_BYTES)


def _divisor(n, cap, align):
    best = None
    d = align
    while d <= min(n, cap):
        if n % d == 0:
            best = d
        d += align
    return best if best is not None else n


def _layered(w):
    return w if isinstance(w, tuple) else (w[None], 0)


def _nblocks(w):
    return _layered(w)[0].shape[1]


def mm_nn(a, w, out_dtype, name, tm=1024, tn_cap=1408, parts=1, rider=None):
    M, K = a.shape
    w, layer = _layered(w)
    _, nb, K2, n = w.shape
    assert K == K2
    tm = _divisor(M, tm, 16)
    tn = n if nb > 1 else _divisor(n // parts, tn_cap, LANES)
    jn = n // tn
    J = nb * jn
    assert J % parts == 0
    jp = J // parts

    def body(a_ref, w_ref, o_ref):
        o_ref[...] = jnp.dot(a_ref[...].astype(BF16), w_ref[...], preferred_element_type=F32).astype(o_ref.dtype)

    if parts == 1:
        out_shape = jax.ShapeDtypeStruct((M, nb * n), out_dtype)
        out_spec = pl.BlockSpec((tm, tn), lambda i, j: (i, j))
    else:
        out_shape = jax.ShapeDtypeStruct((parts, M, nb * n // parts), out_dtype)
        out_spec = pl.BlockSpec((None, tm, tn), lambda i, j: (j // jp, i, j % jp))
    res = carried_call(
        body, rider, name, (M // tm, J),
        [pl.BlockSpec((tm, K), lambda i, j: (i, 0)),
         pl.BlockSpec((None, None, K, tn), lambda i, j: (layer, j // jn, 0, j % jn))],
        [out_spec], [out_shape], [], [a, w], ("parallel", "arbitrary"))
    return res[0] if rider is None else (res[0], res[1:])


def mm_nt(a, w, out_dtype, name, tm=1024, tk=1024, tc_cap=2816):
    w, layer = _layered(w)
    _, nb, K, n = w.shape
    split = a.ndim == 3
    M = a.shape[-2]
    tm = _divisor(M, tm, 16)
    tk = _divisor(K, tk, LANES)
    P = a.shape[0] if split else 1
    tc = n if nb > 1 else _divisor(n // P, tc_cap, LANES)
    jn = n // tc
    J = nb * jn
    if split:
        P = a.shape[0]
        assert J % P == 0 and a.shape[2] * P == nb * n
        jp = J // P
        a_spec = pl.BlockSpec((None, tm, tc), lambda i, k, j: (j // jp, i, j % jp))
    else:
        assert a.shape[1] == nb * n
        a_spec = pl.BlockSpec((tm, tc), lambda i, k, j: (i, j))

    def body(a_ref, w_ref, o_ref, acc_ref):
        j = pl.program_id(2)
        part = lax.dot_general(a_ref[...].astype(BF16), w_ref[...], (((1,), (1,)), ((), ())),
                               preferred_element_type=F32)

        @pl.when(j == 0)
        def _():
            acc_ref[...] = part

        @pl.when(j > 0)
        def _():
            acc_ref[...] += part

        @pl.when(j == J - 1)
        def _():
            o_ref[...] = acc_ref[...].astype(o_ref.dtype)

    return pl.pallas_call(
        body, name=name, out_shape=jax.ShapeDtypeStruct((M, K), out_dtype), grid=(M // tm, K // tk, J),
        in_specs=[a_spec, pl.BlockSpec((None, None, tk, tc), lambda i, k, j: (layer, j // jn, k, j % jn))],
        out_specs=pl.BlockSpec((tm, tk), lambda i, k, j: (i, k)),
        scratch_shapes=[pltpu.VMEM((tm, tk), F32)],
        compiler_params=_params("parallel", "parallel", "arbitrary"),
    )(a, w)


def mm_tn(a, b, nb, out_dtype, name, tm=1024, tka=1024, tn_cap=1408):
    M, Ka = a.shape
    split = b.ndim == 3
    N = b.shape[-1] * (b.shape[0] if split else 1)
    n = N // nb
    tm = _divisor(M, tm, 16)
    tka = _divisor(Ka, tka, LANES)
    tn = n if nb > 1 else _divisor(n // (b.shape[0] if split else 1), tn_cap, LANES)
    jn = n // tn
    J = nb * jn
    steps = M // tm
    if split:
        P = b.shape[0]
        assert J % P == 0
        jp = J // P
        b_spec = pl.BlockSpec((None, tm, tn), lambda k, j, i: (j // jp, i, j % jp))
    else:
        b_spec = pl.BlockSpec((tm, tn), lambda k, j, i: (i, j))

    def body(a_ref, b_ref, o_ref, acc_ref):
        i = pl.program_id(2)
        part = lax.dot_general(a_ref[...].astype(BF16), b_ref[...].astype(BF16), (((0,), (0,)), ((), ())),
                               preferred_element_type=F32)

        @pl.when(i == 0)
        def _():
            acc_ref[...] = part

        @pl.when(i > 0)
        def _():
            acc_ref[...] += part

        @pl.when(i == steps - 1)
        def _():
            o_ref[...] = acc_ref[...].astype(o_ref.dtype)

    return pl.pallas_call(
        body, name=name, out_shape=jax.ShapeDtypeStruct((nb, Ka, n), out_dtype), grid=(Ka // tka, J, steps),
        in_specs=[pl.BlockSpec((tm, tka), lambda k, j, i: (i, k)), b_spec],
        out_specs=pl.BlockSpec((None, tka, tn), lambda k, j, i: (j // jn, k, j % jn)),
        scratch_shapes=[pltpu.VMEM((tka, tn), F32)],
        compiler_params=_params("parallel", "parallel", "arbitrary"),
    )(a, b)


def rms_fwd(x, g, out_dtype, name, ts=512):
    S, D = x.shape
    ts = _divisor(S, ts, 16)

    def body(x_ref, g_ref, h_ref, r_ref):
        x = x_ref[...]
        r = lax.rsqrt(jnp.mean(x * x, axis=-1, keepdims=True) + RMS_EPS)
        h_ref[...] = ((x * r) * g_ref[...]).astype(h_ref.dtype)
        r_ref[...] = r

    return pl.pallas_call(
        body, name=name,
        out_shape=(jax.ShapeDtypeStruct((S, D), out_dtype), jax.ShapeDtypeStruct((S, 1), F32)),
        grid=(S // ts,),
        in_specs=[pl.BlockSpec((ts, D), lambda i: (i, 0)), pl.BlockSpec((1, D), lambda i: (0, 0))],
        out_specs=(pl.BlockSpec((ts, D), lambda i: (i, 0)), pl.BlockSpec((ts, 1), lambda i: (i, 0))),
        compiler_params=_params("parallel"),
    )(x, g)


def rms_bwd(dh, x, rstd, g, name, dres=None, ts=512):
    S, D = x.shape
    ts = _divisor(S, ts, 16)
    has_res = dres is not None

    def body(*refs):
        if has_res:
            dh_ref, x_ref, r_ref, g_ref, res_ref, dx_ref, dg_ref = refs
        else:
            dh_ref, x_ref, r_ref, g_ref, dx_ref, dg_ref = refs
        i = pl.program_id(0)
        dh = dh_ref[...].astype(F32)
        r = r_ref[...]
        xhat = x_ref[...] * r
        dhg = dh * g_ref[...]
        c = jnp.mean(dhg * xhat, axis=-1, keepdims=True)
        dx = r * (dhg - xhat * c)
        if has_res:
            dx = dx + res_ref[...]
        dx_ref[...] = dx
        part = jnp.sum(dh * xhat, axis=0, keepdims=True)

        @pl.when(i == 0)
        def _():
            dg_ref[...] = part

        @pl.when(i > 0)
        def _():
            dg_ref[...] += part

    row = pl.BlockSpec((ts, D), lambda i: (i, 0))
    args = [dh, x, rstd, g] + ([dres] if has_res else [])
    specs = [row, row, pl.BlockSpec((ts, 1), lambda i: (i, 0)), pl.BlockSpec((1, D), lambda i: (0, 0))]
    specs += [row] if has_res else []
    return pl.pallas_call(
        body, name=name,
        out_shape=(jax.ShapeDtypeStruct((S, D), F32), jax.ShapeDtypeStruct((1, D), F32)),
        grid=(S // ts,), in_specs=specs,
        out_specs=(row, pl.BlockSpec((1, D), lambda i: (0, 0))),
        compiler_params=_params("arbitrary"),
    )(*args)


def ew(fn, name, mats, rows, out_dtypes, n_sums=0, ts=512, tc=1024):
    S, C = mats[0].shape
    ts = _divisor(S, ts, 16)
    tc = _divisor(C, tc, LANES)
    n_in = len(mats) + len(rows)
    n_out = len(out_dtypes)

    def body(*refs):
        i = pl.program_id(1)
        res = fn(*[r[...] for r in refs[:n_in]])
        if not isinstance(res, (tuple, list)):
            res = (res,)
        for o_ref, val in zip(refs[n_in:n_in + n_out], res[:n_out]):
            o_ref[...] = val.astype(o_ref.dtype)
        for s_ref, val in zip(refs[n_in + n_out:], res[n_out:]):
            part = jnp.sum(val, axis=0, keepdims=True)

            @pl.when(i == 0)
            def _():
                s_ref[...] = part

            @pl.when(i > 0)
            def _():
                s_ref[...] += part

    mat = pl.BlockSpec((ts, tc), lambda j, i: (i, j))
    row = pl.BlockSpec((1, tc), lambda j, i: (0, j))
    out_shape = tuple(jax.ShapeDtypeStruct((S, C), d) for d in out_dtypes)
    out_shape += tuple(jax.ShapeDtypeStruct((1, C), F32) for _ in range(n_sums))
    return pl.pallas_call(
        body, name=name, out_shape=out_shape, grid=(C // tc, S // ts),
        in_specs=[mat] * len(mats) + [row] * len(rows),
        out_specs=tuple([mat] * n_out + [row] * n_sums),
        compiler_params=_params("parallel", "arbitrary"),
    )(*mats, *rows)


def _halo_specs(ts, tc, S, lead):
    nblk = S // SUBLANES
    per = ts // SUBLANES
    pre = (None,) * 0
    if lead:
        main = pl.BlockSpec((lead, ts, tc), lambda j, i: (0, i, j))
        prev = pl.BlockSpec((lead, SUBLANES, tc), lambda j, i: (0, jnp.maximum(i * per - 1, 0), j))
        nxt = pl.BlockSpec((lead, SUBLANES, tc), lambda j, i: (0, jnp.minimum((i + 1) * per, nblk - 1), j))
    else:
        main = pl.BlockSpec((ts, tc), lambda j, i: (i, j))
        prev = pl.BlockSpec((SUBLANES, tc), lambda j, i: (jnp.maximum(i * per - 1, 0), j))
        nxt = pl.BlockSpec((SUBLANES, tc), lambda j, i: (jnp.minimum((i + 1) * per, nblk - 1), j))
    return main, prev, nxt


def _extended(prev, main, nxt, i, ts, S):
    before = jnp.where(i > 0, prev.astype(F32), 0.0)
    after = jnp.where((i + 1) * ts < S, nxt.astype(F32), 0.0)
    return jnp.concatenate([before, main.astype(F32), after], axis=0)


def _taps(ue):
    n = ue.shape[0]
    return pltpu.roll(ue, 1, axis=0), ue, pltpu.roll(ue, n - 1, axis=0)


def _conv3(taps, w, b):
    return b + (w[0:1] * taps[0] + w[1:2] * taps[1] + w[2:3] * taps[2])


def convgate_fwd(u3, conv_w, conv_b, name, ts=512, tc=512):
    _, S, F = u3.shape
    ts = _divisor(S, ts, 16)
    tc = _divisor(F, tc, LANES)
    main, prev, nxt = _halo_specs(ts, tc, S, 2)

    def body(m_ref, p_ref, n_ref, w_ref, b_ref, o_ref):
        i = pl.program_id(1)
        c = []
        for h in range(2):
            ue = _extended(p_ref[h], m_ref[h], n_ref[h], i, ts, S)
            c.append(_conv3(_taps(ue), w_ref[h], b_ref[h])[SUBLANES:SUBLANES + ts])
        o_ref[...] = (jax.nn.silu(c[0]) * c[1]).astype(o_ref.dtype)

    return pl.pallas_call(
        body, name=name, out_shape=jax.ShapeDtypeStruct((S, F), BF16), grid=(F // tc, S // ts),
        in_specs=[main, prev, nxt, pl.BlockSpec((2, 3, tc), lambda j, i: (0, 0, j)),
                  pl.BlockSpec((2, 1, tc), lambda j, i: (0, 0, j))],
        out_specs=pl.BlockSpec((ts, tc), lambda j, i: (i, j)),
        compiler_params=_params("parallel", "arbitrary"),
    )(u3, u3, u3, conv_w, conv_b)


def convgate_bwd(u3, da, conv_w, conv_b, name, ts=512, tc=512, rider=None):
    _, S, F = u3.shape
    ts = _divisor(S, ts, 16)
    tc = _divisor(F, tc, LANES)
    main, prev, nxt = _halo_specs(ts, tc, S, 2)
    amain, aprev, anxt = _halo_specs(ts, tc, S, 0)
    n = ts + 2 * SUBLANES
    mid = slice(SUBLANES, SUBLANES + ts)

    def body(m_ref, p_ref, n_ref, am_ref, ap_ref, an_ref, w_ref, b_ref, du_ref, dw_ref, db_ref):
        i = pl.program_id(1)
        ue = [_extended(p_ref[h], m_ref[h], n_ref[h], i, ts, S) for h in range(2)]
        shifted = [_taps(ue[h]) for h in range(2)]
        g = _conv3(shifted[0], w_ref[0], b_ref[0])
        v = _conv3(shifted[1], w_ref[1], b_ref[1])
        dae = _extended(ap_ref[...], am_ref[...], an_ref[...], i, ts, S)
        sg = jax.nn.sigmoid(g)
        dc = [dae * v * (sg * (1.0 + g * (1.0 - sg))), dae * (g * sg)]
        for h in range(2):
            w = w_ref[h]
            du = w[0:1] * pltpu.roll(dc[h], n - 1, axis=0) + w[1:2] * dc[h] + w[2:3] * pltpu.roll(dc[h], 1, axis=0)
            du_ref[h] = du[mid].astype(du_ref.dtype)
            dcm = dc[h][mid]
            sums = [jnp.sum(dcm * t[mid], axis=0, keepdims=True) for t in shifted[h]]
            db = jnp.sum(dcm, axis=0, keepdims=True)

            @pl.when(i == 0)
            def _():
                for t in range(3):
                    dw_ref[h, t:t + 1, :] = sums[t]
                db_ref[h] = db

            @pl.when(i > 0)
            def _():
                for t in range(3):
                    dw_ref[h, t:t + 1, :] += sums[t]
                db_ref[h] += db

    res = carried_call(
        body, rider, name, (F // tc, S // ts),
        [main, prev, nxt, amain, aprev, anxt, pl.BlockSpec((2, 3, tc), lambda j, i: (0, 0, j)),
         pl.BlockSpec((2, 1, tc), lambda j, i: (0, 0, j))],
        [pl.BlockSpec((2, ts, tc), lambda j, i: (0, i, j)), pl.BlockSpec((2, 3, tc), lambda j, i: (0, 0, j)),
         pl.BlockSpec((2, 1, tc), lambda j, i: (0, 0, j))],
        [jax.ShapeDtypeStruct((2, S, F), BF16), jax.ShapeDtypeStruct((2, 3, F), F32),
         jax.ShapeDtypeStruct((2, 1, F), F32)], [], [u3, u3, u3, da, da, da, conv_w, conv_b],
        ("parallel", "arbitrary"))
    return tuple(res[:3]) if rider is None else (res[0], res[1], res[2], res[3:])


def ffn_fwd(x, norm_g, w_up, conv_w, conv_b, w_down, tag, rider=None):
    h, rstd = rms_fwd(x, norm_g, BF16, f"ffn_norm_{tag}")
    u3 = mm_nn(h, w_up, F32, f"ffn_up_{tag}", parts=2, rider=rider)
    if rider is not None:
        u3, got = u3
    a = convgate_fwd(u3, conv_w, conv_b, f"ffn_gate_{tag}")
    y = mm_nn_res(a, w_down, x, f"ffn_down_{tag}")
    return (y, (x, h, rstd, u3, a)) if rider is None else (y, (x, h, rstd, u3, a), got)


def mm_nn_res(a, w, res, name, tm=1024, tn=512):
    M, K = a.shape
    w, layer = _layered(w)
    N = w.shape[3]
    assert w.shape[1] == 1
    tm = _divisor(M, tm, 16)
    tn = _divisor(N, tn, LANES)

    def body(a_ref, w_ref, r_ref, o_ref):
        o_ref[...] = r_ref[...] + jnp.dot(a_ref[...].astype(BF16), w_ref[...], preferred_element_type=F32)

    return pl.pallas_call(
        body, name=name, out_shape=jax.ShapeDtypeStruct((M, N), F32), grid=(M // tm, N // tn),
        in_specs=[pl.BlockSpec((tm, K), lambda i, j: (i, 0)),
                  pl.BlockSpec((None, None, K, tn), lambda i, j: (layer, 0, 0, j)),
                  pl.BlockSpec((tm, tn), lambda i, j: (i, j))],
        out_specs=pl.BlockSpec((tm, tn), lambda i, j: (i, j)),
        compiler_params=_params("parallel", "arbitrary"),
    )(a, w, res)


def ffn_bwd(dy, saved, norm_g, w_up, conv_w, conv_b, w_down, tag, rider=None):
    x, h, rstd, u3, a = saved
    nb = _nblocks(w_up)
    dw_down = mm_tn(a, dy, 1, BF16, f"ffn_dwdown_{tag}", tka=1408)
    da = mm_nt(dy, w_down, F32, f"ffn_da_{tag}", tk=512)
    du3, dconv_w, dconv_b, *got = convgate_bwd(u3, da, conv_w, conv_b, f"ffn_dgate_{tag}", rider=rider)
    dh = mm_nt(du3, w_up, F32, f"ffn_dh_{tag}")
    dw_up = mm_tn(h, du3, nb, BF16, f"ffn_dwup_{tag}")
    dx, dnorm = rms_bwd(dh, x, rstd, norm_g, f"ffn_dnorm_{tag}", dres=dy)
    grads = (dnorm, dw_up, dconv_w, dconv_b, dw_down)
    return (dx, grads) if rider is None else (dx, grads, got[0])


def _swa_bias(n_heads):
    kv = n_heads // ATTN_GROUP
    slopes = jnp.asarray((2.0 ** (-8.0 * np.arange(1, n_heads + 1) / n_heads)).astype(np.float32))
    rel = jnp.arange(SWA_KEYS)[None, :] - ATTN_BLOCK - jnp.arange(ATTN_BLOCK)[:, None]
    dist = jnp.abs(rel).astype(F32)
    bias = (-slopes.reshape(kv, ATTN_GROUP, 1, 1) * dist) * LOG2E
    col = jnp.arange(SWA_KEYS)
    out = []
    for dead in (col < ATTN_BLOCK, col < 0, col >= 2 * ATTN_BLOCK):
        keep = (jnp.abs(rel) <= ATTN_BLOCK) & ~dead[None, :]
        out.append(jnp.where(keep[None, None], bias, MASK_VALUE).reshape(kv, SWA_ROWS, SWA_KEYS))
    return jnp.stack(out)


SWA_ROWS = ATTN_GROUP * ATTN_BLOCK
SWA_KEYS = 3 * ATTN_BLOCK
SWA_LOGIT_SCALE = ATTN_HEAD_DIM ** -0.5 * LOG2E


def _swa_probs(q, kc, bias, sink):
    reps = (1, SWA_KEYS // LANES)
    s = lax.dot_general(q, kc, (((1,), (1,)), ((), ())), preferred_element_type=F32) * SWA_LOGIT_SCALE + bias
    m = jnp.maximum(jnp.max(s, axis=1, keepdims=True), sink)
    p = jnp.exp2(s - jnp.tile(m, reps))
    es = jnp.exp2(sink - m)
    r = 1.0 / (jnp.sum(p, axis=1, keepdims=True) + es)
    return p * jnp.tile(r, reps), es * r


def _swa_specs(S):
    nb = S // ATTN_BLOCK

    def at(off):
        return pl.BlockSpec((None, ATTN_BLOCK, ATTN_HEAD_DIM),
                            lambda c, n: (c, jnp.clip(jnp.minimum(n, nb - 1) + off, 0, nb - 1), 0))

    qspec = pl.BlockSpec((None, ATTN_GROUP, ATTN_BLOCK, ATTN_HEAD_DIM), lambda c, n: (c, 0, jnp.minimum(n, nb - 1), 0))
    bias = pl.BlockSpec((None, None, SWA_ROWS, SWA_KEYS),
                        lambda c, n: (jnp.where(n == 0, 0, jnp.where(n >= nb - 1, 2, 1)), c, 0, 0))
    sink = pl.BlockSpec((None, SWA_ROWS, LANES), lambda c, n: (c, 0, 0))
    return qspec, [at(-1), at(0), at(1)], bias, sink


def swa_fwd(q, k, v, bias, sink, name, rider=None):
    KV, G, S, dh = q.shape
    nb = S // ATTN_BLOCK
    qspec, kspecs, bspec, sspec = _swa_specs(S)

    def body(q_ref, k0, k1, k2, v0, v1, v2, b_ref, s_ref, o_ref):
        kc = jnp.concatenate([k0[...], k1[...], k2[...]], axis=0)
        vc = jnp.concatenate([v0[...], v1[...], v2[...]], axis=0)
        p, _ = _swa_probs(q_ref[...].reshape(SWA_ROWS, dh), kc, b_ref[...], s_ref[...])
        o = jnp.dot(p.astype(BF16), vc, preferred_element_type=F32)
        o_ref[...] = o.reshape(G, ATTN_BLOCK, dh).astype(o_ref.dtype)

    res = carried_call(
        body, rider, name, (KV, nb), [qspec] + kspecs + kspecs + [bspec, sspec], [qspec],
        [jax.ShapeDtypeStruct(q.shape, BF16)], [], [q, k, k, k, v, v, v, bias, sink], ("parallel", "arbitrary"))
    return res[0] if rider is None else (res[0], res[1:])


def swa_bwd(q, k, v, bias, sink, do, name, rider=None):
    KV, G, S, dh = q.shape
    nb = S // ATTN_BLOCK
    qspec, kspecs, bspec, sspec = _swa_specs(S)
    scale = ATTN_HEAD_DIM ** -0.5
    kv_out = pl.BlockSpec((None, ATTN_BLOCK, dh), lambda c, n: (c, jnp.maximum(n - 1, 0), 0))

    def body(q_ref, k0, k1, k2, v0, v1, v2, b_ref, s_ref, do_ref, dq_ref, dk_ref, dv_ref, ds_ref, dk_acc, dv_acc):
        n = pl.program_id(1)

        @pl.when(n == 0)
        def _():
            dk_acc[...] = jnp.zeros_like(dk_acc)
            dv_acc[...] = jnp.zeros_like(dv_acc)
            ds_ref[...] = jnp.zeros_like(ds_ref)

        @pl.when(n > 0)
        def _():
            dk_acc[(n + 1) % 3] = jnp.zeros((ATTN_BLOCK, dh), F32)
            dv_acc[(n + 1) % 3] = jnp.zeros((ATTN_BLOCK, dh), F32)

        @pl.when(n < nb)
        def _():
            kc = jnp.concatenate([k0[...], k1[...], k2[...]], axis=0)
            vc = jnp.concatenate([v0[...], v1[...], v2[...]], axis=0)
            qa = q_ref[...].reshape(SWA_ROWS, dh)
            da = do_ref[...].reshape(SWA_ROWS, dh)
            p, psink = _swa_probs(qa, kc, b_ref[...], s_ref[...])
            dp = lax.dot_general(da, vc, (((1,), (1,)), ((), ())), preferred_element_type=F32)
            delta = jnp.sum(p * dp, axis=1, keepdims=True) + jnp.zeros((SWA_ROWS, LANES), F32)
            ds = ((p * (dp - jnp.tile(delta, (1, SWA_KEYS // LANES)))) * scale).astype(BF16)
            dq = jnp.dot(ds, kc, preferred_element_type=F32)
            dq_ref[...] = dq.reshape(G, ATTN_BLOCK, dh).astype(dq_ref.dtype)
            dkc = lax.dot_general(ds, qa, (((0,), (0,)), ((), ())), preferred_element_type=F32)
            dvc = lax.dot_general(p.astype(BF16), da, (((0,), (0,)), ((), ())), preferred_element_type=F32)
            ds_ref[...] -= psink * delta
            for o in range(3):
                slot = (n + 2 + o) % 3
                dk_acc[slot] += dkc[o * ATTN_BLOCK:(o + 1) * ATTN_BLOCK]
                dv_acc[slot] += dvc[o * ATTN_BLOCK:(o + 1) * ATTN_BLOCK]

        done = (n + 2) % 3
        dk_ref[...] = dk_acc[done].astype(dk_ref.dtype)
        dv_ref[...] = dv_acc[done].astype(dv_ref.dtype)

    res = carried_call(
        body, rider, name, (KV, nb + 1), [qspec] + kspecs + kspecs + [bspec, sspec, qspec],
        [qspec, kv_out, kv_out, sspec],
        [jax.ShapeDtypeStruct(q.shape, BF16), jax.ShapeDtypeStruct(k.shape, BF16),
         jax.ShapeDtypeStruct(v.shape, BF16), jax.ShapeDtypeStruct((KV, SWA_ROWS, LANES), F32)],
        [pltpu.VMEM((3, ATTN_BLOCK, dh), F32), pltpu.VMEM((3, ATTN_BLOCK, dh), F32)],
        [q, k, k, k, v, v, v, bias, sink, do], ("parallel", "arbitrary"))
    return tuple(res[:4]) if rider is None else (res[0], res[1], res[2], res[3], res[4:])


def _heads_major(x, n_heads):
    S = x.shape[0]
    return x.reshape(S, n_heads, -1).transpose(1, 0, 2)


def _heads_minor(x):
    H, S, dh = x.shape
    return x.transpose(1, 0, 2).reshape(S, H * dh)


def swa_layer_fwd(x, norm_g, w_qkv, w_o, sink, tag, rider=None):
    S, D = x.shape
    H = _layered(w_o)[0].shape[2] // ATTN_HEAD_DIM
    KV = H // ATTN_GROUP
    h, rstd = rms_fwd(x, norm_g, BF16, f"swa_norm_{tag}")
    qkv = mm_nn(h, w_qkv, BF16, f"swa_qkv_{tag}", tn_cap=1280)
    q = _heads_major(qkv[:, :H * ATTN_HEAD_DIM], H).reshape(KV, ATTN_GROUP, S, ATTN_HEAD_DIM)
    k = _heads_major(qkv[:, H * ATTN_HEAD_DIM:(H + KV) * ATTN_HEAD_DIM], KV)
    v = _heads_major(qkv[:, (H + KV) * ATTN_HEAD_DIM:], KV)
    bias = _swa_bias(H)
    sinkb = jnp.broadcast_to((sink.astype(F32) * LOG2E).reshape(KV, ATTN_GROUP, 1, 1),
                             (KV, ATTN_GROUP, ATTN_BLOCK, LANES)).reshape(KV, SWA_ROWS, LANES)
    o = swa_fwd(q, k, v, bias, sinkb, f"swa_attn_{tag}", rider=rider)
    if rider is not None:
        o, got = o
    o2 = _heads_minor(o.reshape(H, S, ATTN_HEAD_DIM))
    y = mm_nn_res(o2, w_o, x, f"swa_out_{tag}")
    saved = (x, h, rstd, q, k, v, bias, sinkb, o2)
    return (y, saved) if rider is None else (y, saved, got)


def swa_layer_bwd(dy, saved, norm_g, w_qkv, w_o, tag, rider=None):
    x, h, rstd, q, k, v, bias, sinkb, o2 = saved
    KV, G, S, dh = q.shape
    H = KV * G
    dw_o = mm_tn(o2, dy, 1, BF16, f"swa_dwo_{tag}")
    do2 = mm_nt(dy, w_o, BF16, f"swa_do_{tag}")
    do = _heads_major(do2, H).reshape(KV, G, S, dh)
    dq, dk, dv, dsink, *got = swa_bwd(q, k, v, bias, sinkb, do, f"swa_dattn_{tag}", rider=rider)
    dqkv = jnp.concatenate([_heads_minor(dq.reshape(H, S, dh)), _heads_minor(dk), _heads_minor(dv)], axis=1)
    dw_qkv = mm_tn(h, dqkv, 1, BF16, f"swa_dwqkv_{tag}", tn_cap=1280)
    dh_ = mm_nt(dqkv, w_qkv, F32, f"swa_dh_{tag}", tc_cap=2560)
    dx, dnorm = rms_bwd(dh_, x, rstd, norm_g, f"swa_dnorm_{tag}", dres=dy)
    grads = (dnorm, dw_qkv, dw_o, dsink[:, :, 0].reshape(H, ATTN_BLOCK).sum(axis=1))
    return (dx, grads) if rider is None else (dx, grads, got[0])


def flash_fwd(q, k, v, name, tq=1024, tk=512, rider=None):
    H, S, dk = q.shape
    dv = v.shape[-1]
    assert dv == LANES
    tq = _divisor(S, tq, 16)
    tk = _divisor(S, tk, LANES)
    nk = S // tk

    def body(q_ref, k_ref, v_ref, o_ref, lse_ref, m_sc, l_sc, acc_sc):
        m_sc[...] = jnp.full(m_sc.shape, MASK_VALUE, F32)
        l_sc[...] = jnp.zeros(l_sc.shape, F32)
        acc_sc[...] = jnp.zeros(acc_sc.shape, F32)

        def step(kt, carry):
            off = pl.multiple_of(kt * tk, tk)
            s = lax.dot_general(q_ref[...], k_ref[pl.ds(off, tk), :], (((1,), (1,)), ((), ())),
                                preferred_element_type=F32)
            m_prev = m_sc[...]
            m_new = jnp.maximum(m_prev, jnp.max(s, axis=1, keepdims=True))
            alpha = jnp.exp2(m_prev - m_new)
            p = jnp.exp2(s - jnp.tile(m_new, (1, tk // LANES)))
            l_sc[...] = alpha * l_sc[...] + jnp.sum(p, axis=1, keepdims=True)
            acc_sc[...] = alpha * acc_sc[...] + jnp.dot(p.astype(BF16), v_ref[pl.ds(off, tk), :],
                                                        preferred_element_type=F32)
            m_sc[...] = m_new
            return carry

        lax.fori_loop(0, nk, step, 0)
        l = l_sc[...]
        o_ref[...] = (acc_sc[...] / l).astype(o_ref.dtype)
        lse_ref[...] = (m_sc[...] + jnp.log2(l))[:, :1]

    res = carried_call(
        body, rider, name, (H, S // tq),
        [pl.BlockSpec((None, tq, dk), lambda h, i: (h, i, 0)),
         pl.BlockSpec((None, S, dk), lambda h, i: (h, 0, 0)),
         pl.BlockSpec((None, S, dv), lambda h, i: (h, 0, 0))],
        [pl.BlockSpec((None, tq, dv), lambda h, i: (h, i, 0)), pl.BlockSpec((None, tq, 1), lambda h, i: (h, i, 0))],
        [jax.ShapeDtypeStruct((H, S, dv), BF16), jax.ShapeDtypeStruct((H, S, 1), F32)],
        [pltpu.VMEM((tq, LANES), F32), pltpu.VMEM((tq, LANES), F32), pltpu.VMEM((tq, dv), F32)],
        [q, k, v], ("parallel", "arbitrary"))
    return (res[0], res[1]) if rider is None else (res[0], res[1], res[2:])


def flash_delta(o, do, name, ts=1024):
    H, S, dv = o.shape
    ts = _divisor(S, ts, 16)

    def body(o_ref, do_ref, d_ref):
        d_ref[...] = jnp.sum(o_ref[...].astype(F32) * do_ref[...].astype(F32), axis=-1, keepdims=True)

    spec = pl.BlockSpec((None, ts, dv), lambda h, i: (h, i, 0))
    return pl.pallas_call(
        body, name=name, out_shape=jax.ShapeDtypeStruct((H, S, 1), F32), grid=(H, S // ts),
        in_specs=[spec, spec], out_specs=pl.BlockSpec((None, ts, 1), lambda h, i: (h, i, 0)),
        compiler_params=_params("parallel", "parallel"),
    )(o, do)


def flash_bwd(q, k, v, do, lse_row, delta_row, scale, name, tq=1024, tk=512, rider=None):
    H, S, dk = q.shape
    dv = v.shape[-1]
    tq = _divisor(S, tq, LANES)
    tk = _divisor(S, tk, 16)
    nq = S // tq
    nk = S // tk

    def body(q_ref, k_ref, v_ref, do_ref, lse_ref, dl_ref, dq_ref, dk_ref, dv_ref, dk_sc, dv_sc):
        j = pl.program_id(1)

        @pl.when(j == 0)
        def _():
            dq_ref[...] = jnp.zeros(dq_ref.shape, F32)

        dk_sc[...] = jnp.zeros(dk_sc.shape, F32)
        dv_sc[...] = jnp.zeros(dv_sc.shape, F32)
        kk = k_ref[...]
        vv = v_ref[...]

        def step(t, carry):
            off = pl.multiple_of(t * tq, tq)
            qq = q_ref[pl.ds(off, tq), :]
            dd = do_ref[pl.ds(off, tq), :]
            st = lax.dot_general(kk, qq, (((1,), (1,)), ((), ())), preferred_element_type=F32)
            pt = jnp.exp2(st - lse_ref[:, pl.ds(off, tq)])
            dpt = lax.dot_general(vv, dd, (((1,), (1,)), ((), ())), preferred_element_type=F32)
            dst = (pt * (dpt - dl_ref[:, pl.ds(off, tq)])).astype(BF16)
            dv_sc[...] += jnp.dot(pt.astype(BF16), dd, preferred_element_type=F32)
            dk_sc[...] += jnp.dot(dst, qq, preferred_element_type=F32)
            dq_ref[pl.ds(off, tq), :] += lax.dot_general(dst, kk, (((0,), (0,)), ((), ())),
                                                         preferred_element_type=F32)
            return carry

        lax.fori_loop(0, nq, step, 0)
        dk_ref[...] = (dk_sc[...] * (1.0 / LOG2E)).astype(dk_ref.dtype)
        dv_ref[...] = dv_sc[...].astype(dv_ref.dtype)

        @pl.when(j == nk - 1)
        def _():
            dq_ref[...] = dq_ref[...] * scale

    ks = pl.BlockSpec((None, tk, dk), lambda h, j: (h, j, 0))
    vs = pl.BlockSpec((None, tk, dv), lambda h, j: (h, j, 0))
    row = pl.BlockSpec((None, 1, S), lambda h, j: (h, 0, 0))
    qs = pl.BlockSpec((None, S, dk), lambda h, j: (h, 0, 0))
    res = carried_call(
        body, rider, name, (H, S // tk),
        [qs, ks, vs, pl.BlockSpec((None, S, dv), lambda h, j: (h, 0, 0)), row, row], [qs, ks, vs],
        [jax.ShapeDtypeStruct((H, S, dk), F32), jax.ShapeDtypeStruct((H, S, dk), F32),
         jax.ShapeDtypeStruct((H, S, dv), BF16)],
        [pltpu.VMEM((tk, dk), F32), pltpu.VMEM((tk, dv), F32)], [q, k, v, do, lse_row, delta_row],
        ("parallel", "arbitrary"))
    return tuple(res[:3]) if rider is None else (res[0], res[1], res[2], res[3:])


def _rope_tables(S, reps):
    half = MLA_ROPE // 2
    pos = jnp.arange(S, dtype=F32)
    inv = ROPE_THETA ** (-jnp.arange(half, dtype=F32) / half)
    ang = pos[:, None] * inv[None, :]
    return jnp.tile(jnp.cos(ang), (1, reps)), jnp.tile(jnp.sin(ang), (1, reps))


def _rotate(x1, x2, cos, sin, out_dtype, name, gain=None):
    if gain is None:
        return ew(lambda a, b, c, s: (a * c - b * s, b * c + a * s), name, [x1, x2, cos, sin], [],
                  [out_dtype, out_dtype])
    return ew(lambda a, b, c, s, g: ((a * c - b * s) * g, (b * c + a * s) * g), name, [x1, x2, cos, sin], [gain],
              [out_dtype, out_dtype])


def mla_layer_fwd(x, norm_g, w_dqkv, q_norm, kv_norm, w_uq, w_ukv, w_o, tag, rider=None):
    S, D = x.shape
    QL, KL = q_norm.shape[1], kv_norm.shape[1]
    H = _layered(w_o)[0].shape[2] // MLA_V
    half = MLA_ROPE // 2
    h, rstd = rms_fwd(x, norm_g, BF16, f"mla_norm_{tag}")
    d = mm_nn(h, w_dqkv, F32, f"mla_down_{tag}")
    c_q, c_kv, k_rope = d[:, :QL], d[:, QL:QL + KL], d[:, QL + KL:]
    cq_n, rstd_q = rms_fwd(c_q, q_norm, BF16, f"mla_qnorm_{tag}")
    ckv_n, rstd_kv = rms_fwd(c_kv, kv_norm, BF16, f"mla_kvnorm_{tag}")
    q = mm_nn(cq_n, w_uq, F32, f"mla_uq_{tag}").reshape(S, H, MLA_NOPE + MLA_ROPE)
    kv = mm_nn(ckv_n, w_ukv, BF16, f"mla_ukv_{tag}").reshape(S, H, MLA_NOPE + MLA_V)
    cos, sin = _rope_tables(S, H + 1)
    x1 = jnp.concatenate([q[:, :, MLA_NOPE:MLA_NOPE + half].reshape(S, H * half), k_rope[:, :half]], axis=1)
    x2 = jnp.concatenate([q[:, :, MLA_NOPE + half:].reshape(S, H * half), k_rope[:, half:]], axis=1)
    logit_gain = (MLA_NOPE + MLA_ROPE) ** -0.5 * LOG2E
    gain = jnp.concatenate([jnp.full((1, H * half), logit_gain, F32), jnp.ones((1, half), F32)], axis=1)
    r1, r2 = _rotate(x1, x2, cos, sin, BF16, f"mla_rope_{tag}", gain=gain)
    qr = jnp.concatenate([r1[:, :H * half].reshape(S, H, half), r2[:, :H * half].reshape(S, H, half)], axis=2)
    kr = jnp.concatenate([r1[:, H * half:], r2[:, H * half:]], axis=1)
    qh = jnp.concatenate([(q[:, :, :MLA_NOPE] * logit_gain).astype(BF16), qr], axis=2).transpose(1, 0, 2)
    kh = jnp.concatenate([kv[:, :, :MLA_NOPE], jnp.broadcast_to(kr[:, None, :], (S, H, MLA_ROPE))],
                         axis=2).transpose(1, 0, 2)
    vh = kv[:, :, MLA_NOPE:].transpose(1, 0, 2)
    oh, lse, *got = flash_fwd(qh, kh, vh, f"mla_attn_{tag}", rider=rider)
    o2 = _heads_minor(oh)
    y = mm_nn_res(o2, w_o, x, f"mla_out_{tag}")
    saved = (x, h, rstd, c_q, rstd_q, cq_n, c_kv, rstd_kv, ckv_n, qh, kh, vh, oh, lse, o2, cos, sin)
    return (y, saved) if rider is None else (y, saved, got[0])


def mla_layer_bwd(dy, saved, norm_g, w_dqkv, q_norm, kv_norm, w_uq, w_ukv, w_o, tag, rider=None):
    x, h, rstd, c_q, rstd_q, cq_n, c_kv, rstd_kv, ckv_n, qh, kh, vh, oh, lse, o2, cos, sin = saved
    H, S, _ = qh.shape
    half = MLA_ROPE // 2
    scale = (MLA_NOPE + MLA_ROPE) ** -0.5
    dw_o = mm_tn(o2, dy, 1, BF16, f"mla_dwo_{tag}")
    do2 = mm_nt(dy, w_o, BF16, f"mla_do_{tag}")
    doh = _heads_major(do2, H)
    delta = flash_delta(oh, doh, f"mla_delta_{tag}")
    dqh, dkh, dvh, *got = flash_bwd(qh, kh, vh, doh, lse.reshape(H, 1, S), delta.reshape(H, 1, S), scale,
                                    f"mla_dattn_{tag}", rider=rider)
    dq = dqh.transpose(1, 0, 2)
    dk = dkh.transpose(1, 0, 2)
    dkr = jnp.sum(dk[:, :, MLA_NOPE:], axis=1)
    g1 = jnp.concatenate([dq[:, :, MLA_NOPE:MLA_NOPE + half].reshape(S, H * half), dkr[:, :half]], axis=1)
    g2 = jnp.concatenate([dq[:, :, MLA_NOPE + half:].reshape(S, H * half), dkr[:, half:]], axis=1)
    b1, b2 = _rotate(g1, g2, cos, -sin, F32, f"mla_drope_{tag}")
    dq_rope = jnp.concatenate([b1[:, :H * half].reshape(S, H, half), b2[:, :H * half].reshape(S, H, half)], axis=2)
    dk_rope = jnp.concatenate([b1[:, H * half:], b2[:, H * half:]], axis=1)
    dq_full = jnp.concatenate([dq[:, :, :MLA_NOPE], dq_rope], axis=2).reshape(S, -1).astype(BF16)
    dkv = jnp.concatenate([dk[:, :, :MLA_NOPE].astype(BF16), dvh.transpose(1, 0, 2)], axis=2).reshape(S, -1)
    dw_uq = mm_tn(cq_n, dq_full, _nblocks(w_uq), BF16, f"mla_dwuq_{tag}")
    dw_ukv = mm_tn(ckv_n, dkv, _nblocks(w_ukv), BF16, f"mla_dwukv_{tag}")
    dcq_n = mm_nt(dq_full, w_uq, F32, f"mla_dcq_{tag}")
    dckv_n = mm_nt(dkv, w_ukv, F32, f"mla_dckv_{tag}")
    dc_q, dq_norm = rms_bwd(dcq_n, c_q, rstd_q, q_norm, f"mla_dqnorm_{tag}")
    dc_kv, dkv_norm = rms_bwd(dckv_n, c_kv, rstd_kv, kv_norm, f"mla_dkvnorm_{tag}")
    dd = jnp.concatenate([dc_q, dc_kv, dk_rope], axis=1).astype(BF16)
    dw_dqkv = mm_tn(h, dd, 1, BF16, f"mla_dwdown_{tag}")
    dh_ = mm_nt(dd, w_dqkv, F32, f"mla_dh_{tag}")
    dx, dnorm = rms_bwd(dh_, x, rstd, norm_g, f"mla_dnorm_{tag}", dres=dy)
    grads = (dnorm, dw_dqkv, dq_norm, dkv_norm, dw_uq, dw_ukv, dw_o)
    return (dx, grads) if rider is None else (dx, grads, got[0])


SLAB = LANES
SLAB_GROUPS = SLAB // SSM_GROUP_CH
SLAB_HALF = SLAB_GROUPS * SSM_STATE
SLAB_W = 2 * SLAB_HALF


def _scan_rows(st_ref, carry_ref, lam_ref, nt, rev):
    h = SLAB_HALF
    lr = lam_ref[:, :h]
    li = lam_ref[:, h:]

    def step(i, c):
        xr, xi = c
        ii = (nt - 1 - i) if rev else i
        row = pl.multiple_of(ii * SEGS, SEGS)
        nr = lr * xr - li * xi + st_ref[pl.ds(row, SEGS), :h]
        ni = lr * xi + li * xr + st_ref[pl.ds(row, SEGS), h:]
        st_ref[pl.ds(row, SEGS), :h] = nr
        st_ref[pl.ds(row, SEGS), h:] = ni
        return nr, ni

    xr, xi = lax.fori_loop(0, nt, step, (carry_ref[:, :h], carry_ref[:, h:]), unroll=4)
    carry_ref[:, :h] = xr
    carry_ref[:, h:] = xi


def s5_scan(mode, inp, win, lam, rev, name, init=None, wout=None, xs=None, xinit=None, u=None, rows=1024,
            rider=None):
    T, C = inp.shape
    K = win.shape[0]
    W = SLAB_W
    Tc = _divisor(T, rows, 16)
    nt = Tc // SEGS
    nT = T // Tc
    tiles = T // SEGS

    def chunk(jj):
        return (nT - 1 - jj) if rev else jj

    slab_in = pl.BlockSpec((Tc, SLAB), lambda k, jj: (chunk(jj), k))
    wspec = pl.BlockSpec((None, SLAB, W), lambda k, jj: (k, 0, 0))
    vspec = pl.BlockSpec((None, SEGS, W), lambda k, jj: (k, 0, 0))
    wospec = pl.BlockSpec((None, W, SLAB), lambda k, jj: (k, 0, 0))
    xspec = pl.BlockSpec((Tc, W), lambda k, jj: (chunk(jj), k))
    scratch = [pltpu.VMEM((Tc, W), F32), pltpu.VMEM((SEGS, W), F32)]
    sem = _params("parallel", "arbitrary")

    def project_in(in_ref, w_ref, st_ref):
        st_ref[...] = jnp.dot(in_ref[...].astype(BF16), w_ref[...], preferred_element_type=F32)

    if mode == "finals":
        def body(in_ref, w_ref, lam_ref, fin_ref, st_ref, carry_ref):
            jj = pl.program_id(1)

            @pl.when(jj == 0)
            def _():
                carry_ref[...] = jnp.zeros_like(carry_ref)

            project_in(in_ref, w_ref, st_ref)
            _scan_rows(st_ref, carry_ref, lam_ref, nt, rev)

            @pl.when(jj == nT - 1)
            def _():
                fin_ref[...] = carry_ref[...]

        return pl.pallas_call(
            body, name=name, out_shape=jax.ShapeDtypeStruct((K, SEGS, W), F32), grid=(K, nT),
            in_specs=[slab_in, wspec, vspec], out_specs=vspec, scratch_shapes=scratch, compiler_params=sem,
        )(inp, win, lam)

    if mode == "fwd":
        def body(in_ref, w_ref, lam_ref, init_ref, wo_ref, xs_ref, y_ref, st_ref, carry_ref):
            jj = pl.program_id(1)

            @pl.when(jj == 0)
            def _():
                carry_ref[...] = init_ref[...]

            project_in(in_ref, w_ref, st_ref)
            _scan_rows(st_ref, carry_ref, lam_ref, nt, rev)
            xs = st_ref[...]
            xs_ref[...] = xs
            y_ref[...] = jnp.dot(xs.astype(BF16), wo_ref[...], preferred_element_type=F32)

        res = carried_call(
            body, rider, name, (K, nT), [slab_in, wspec, vspec, vspec, wospec], [xspec, slab_in],
            [jax.ShapeDtypeStruct((T, K * W), F32), jax.ShapeDtypeStruct((T, C), F32)], scratch,
            [inp, win, lam, init, wout], ("parallel", "arbitrary"))
        return (res[0], res[1]) if rider is None else (res[0], res[1], res[2:])

    assert mode == "bwd"
    x_rev = not rev

    def halo_index(k, jj):
        ch = chunk(jj)
        tile = jnp.minimum((ch + 1) * nt, tiles - 1) if x_rev else jnp.maximum(ch * nt - 1, 0)
        return (tile, k)

    halo = pl.BlockSpec((SEGS, W), halo_index)

    def body(in_ref, w_ref, lam_ref, init_ref, wo_ref, xs_ref, xh_ref, xi_ref, u_ref,
             du_ref, dwin_ref, dwout_ref, dlam_ref, st_ref, carry_ref):
        jj = pl.program_id(1)
        ch = chunk(jj)

        @pl.when(jj == 0)
        def _():
            carry_ref[...] = init_ref[...]

        g = in_ref[...].astype(BF16)
        st_ref[...] = jnp.dot(g, w_ref[...], preferred_element_type=F32)
        _scan_rows(st_ref, carry_ref, lam_ref, nt, rev)
        adj = st_ref[...]
        adj16 = adj.astype(BF16)
        du_ref[...] = jnp.dot(adj16, wo_ref[...], preferred_element_type=F32)
        xs = xs_ref[...]
        edge = (ch == nT - 1) if x_rev else (ch == 0)
        first = jnp.where(edge, xi_ref[...], xh_ref[...])
        if x_rev:
            xp = jnp.concatenate([xs[SEGS:], first], axis=0)
        else:
            xp = jnp.concatenate([first, xs[:Tc - SEGS]], axis=0)
        h = SLAB_HALF
        ar, ai, pr, pi = adj[:, :h], adj[:, h:], xp[:, :h], xp[:, h:]
        dlr = (ar * pr + ai * pi).reshape(nt, SEGS, h).sum(axis=0)
        dli = (ai * pr - ar * pi).reshape(nt, SEGS, h).sum(axis=0)
        dwin = lax.dot_general(u_ref[...].astype(BF16), adj16, (((0,), (0,)), ((), ())), preferred_element_type=F32)
        dwout = lax.dot_general(xs.astype(BF16), g, (((0,), (0,)), ((), ())), preferred_element_type=F32)

        @pl.when(jj == 0)
        def _():
            dwin_ref[...] = dwin
            dwout_ref[...] = dwout
            dlam_ref[:, :h] = dlr
            dlam_ref[:, h:] = dli

        @pl.when(jj > 0)
        def _():
            dwin_ref[...] += dwin
            dwout_ref[...] += dwout
            dlam_ref[:, :h] += dlr
            dlam_ref[:, h:] += dli

    res = carried_call(
        body, rider, name, (K, nT), [slab_in, wspec, vspec, vspec, wospec, xspec, halo, vspec, slab_in],
        [slab_in, wspec, wospec, vspec],
        [jax.ShapeDtypeStruct((T, C), F32), jax.ShapeDtypeStruct((K, SLAB, W), F32),
         jax.ShapeDtypeStruct((K, W, SLAB), F32), jax.ShapeDtypeStruct((K, SEGS, W), F32)], scratch,
        [inp, win, lam, init, wout, xs, xs, xinit, u], ("parallel", "arbitrary"))
    return tuple(res[:4]) if rider is None else (res[0], res[1], res[2], res[3], res[4:])


def _s5_discretize(a_re, a_im, log_step, b_re, b_im):
    step = jnp.exp(log_step)[:, None]
    mag = jnp.exp(step * a_re)
    lb_re = mag * jnp.cos(step * a_im)
    lb_im = mag * jnp.sin(step * a_im)
    n_re, n_im = lb_re - 1.0, lb_im
    den = a_re * a_re + a_im * a_im
    coef_re = (n_re * a_re + n_im * a_im) / den
    coef_im = (n_im * a_re - n_re * a_im) / den
    bb_re = coef_re[..., None] * b_re - coef_im[..., None] * b_im
    bb_im = coef_re[..., None] * b_im + coef_im[..., None] * b_re
    return lb_re, lb_im, bb_re, bb_im


def _slab_in_matrix(bb_re, bb_im):
    G, N, Cg = bb_re.shape
    K = G // SLAB_GROUPS
    eye = jnp.eye(SLAB_GROUPS, dtype=F32)
    parts = [jnp.einsum('kgnc,gh->kgchn', b.reshape(K, SLAB_GROUPS, N, Cg), eye).reshape(K, SLAB, SLAB_HALF)
             for b in (bb_re, bb_im)]
    return jnp.concatenate(parts, axis=2)


def _slab_in_unpack(m):
    K = m.shape[0]
    m6 = m.reshape(K, SLAB_GROUPS, SSM_GROUP_CH, 2, SLAB_GROUPS, SSM_STATE)
    d = jnp.einsum('kgcphn,gh->pkgnc', m6, jnp.eye(SLAB_GROUPS, dtype=F32))
    d = d.reshape(2, K * SLAB_GROUPS, SSM_STATE, SSM_GROUP_CH)
    return d[0], d[1]


def _slab_out_matrix(c_re, c_im):
    G, Cg, N = c_re.shape
    K = G // SLAB_GROUPS
    eye = jnp.eye(SLAB_GROUPS, dtype=F32)
    parts = [jnp.einsum('kgcn,gh->kgnhc', c.reshape(K, SLAB_GROUPS, Cg, N), eye).reshape(K, SLAB_HALF, SLAB)
             for c in (c_re, -c_im)]
    return jnp.concatenate(parts, axis=1)


def _slab_out_unpack(m):
    K = m.shape[0]
    m6 = m.reshape(K, 2, SLAB_GROUPS, SSM_STATE, SLAB_GROUPS, SSM_GROUP_CH)
    d = jnp.einsum('kpgnhc,gh->pkgcn', m6, jnp.eye(SLAB_GROUPS, dtype=F32))
    d = d.reshape(2, K * SLAB_GROUPS, SSM_GROUP_CH, SSM_STATE)
    return d[0], -d[1]


def _slab_vec(re, im):
    K = re.shape[0] // SLAB_GROUPS
    v = jnp.concatenate([re.reshape(K, SLAB_HALF), im.reshape(K, SLAB_HALF)], axis=1)
    return jnp.broadcast_to(v[:, None, :], (K, SEGS, SLAB_W))


def _segment_inits(fin, lam, seg_len, rev):
    h = SLAB_HALF
    pr, pi = lam[:, 0, :h], lam[:, 0, h:]
    steps = int(round(math.log2(seg_len)))
    assert 2 ** steps == seg_len
    for _ in range(steps):
        pr, pi = pr * pr - pi * pi, 2.0 * pr * pi
    cr = jnp.zeros_like(pr)
    ci = jnp.zeros_like(pi)
    inits = [None] * SEGS
    for s in (range(SEGS - 1, -1, -1) if rev else range(SEGS)):
        inits[s] = jnp.concatenate([cr, ci], axis=1)
        cr, ci = pr * cr - pi * ci + fin[:, s, :h], pr * ci + pi * cr + fin[:, s, h:]
    return jnp.stack(inits, axis=1)


def _time_permute(x):
    T, C = x.shape
    return x.reshape(SEGS, T // SEGS, C).transpose(1, 0, 2).reshape(T, C)


def _time_unpermute(x):
    T, C = x.shape
    return x.reshape(T // SEGS, SEGS, C).transpose(1, 0, 2).reshape(T, C)


_GELU_K = math.sqrt(2.0 / math.pi)
_GELU_A = 0.044715


def _gelu_grad(y):
    t = jnp.tanh(_GELU_K * (y + _GELU_A * y * y * y))
    return 0.5 * (1.0 + t) + 0.5 * y * (1.0 - t * t) * (_GELU_K * (1.0 + 3.0 * _GELU_A * y * y))


def _conj(lam):
    return jnp.concatenate([lam[:, :, :SLAB_HALF], -lam[:, :, SLAB_HALF:]], axis=2)


def s5_layer_fwd(x, norm_g, ssm, w_glu, tag, rider=None):
    S, D = x.shape
    u_nat, rstd = rms_fwd(x, norm_g, F32, f"s5_norm_{tag}")
    u = _time_permute(u_nat)
    dirs = []
    ys = []
    received = None
    for dr in range(2):
        rev = dr == 1
        lb_re, lb_im, bb_re, bb_im = _s5_discretize(ssm["a_re"][dr], ssm["a_im"][dr], ssm["log_step"][dr],
                                                    ssm["b_re"][dr], ssm["b_im"][dr])
        win = _slab_in_matrix(bb_re, bb_im).astype(BF16)
        wout = _slab_out_matrix(ssm["c_re"][dr], ssm["c_im"][dr]).astype(BF16)
        lam = _slab_vec(lb_re, lb_im)
        fin = s5_scan("finals", u, win, lam, rev, f"s5_fin{dr}_{tag}")
        init = _segment_inits(fin, lam, S // SEGS, rev)
        xs, y, *got = s5_scan("fwd", u, win, lam, rev, f"s5_fwd{dr}_{tag}", init=init, wout=wout,
                              rider=rider if dr == 0 else None)
        received = got[0] if got else received
        dirs.append((win, wout, lam, init, xs))
        ys.append(y)
    yy, zb = ew(lambda uu, a, b, d: (d * uu + a + b, jax.nn.gelu(d * uu + a + b)), f"s5_y_{tag}",
                [u, ys[0], ys[1]], [ssm["d"]], [F32, BF16])
    lin = mm_nn(zb, w_glu, F32, f"s5_glu_{tag}", tn_cap=512)
    mix = ew(lambda y_, l_, b: jax.nn.gelu(y_) * jax.nn.sigmoid(l_ + b), f"s5_mix_{tag}",
             [yy, lin], [ssm["b_glu"]], [F32])[0]
    out = x + _time_unpermute(mix)
    saved = (x, rstd, u, dirs, yy, zb, lin)
    return (out, saved) if rider is None else (out, saved, received)


def s5_layer_bwd(dy, saved, norm_g, ssm, w_glu, tag, rider=None):
    x, rstd, u, dirs, yy, zb, lin = saved
    S, D = x.shape
    dmix = _time_permute(dy)

    def glu_back(dm, y_, l_, b):
        z = jax.nn.gelu(y_)
        sg = jax.nn.sigmoid(l_ + b)
        dlin = dm * z * (sg * (1.0 - sg))
        return dlin, dm * sg, dlin

    dlin, dz_direct, db_glu = ew(glu_back, f"s5_dmix_{tag}", [dmix, yy, lin], [ssm["b_glu"]], [BF16, F32], n_sums=1)
    dw_glu = mm_tn(zb, dlin, 1, BF16, f"s5_dwglu_{tag}", tn_cap=512)
    dz_mm = mm_nt(dlin, w_glu, F32, f"s5_dz_{tag}")

    def gelu_back(a, b, y_, uu):
        dyy = (a + b) * _gelu_grad(y_)
        return dyy, dyy * uu

    dyy, dd = ew(gelu_back, f"s5_dy_{tag}", [dz_direct, dz_mm, yy, u], [], [F32], n_sums=1)
    grads = {"d": dd, "b_glu": db_glu, "w_glu": dw_glu}
    dus = []
    per_dir = []
    received = None
    for dr in range(2):
        rev = dr == 1
        win, wout, lam, xinit, xs = dirs[dr]
        lamc = _conj(lam)
        ein = wout.transpose(0, 2, 1)
        eout = win.transpose(0, 2, 1)
        fin = s5_scan("finals", dyy, ein, lamc, not rev, f"s5_bfin{dr}_{tag}")
        init = _segment_inits(fin, lamc, S // SEGS, not rev)
        du, dwin, dwout, dlam, *got = s5_scan("bwd", dyy, ein, lamc, not rev, f"s5_bwd{dr}_{tag}", init=init,
                                              wout=eout, xs=xs, xinit=xinit, u=u, rider=rider if dr == 0 else None)
        received = got[0] if got else received
        dus.append(du)
        dbb_re, dbb_im = _slab_in_unpack(dwin)
        dc_re, dc_im = _slab_out_unpack(dwout)
        dl = dlam.sum(axis=1)
        dlb_re = dl[:, :SLAB_HALF].reshape(-1, SSM_STATE)
        dlb_im = dl[:, SLAB_HALF:].reshape(-1, SSM_STATE)
        prm = (ssm["a_re"][dr], ssm["a_im"][dr], ssm["log_step"][dr], ssm["b_re"][dr], ssm["b_im"][dr])
        _, vjp = jax.vjp(_s5_discretize, *prm)
        per_dir.append(vjp((dlb_re, dlb_im, dbb_re, dbb_im)) + (dc_re, dc_im))
    for i, nm in enumerate(["a_re", "a_im", "log_step", "b_re", "b_im", "c_re", "c_im"]):
        grads[nm] = jnp.stack([per_dir[0][i], per_dir[1][i]], axis=0)
    du_p = ew(lambda g, a, b, d: d * g + a + b, f"s5_du_{tag}", [dyy, dus[0], dus[1]], [ssm["d"]], [F32])[0]
    dx, dnorm = rms_bwd(_time_unpermute(du_p), x, rstd, norm_g, f"s5_dnorm_{tag}", dres=dy)
    grads["norm"] = dnorm
    return (dx, grads) if rider is None else (dx, grads, received)


def final_loss(x, g, target, name, ts=512):
    S, D = x.shape
    ts = _divisor(S, ts, 16)

    def body(x_ref, g_ref, t_ref, loss_ref, dx_ref, dg_ref):
        i = pl.program_id(0)
        x = x_ref[...]
        gg = g_ref[...]
        r = lax.rsqrt(jnp.mean(x * x, axis=-1, keepdims=True) + RMS_EPS)
        xhat = x * r
        err = xhat * gg - t_ref[...]
        row_loss = jnp.mean(err * err, axis=-1, keepdims=True)
        part = jnp.broadcast_to(0.5 * jnp.sum(row_loss, axis=0, keepdims=True), (1, LANES))
        dy = err * (1.0 / D)
        dhg = dy * gg
        c = jnp.mean(dhg * xhat, axis=-1, keepdims=True)
        dx_ref[...] = r * (dhg - xhat * c)
        dg = jnp.sum(dy * xhat, axis=0, keepdims=True)

        @pl.when(i == 0)
        def _():
            loss_ref[...] = part
            dg_ref[...] = dg

        @pl.when(i > 0)
        def _():
            loss_ref[...] += part
            dg_ref[...] += dg

    row = pl.BlockSpec((ts, D), lambda i: (i, 0))
    vec = pl.BlockSpec((1, D), lambda i: (0, 0))
    return pl.pallas_call(
        body, name=name,
        out_shape=(jax.ShapeDtypeStruct((1, LANES), F32), jax.ShapeDtypeStruct((S, D), F32),
                   jax.ShapeDtypeStruct((1, D), F32)),
        grid=(S // ts,), in_specs=[row, vec, row],
        out_specs=(pl.BlockSpec((1, LANES), lambda i: (0, 0)), row, vec),
        compiler_params=_params("arbitrary"),
    )(x, g, target)


FLAT_W = 8 * LANES


def _adamw_math(w, g, m, v):
    m = ADAM_B1 * m + (1.0 - ADAM_B1) * g
    v = ADAM_B2 * v + (1.0 - ADAM_B2) * (g * g)
    m_hat = m / (1.0 - ADAM_B1 ** ADAM_STEP)
    v_hat = v / (1.0 - ADAM_B2 ** ADAM_STEP)
    delta = -ADAM_LR * (m_hat / (jnp.sqrt(v_hat) + ADAM_EPS) + ADAM_WD * w)
    return delta, m, v


def _ordered_sum(parts_ref):
    total = parts_ref[0].astype(F32)
    for s in range(1, N_DEV):
        total = total + parts_ref[s].astype(F32)
    return total


ADAMW_BLOCK_ELEMS = 256 * 1024


def adamw_from_parts(parts, w, m, v, name):
    R, B = w.shape
    tr = _divisor(R, max(16, ADAMW_BLOCK_ELEMS // B), 16)

    def body(p_ref, w_ref, m_ref, v_ref, g_ref, d_ref, nm_ref, nv_ref):
        g = _ordered_sum(p_ref)
        delta, nm, nv = _adamw_math(w_ref[...], g, m_ref[...], v_ref[...])
        g_ref[...] = g
        d_ref[...] = delta
        nm_ref[...] = nm
        nv_ref[...] = nv

    flat = pl.BlockSpec((tr, B), lambda i: (i, 0))
    out = jax.ShapeDtypeStruct((R, B), F32)
    return pl.pallas_call(
        body, name=name, out_shape=(out, out, out, out), grid=(R // tr,),
        in_specs=[pl.BlockSpec((N_DEV, tr, B), lambda i: (0, i, 0)), flat, flat, flat],
        out_specs=(flat, flat, flat, flat), compiler_params=_params("parallel"),
    )(parts, w, m, v)


def sum_parts(parts, name, tr=512):
    R = parts.shape[1]
    tr = _divisor(R, tr, 16)

    def body(p_ref, o_ref):
        o_ref[...] = _ordered_sum(p_ref)

    return pl.pallas_call(
        body, name=name, out_shape=jax.ShapeDtypeStruct((R, FLAT_W), F32), grid=(R // tr,),
        in_specs=[pl.BlockSpec((N_DEV, tr, FLAT_W), lambda i: (0, i, 0))],
        out_specs=pl.BlockSpec((tr, FLAT_W), lambda i: (i, 0)), compiler_params=_params("parallel"),
    )(parts)


def adamw_flat(g, w, m, v, name, tr=512):
    R = w.shape[0]
    tr = _divisor(R, tr, 16)

    def body(g_ref, w_ref, m_ref, v_ref, d_ref, nm_ref, nv_ref):
        delta, nm, nv = _adamw_math(w_ref[...], g_ref[...], m_ref[...], v_ref[...])
        d_ref[...] = delta
        nm_ref[...] = nm
        nv_ref[...] = nv

    flat = pl.BlockSpec((tr, FLAT_W), lambda i: (i, 0))
    out = jax.ShapeDtypeStruct((R, FLAT_W), F32)
    return pl.pallas_call(
        body, name=name, out_shape=(out, out, out), grid=(R // tr,),
        in_specs=[flat, flat, flat, flat], out_specs=(flat, flat, flat), compiler_params=_params("parallel"),
    )(g, w, m, v)


MESH_ID = pl.DeviceIdType.MESH
HBM_SPEC = pl.BlockSpec(memory_space=pltpu.HBM)


def _position():
    x, y, c = lax.axis_index("x"), lax.axis_index("y"), lax.axis_index("c")
    return x, y, c


def _flat_index(px, py, pc):
    return 4 * px + 2 * py + pc


def all_gather(arrays, axes, name):
    n = len(arrays)

    def body(*refs):
        ins, outs = refs[:n], refs[n:2 * n]
        send_sems, recv_sems, local_sems = refs[2 * n:]
        x, y, c = _position()
        me, sibling = (x, y, c), (x, y, 1 - c)
        chips = [(1 - x, y), (x, 1 - y), (1 - x, 1 - y)]

        def block_of(a, pos):
            idx = _flat_index(*pos)
            return outs[a].at[:, idx] if axes[a] == 1 else outs[a].at[idx]

        def copy(a, k, block, to, src=None):
            rows = block_of(a, block)
            return pltpu.make_async_remote_copy(
                src_ref=rows if src is None else src, dst_ref=rows,
                send_sem=send_sems.at[7 * a + k], recv_sem=recv_sems.at[7 * a + k],
                device_id=to, device_id_type=MESH_ID)

        mine, first, passed = [], [], []
        for a in range(n):
            cp = pltpu.make_async_copy(ins[a], block_of(a, me), local_sems.at[a])
            cp.start()
            mine.append(cp)
            first.append(copy(a, 0, me, sibling, src=ins[a]))
            first += [copy(a, 1 + j, me, (*chip, c), src=ins[a]) for j, chip in enumerate(chips)]
        for cp in first:
            cp.start()
        for a in range(n):
            for j, chip in enumerate(chips):
                copy(a, 1 + j, (*chip, c), me).wait_recv()
                fwd = copy(a, 4 + j, (*chip, c), sibling)
                fwd.start()
                passed.append(fwd)
        for a in range(n):
            copy(a, 0, sibling, me).wait_recv()
            for j, chip in enumerate(chips):
                copy(a, 4 + j, (*chip, 1 - c), me).wait_recv()
        for cp in first + passed:
            cp.wait_send()
        for cp in mine:
            cp.wait()

    return pl.pallas_call(
        body, name=name,
        out_shape=tuple(jax.ShapeDtypeStruct(a.shape[:ax] + (N_DEV,) + a.shape[ax:], a.dtype)
                        for a, ax in zip(arrays, axes)),
        in_specs=[HBM_SPEC] * n, out_specs=tuple([HBM_SPEC] * n),
        scratch_shapes=[pltpu.SemaphoreType.DMA((7 * n,)), pltpu.SemaphoreType.DMA((7 * n,)),
                        pltpu.SemaphoreType.DMA((n,))],
    )(*arrays)


def exchange(slotted, whole, name):
    n = len(slotted) + 1

    def body(*refs):
        srcs, dsts = refs[:n], refs[n:2 * n]
        send_sems, recv_sems, local_sems = refs[2 * n:]
        x, y, c = _position()
        me = _flat_index(x, y, c)

        def source(a, slot):
            return srcs[a].at[slot] if a < n - 1 else srcs[a]

        own = [pltpu.make_async_copy(source(a, me), dsts[a].at[me], local_sems.at[a]) for a in range(n)]
        for cp in own:
            cp.start()
        sends, recvs = [], []
        for r in range(1, N_DEV):
            peer = (1 - x if r & 4 else x, 1 - y if r & 2 else y, 1 - c if r & 1 else c)
            pidx = _flat_index(*peer)
            for a in range(n):
                k = 7 * a + r - 1
                sends.append(pltpu.make_async_remote_copy(
                    src_ref=source(a, pidx), dst_ref=dsts[a].at[me], send_sem=send_sems.at[k],
                    recv_sem=recv_sems.at[k], device_id=peer, device_id_type=MESH_ID))
                recvs.append(pltpu.make_async_remote_copy(
                    src_ref=source(a, pidx), dst_ref=dsts[a].at[pidx], send_sem=send_sems.at[k],
                    recv_sem=recv_sems.at[k], device_id=peer, device_id_type=MESH_ID))
        for cp in sends:
            cp.start()
        for cp in recvs:
            cp.wait_recv()
        for cp in sends:
            cp.wait_send()
        for cp in own:
            cp.wait()

    return pl.pallas_call(
        body, name=name,
        out_shape=tuple(jax.ShapeDtypeStruct(s.shape, s.dtype) for s in slotted)
        + (jax.ShapeDtypeStruct((N_DEV,) + whole.shape, whole.dtype),),
        in_specs=[HBM_SPEC] * n, out_specs=tuple([HBM_SPEC] * n),
        scratch_shapes=[pltpu.SemaphoreType.DMA((7 * n,)), pltpu.SemaphoreType.DMA((7 * n,)),
                        pltpu.SemaphoreType.DMA((n,))],
    )(*slotted, whole)


class RidingGather:
    def __init__(self, arrays):
        self.arrays = list(arrays)
        self.n = len(self.arrays)
        self.out_shapes = [jax.ShapeDtypeStruct((N_DEV,) + a.shape, a.dtype) for a in self.arrays]
        self.sems = [pltpu.SemaphoreType.DMA((7 * self.n,)), pltpu.SemaphoreType.DMA((7 * self.n,)),
                     pltpu.SemaphoreType.DMA((self.n,))]

    def _copies(self, ins, outs, sems):
        send_sems, recv_sems, local_sems = sems
        x, y, c = _position()
        me, sibling = (x, y, c), (x, y, 1 - c)
        chips = [(1 - x, y), (x, 1 - y), (1 - x, 1 - y)]

        def copy(a, k, block, to, src=None):
            rows = outs[a].at[_flat_index(*block)]
            return pltpu.make_async_remote_copy(
                src_ref=rows if src is None else src, dst_ref=rows,
                send_sem=send_sems.at[7 * a + k], recv_sem=recv_sems.at[7 * a + k],
                device_id=to, device_id_type=MESH_ID)

        mine = [pltpu.make_async_copy(ins[a], outs[a].at[_flat_index(*me)], local_sems.at[a]) for a in range(self.n)]
        first = []
        for a in range(self.n):
            first.append(copy(a, 0, me, sibling, src=ins[a]))
            first += [copy(a, 1 + j, me, (*chip, c), src=ins[a]) for j, chip in enumerate(chips)]
        return copy, mine, first, me, sibling, chips, c

    def start(self, ins, outs, sems):
        _, mine, first, *_ = self._copies(ins, outs, sems)
        for cp in mine + first:
            cp.start()

    def finish(self, ins, outs, sems):
        copy, mine, first, me, sibling, chips, c = self._copies(ins, outs, sems)
        passed = []
        for a in range(self.n):
            for j, chip in enumerate(chips):
                copy(a, 1 + j, (*chip, c), me).wait_recv()
                fwd = copy(a, 4 + j, (*chip, c), sibling)
                fwd.start()
                passed.append(fwd)
        for a in range(self.n):
            copy(a, 0, sibling, me).wait_recv()
            for j, chip in enumerate(chips):
                copy(a, 4 + j, (*chip, 1 - c), me).wait_recv()
        for cp in first + passed:
            cp.wait_send()
        for cp in mine:
            cp.wait()


class RidingExchange:
    def __init__(self, arrays):
        self.arrays = list(arrays)
        self.n = len(self.arrays)
        self.out_shapes = [jax.ShapeDtypeStruct(a.shape, a.dtype) for a in self.arrays]
        self.sems = [pltpu.SemaphoreType.DMA((7 * self.n,)), pltpu.SemaphoreType.DMA((7 * self.n,)),
                     pltpu.SemaphoreType.DMA((self.n,))]

    def _copies(self, ins, outs, sems):
        send_sems, recv_sems, local_sems = sems
        x, y, c = _position()
        me = _flat_index(x, y, c)
        own = [pltpu.make_async_copy(ins[a].at[me], outs[a].at[me], local_sems.at[a]) for a in range(self.n)]
        sends, recvs = [], []
        for r in range(1, N_DEV):
            peer = (1 - x if r & 4 else x, 1 - y if r & 2 else y, 1 - c if r & 1 else c)
            pidx = _flat_index(*peer)
            for a in range(self.n):
                k = 7 * a + r - 1
                sends.append(pltpu.make_async_remote_copy(
                    src_ref=ins[a].at[pidx], dst_ref=outs[a].at[me], send_sem=send_sems.at[k],
                    recv_sem=recv_sems.at[k], device_id=peer, device_id_type=MESH_ID))
                recvs.append(pltpu.make_async_remote_copy(
                    src_ref=ins[a].at[pidx], dst_ref=outs[a].at[pidx], send_sem=send_sems.at[k],
                    recv_sem=recv_sems.at[k], device_id=peer, device_id_type=MESH_ID))
        return own, sends, recvs

    def start(self, ins, outs, sems):
        own, sends, _ = self._copies(ins, outs, sems)
        for cp in own + sends:
            cp.start()

    def finish(self, ins, outs, sems):
        own, sends, recvs = self._copies(ins, outs, sems)
        for cp in recvs:
            cp.wait_recv()
        for cp in sends:
            cp.wait_send()
        for cp in own:
            cp.wait()


def carried_call(body, rider, name, grid, in_specs, out_specs, out_shape, scratch_shapes, args, semantics):
    in_specs, out_specs, out_shape = list(in_specs), list(out_specs), list(out_shape)
    scratch_shapes, args = list(scratch_shapes), list(args)
    if rider is None:
        return pl.pallas_call(body, name=name, grid=grid, in_specs=in_specs, out_specs=tuple(out_specs),
                              out_shape=tuple(out_shape), scratch_shapes=scratch_shapes,
                              compiler_params=_params(*semantics))(*args)
    n_in, n_out, n_scr, n = len(in_specs), len(out_specs), len(scratch_shapes), rider.n

    def riding(*refs):
        ins, srcs = refs[:n_in], refs[n_in:n_in + n]
        outs, dsts = refs[n_in + n:n_in + n + n_out], refs[n_in + n + n_out:n_in + 2 * n + n_out]
        scr = refs[n_in + 2 * n + n_out:]
        scratch, sems = scr[:n_scr], scr[n_scr:]
        first = functools.reduce(jnp.logical_and, [pl.program_id(d) == 0 for d in range(len(grid))])
        last = functools.reduce(jnp.logical_and, [pl.program_id(d) == g - 1 for d, g in enumerate(grid)])

        @pl.when(first)
        def _():
            rider.start(srcs, dsts, sems)

        body(*ins, *outs, *scratch)

        @pl.when(last)
        def _():
            rider.finish(srcs, dsts, sems)

    return pl.pallas_call(
        riding, name=name, grid=grid, in_specs=in_specs + [HBM_SPEC] * n,
        out_specs=tuple(out_specs + [HBM_SPEC] * n), out_shape=tuple(out_shape + rider.out_shapes),
        scratch_shapes=scratch_shapes + rider.sems,
        compiler_params=_params(*["arbitrary"] * len(grid)))(*args, *rider.arrays)


WEIGHTS = ['mix_norm', 'ffn_norm', 'final_norm', 'attn_w_qkv', 'attn_w_o', 'attn_sink', 'ssm_a_re', 'ssm_a_im',
           'ssm_log_step', 'ssm_b_re', 'ssm_b_im', 'ssm_c_re', 'ssm_c_im', 'ssm_d', 'ssm_w_glu', 'ssm_b_glu',
           'mla_w_dqkv', 'mla_q_norm', 'mla_kv_norm', 'mla_w_uq', 'mla_w_ukv', 'mla_w_o', 'ffn_w_up', 'ffn_conv_w',
           'ffn_conv_b', 'ffn_w_down']
BIG = [('attn_w_qkv', 'col'), ('attn_w_o', 'row'), ('ssm_w_glu', 'row'), ('mla_w_dqkv', 'row'), ('mla_w_uq', 'col'),
       ('mla_w_ukv', 'col'), ('mla_w_o', 'row'), ('ffn_w_up', 'col'), ('ffn_w_down', 'row')]
BIG_NAMES = [n for n, _ in BIG]
MIXER_WEIGHTS = {0: ['attn_w_qkv', 'attn_w_o'], 1: ['ssm_w_glu'],
                 2: ['mla_w_dqkv', 'mla_w_uq', 'mla_w_ukv', 'mla_w_o']}
FFN_WEIGHTS = ['ffn_w_up', 'ffn_w_down']
SMALL_SHARDED = ['mla_q_norm', 'mla_kv_norm', 'ffn_conv_w']
SMALL = [n for n in WEIGHTS if n not in BIG_NAMES]


def _pack(arrays, dtype):
    flat = jnp.concatenate([a.astype(dtype).reshape(-1) for a in arrays])
    pad = (-flat.shape[0]) % (16 * FLAT_W)
    if pad:
        flat = jnp.concatenate([flat, jnp.zeros((pad,), dtype)])
    return flat.reshape(-1, FLAT_W)


def _unpack(flat, shapes):
    flat = flat.reshape(-1)
    out, off = [], 0
    for s in shapes:
        n = int(np.prod(s))
        out.append(flat[off:off + n].reshape(s))
        off += n
    return out


def _own_slice(full, idx):
    n = full.shape[-1] // N_DEV
    return lax.dynamic_slice_in_dim(full, idx * n, n, axis=full.ndim - 1)


def kernel(x, mix_norm, ffn_norm, final_norm, attn_w_qkv, attn_w_o, attn_sink, ssm_a_re, ssm_a_im, ssm_log_step, ssm_b_re, ssm_b_im, ssm_c_re, ssm_c_im, ssm_d, ssm_w_glu, ssm_b_glu, mla_w_dqkv, mla_q_norm, mla_kv_norm, mla_w_uq, mla_w_ukv, mla_w_o, ffn_w_up, ffn_conv_w, ffn_conv_b, ffn_w_down, loss_target, m_mix_norm, m_ffn_norm, m_final_norm, m_attn_w_qkv, m_attn_w_o, m_attn_sink, m_ssm_a_re, m_ssm_a_im, m_ssm_log_step, m_ssm_b_re, m_ssm_b_im, m_ssm_c_re, m_ssm_c_im, m_ssm_d, m_ssm_w_glu, m_ssm_b_glu, m_mla_w_dqkv, m_mla_q_norm, m_mla_kv_norm, m_mla_w_uq, m_mla_w_ukv, m_mla_w_o, m_ffn_w_up, m_ffn_conv_w, m_ffn_conv_b, m_ffn_w_down, v_mix_norm, v_ffn_norm, v_final_norm, v_attn_w_qkv, v_attn_w_o, v_attn_sink, v_ssm_a_re, v_ssm_a_im, v_ssm_log_step, v_ssm_b_re, v_ssm_b_im, v_ssm_c_re, v_ssm_c_im, v_ssm_d, v_ssm_w_glu, v_ssm_b_glu, v_mla_w_dqkv, v_mla_q_norm, v_mla_kv_norm, v_mla_w_uq, v_mla_w_ukv, v_mla_w_o, v_ffn_w_up, v_ffn_conv_w, v_ffn_conv_b, v_ffn_w_down):
    given = dict(locals())
    idx = _flat_index(*_position())
    depth = mix_norm.shape[0]
    xs = x[0]
    S, D = xs.shape

    kinds = dict(BIG)

    def layer_weights(i):
        return [(n, i // 3) for n in MIXER_WEIGHTS[i % 3]] + [(n, i) for n in FFN_WEIGHTS]

    def shard(name, j):
        return given[name][j].astype(BF16)

    def whole_weight(name, g):
        a, b = g.shape[1:]
        if kinds[name] == 'row':
            return (g.reshape(1, 1, N_DEV * a, b), 0)
        if b % LANES:
            return (g.transpose(1, 0, 2).reshape(1, 1, a, N_DEV * b), 0)
        return (g[None], 0)

    small_flat = _pack([given[n] for n in SMALL_SHARDED], F32)
    first = [(n, 0) for n in MIXER_WEIGHTS[0]]
    gathered = all_gather([shard(n, j) for n, j in first] + [small_flat], [0] * (len(first) + 1), "gather_l0")
    W = {0: {n: whole_weight(n, g) for (n, _), g in zip(first, gathered)}}
    sm = _unpack_rows(gathered[-1], [given[n].shape for n in SMALL_SHARDED])
    q_norm_full = sm[0][:, 0].reshape(1, -1)
    kv_norm_full = sm[1][:, 0].reshape(1, -1)
    conv_w_full = sm[2].transpose(1, 2, 0, 3).reshape(depth, 3, -1)
    F = conv_w_full.shape[2] // 2

    def conv_params(i):
        cw = conv_w_full[i].reshape(3, 2, F).transpose(1, 0, 2)
        cb = ffn_conv_b[i].reshape(2, 1, F)
        return cw, cb

    ssm = lambda j: {"a_re": ssm_a_re[j], "a_im": ssm_a_im[j], "log_step": ssm_log_step[j], "b_re": ssm_b_re[j],
                     "b_im": ssm_b_im[j], "c_re": ssm_c_re[j], "c_im": ssm_c_im[j], "d": ssm_d[j][None],
                     "b_glu": ssm_b_glu[j][None]}

    def mixer_fwd(i, cur, rider=None):
        kind, j, tag, w = i % 3, i // 3, f"l{i}", W[i]
        if kind == 0:
            return swa_layer_fwd(cur, mix_norm[i][None], w['attn_w_qkv'], w['attn_w_o'], attn_sink[j], tag,
                                 rider=rider)
        if kind == 1:
            return s5_layer_fwd(cur, mix_norm[i][None], ssm(j), w['ssm_w_glu'], tag)
        return mla_layer_fwd(cur, mix_norm[i][None], w['mla_w_dqkv'], q_norm_full, kv_norm_full, w['mla_w_uq'],
                             w['mla_w_ukv'], w['mla_w_o'], tag)

    cur = xs
    saved = []
    for i in range(depth):
        if i == 0:
            ffn0 = [(n, 0) for n in FFN_WEIGHTS]
            cur, sv, got = mixer_fwd(0, cur, RidingGather([shard(n, j) for n, j in ffn0]))
            W[0].update({n: whole_weight(n, g) for (n, _), g in zip(ffn0, got)})
        else:
            cur, sv = mixer_fwd(i, cur)
        cw, cb = conv_params(i)
        nxt = layer_weights(i + 1) if i + 1 < depth else []
        rider = RidingGather([shard(n, j) for n, j in nxt]) if nxt else None
        res = ffn_fwd(cur, ffn_norm[i][None], W[i]['ffn_w_up'], cw, cb, W[i]['ffn_w_down'], f"l{i}", rider=rider)
        cur, fsv = res[0], res[1]
        if nxt:
            W[i + 1] = {n: whole_weight(n, g) for (n, _), g in zip(nxt, res[2])}
        saved.append((sv, fsv))
    loss_part, dcur, dfinal = final_loss(cur, final_norm[None], loss_target[0], "loss_head")

    recv = {n: [None] * given[n].shape[0] for n in BIG_NAMES}
    gs = {n: [None] * given[n].shape[0] for n in SMALL if given[n].ndim > 1}
    gs['final_norm'] = dfinal[0]

    def blocked(g, name):
        a, b = given[name].shape[1:]
        if kinds[name] == 'row':
            return g.reshape(N_DEV, a, b)
        if b % LANES:
            return g.reshape(a, N_DEV, b).transpose(1, 0, 2)
        return g

    def mixer_bwd(i, dcur, sv, rider):
        kind, j, tag, w = i % 3, i // 3, f"l{i}", W[i]
        if kind == 0:
            dcur, (dn, dwqkv, dwo, dsink), got = swa_layer_bwd(dcur, sv, mix_norm[i][None], w['attn_w_qkv'],
                                                               w['attn_w_o'], tag, rider=rider)
            gs['attn_sink'][j] = dsink
            big = [('attn_w_qkv', j, dwqkv), ('attn_w_o', j, dwo)]
        elif kind == 1:
            dcur, g5, got = s5_layer_bwd(dcur, sv, mix_norm[i][None], ssm(j), w['ssm_w_glu'], tag, rider=rider)
            dn = g5["norm"]
            for nm in ("a_re", "a_im", "log_step", "b_re", "b_im", "c_re", "c_im"):
                gs['ssm_' + nm][j] = g5[nm]
            gs['ssm_d'][j] = g5["d"][0]
            gs['ssm_b_glu'][j] = g5["b_glu"][0]
            big = [('ssm_w_glu', j, g5["w_glu"])]
        else:
            dcur, (dn, dwd, dqn, dkn, dwuq, dwukv, dwo), got = mla_layer_bwd(
                dcur, sv, mix_norm[i][None], w['mla_w_dqkv'], q_norm_full, kv_norm_full, w['mla_w_uq'],
                w['mla_w_ukv'], w['mla_w_o'], tag, rider=rider)
            gs['mla_q_norm'][j] = dqn[0]
            gs['mla_kv_norm'][j] = dkn[0]
            big = [('mla_w_dqkv', j, dwd), ('mla_w_uq', j, dwuq), ('mla_w_ukv', j, dwukv), ('mla_w_o', j, dwo)]
        gs['mix_norm'][i] = dn[0]
        return dcur, [(n, jj, blocked(g, n)) for n, jj, g in big], got

    pending = []
    early = [n for n in SMALL if n.startswith('ssm_')]
    early_parts = None
    for i in reversed(range(depth)):
        sv, fsv = saved[i]
        cw, cb = conv_params(i)
        arrays = [g for _, _, g in pending]
        rides_small = any(n == 'ssm_w_glu' for n, _, _ in pending)
        if rides_small:
            flat = _pack([jnp.stack(gs[n], axis=0) for n in early], F32)
            arrays.append(jnp.broadcast_to(flat[None], (N_DEV,) + flat.shape))
        rider = RidingExchange(arrays) if arrays else None
        res = ffn_bwd(dcur, fsv, ffn_norm[i][None], W[i]['ffn_w_up'], cw, cb, W[i]['ffn_w_down'], f"l{i}",
                      rider=rider)
        dcur, (dn, dwup, dcw, dcb, dwdn) = res[0], res[1]
        for (n, jj, _), got in zip(pending, res[2] if pending else []):
            recv[n][jj] = got
        if rides_small:
            early_parts = res[2][-1]
        gs['ffn_norm'][i] = dn[0]
        gs['ffn_conv_w'][i] = dcw.transpose(1, 0, 2).reshape(3, 2 * F)
        gs['ffn_conv_b'][i] = dcb.reshape(2 * F)
        up_g, down_g = blocked(dwup, 'ffn_w_up'), blocked(dwdn, 'ffn_w_down')
        later = i % 3 == 0 and i > 0
        dcur, pending, got = mixer_bwd(i, dcur, sv, RidingExchange([up_g] if later else [up_g, down_g]))
        recv['ffn_w_up'][i] = got[0]
        if later:
            pending = pending + [('ffn_w_down', i, down_g)]
        else:
            recv['ffn_w_down'][i] = got[1]
    grad_x = dcur[None]

    full = {n: gs[n] if n == 'final_norm' else jnp.stack(gs[n], axis=0) for n in SMALL}
    late = [n for n in SMALL if early_parts is None or n not in early]
    whole = _pack([loss_part[0, :1]] + [full[n] for n in late], F32)
    received = exchange([g for _, _, g in pending], whole, "exchange_last")
    for (n, jj, _), got in zip(pending, received):
        recv[n][jj] = got
    summed = _unpack(sum_parts(received[-1], "sum_small"), [(1,)] + [full[n].shape for n in late])
    loss = summed[0][0]
    total = dict(zip(late, summed[1:]))
    if early_parts is not None:
        total.update(zip(early, _unpack(sum_parts(early_parts, "sum_small_early"), [full[n].shape for n in early])))

    out = {}
    for n in BIG_NAMES:
        shape = given[n].shape
        rows = (shape[0] * shape[1], shape[2])
        parts = recv[n][0] if shape[0] == 1 else jnp.stack(recv[n], axis=1)
        res = adamw_from_parts(parts.reshape((N_DEV,) + rows), given[n].reshape(rows), given['m_' + n].reshape(rows),
                               given['v_' + n].reshape(rows), f"adamw_{n}")
        for key, val in zip(("grad", "delta", "new_m", "new_v"), res):
            out[key, n] = val.reshape(shape)
    small_grads = [_own_slice(total[n], idx) if n in SMALL_SHARDED else total[n] for n in SMALL]
    small_shapes = [given[n].shape for n in SMALL]
    d_s, nm_s, nv_s = adamw_flat(_pack(small_grads, F32), _pack([given[n] for n in SMALL], F32),
                                 _pack([given['m_' + n] for n in SMALL], F32),
                                 _pack([given['v_' + n] for n in SMALL], F32), "adamw_small")
    for n, g in zip(SMALL, small_grads):
        out["grad", n] = g
    for key, flat in (("delta", d_s), ("new_m", nm_s), ("new_v", nv_s)):
        for n, val in zip(SMALL, _unpack(flat, small_shapes)):
            out[key, n] = val
    return (loss, grad_x, *[out["grad", n] for n in WEIGHTS], *[out["delta", n] for n in WEIGHTS],
            *[out["new_m", n] for n in WEIGHTS], *[out["new_v", n] for n in WEIGHTS])


def _unpack_rows(gathered, shapes):
    flat = gathered.reshape(N_DEV, -1)
    out, off = [], 0
    for s in shapes:
        n = int(np.prod(s))
        out.append(flat[:, off:off + n].reshape((N_DEV,) + tuple(s)))
        off += n
    return out
```
